```python
import math
import jax, jax.numpy as jnp
from jax import lax
import numpy as np

D_MODEL = 1024
BATCH = 16
SEQ = 2048
DEPTH = 2

CHUNK = 64
Q_BLOCK = 128
EPS = 1e-6

A_HEADS = 8
A_HEAD_DIM = 64
A_LATENT = 128
IDX_HEADS = 8
IDX_DIM = 64
TOPK_MAX = 256
REL_BUCKETS = 32
REL_MAX_DIST = 128
B_HEADS = 4
B_DK = 128
B_DV = 128
GM_CHUNK = 128
GM_WIDTH = D_MODEL
GM_GROUPS = 8
MEM_TOKENS = 256
X_HEADS = 4
X_HEAD_DIM = 128
FF_DIM = ((8 * D_MODEL // 3 + 255) // 256) * 256

kernel_name = 'hybrid_dsa_hgrn2_gmlp_stream_encoder'


def rmsnorm(x, g):
    xf = x.astype(jnp.float32)
    xf = xf * lax.rsqrt(jnp.mean(xf * xf, axis=-1, keepdims=True) + EPS)
    return xf.astype(x.dtype) * g


def layernorm(x, g, b):
    xf = x.astype(jnp.float32)
    mu = jnp.mean(xf, axis=-1, keepdims=True)
    var = jnp.mean(jnp.square(xf - mu), axis=-1, keepdims=True)
    return ((xf - mu) * lax.rsqrt(var + EPS)).astype(x.dtype) * g + b


def t5_bucket(rel):
    nb = REL_BUCKETS // 2
    max_exact = nb // 2
    ret = (rel > 0).astype(jnp.int32) * nb
    n = jnp.abs(rel)
    nf = jnp.maximum(n, 1).astype(jnp.float32)
    large = max_exact + (jnp.log(nf / max_exact) / math.log(REL_MAX_DIST / max_exact) * (nb - max_exact)).astype(jnp.int32)
    large = jnp.minimum(large, nb - 1)
    return ret + jnp.where(n < max_exact, n, large)


def dsa_sparse_attention(q_lat, c, qi, ki, wi, w_uv, rel_bias, topk):
    B, S, H, _ = q_lat.shape
    nblk = S // Q_BLOCK

    def blocks(a):
        return jnp.moveaxis(a.reshape((B, nblk, Q_BLOCK) + a.shape[2:]), 1, 0)

    ki32 = ki.astype(jnp.float32)
    kchunk = jnp.arange(S, dtype=jnp.int32) // CHUNK

    def one_block(args):
        ql, qib, wib, blk = args
        qpos = blk * Q_BLOCK + jnp.arange(Q_BLOCK, dtype=jnp.int32)
        qchunk = qpos // CHUNK
        admissible = kchunk[None, :] <= qchunk[:, None]
        dots = jnp.einsum('bthd,bsd->bths', qib.astype(jnp.float32), ki32) * (IDX_DIM ** -0.5)
        score = jnp.einsum('bth,bths->bts', wib.astype(jnp.float32), jax.nn.relu(dots))
        score = jnp.where(admissible[None], score, -jnp.inf)
        _, idx = lax.top_k(score, topk)
        sel = jax.vmap(lambda cb, ib: cb[ib])(c, idx)
        valid = (idx // CHUNK) <= qchunk[None, :, None]
        bias = rel_bias[t5_bucket(idx - qpos[None, :, None])]
        logits = jnp.einsum('bthc,btkc->bthk', ql, sel).astype(jnp.float32) * (A_HEAD_DIM ** -0.5)
        logits = logits + jnp.moveaxis(bias, -1, 2).astype(jnp.float32)
        logits = jnp.where(valid[:, :, None, :], logits, -jnp.inf)
        p = jax.nn.softmax(logits, axis=-1).astype(sel.dtype)
        o_lat = jnp.einsum('bthk,btkc->bthc', p, sel)
        return jnp.einsum('bthc,hcd->bthd', o_lat, w_uv).reshape(B, Q_BLOCK, H * A_HEAD_DIM)

    out = lax.map(one_block, (blocks(q_lat), blocks(qi), blocks(wi), jnp.arange(nblk, dtype=jnp.int32)))
    return jnp.moveaxis(out, 0, 1).reshape(B, S, H * A_HEAD_DIM)


def hgrn2_recurrence(q, f_logit, v, lb):
    B, S, H, DK = q.shape
    DV = v.shape[-1]
    f32 = jnp.float32
    f = lb + (1.0 - lb) * jax.nn.sigmoid(f_logit.astype(f32))
    g = jnp.log(f)
    k = 1.0 - f
    qf = jax.nn.silu(q.astype(f32))

    def to_chunks(a):
        return a.reshape(B, S // CHUNK, CHUNK, H, a.shape[-1]).transpose(1, 0, 3, 2, 4)

    tri = jnp.tril(jnp.ones((CHUNK, CHUNK), dtype=bool))

    def step(state, xs):
        qc, kc, gc, vc = xs
        b = jnp.cumsum(gc, axis=2)
        dec = jnp.where(tri[:, :, None], b[:, :, :, None, :] - b[:, :, None, :, :], -jnp.inf)
        scores = jnp.einsum('bhtd,bhsd,bhtsd->bhts', qc, kc, jnp.exp(dec))
        o = jnp.einsum('bhts,bhsv->bhtv', scores, vc) + jnp.einsum('bhtd,bhdv->bhtv', qc * jnp.exp(b), state)
        b_end = b[:, :, -1:, :]
        state = jnp.exp(b_end[:, :, 0, :])[..., None] * state + jnp.einsum('bhsd,bhsv->bhdv', kc * jnp.exp(b_end - b), vc)
        return state, o

    state0 = jnp.zeros((B, H, DK, DV), f32)
    _, o = lax.scan(step, state0, (to_chunks(qf), to_chunks(k), to_chunks(g), to_chunks(v.astype(f32))))
    return o.transpose(1, 0, 3, 2, 4).reshape(B, S, H, DV).astype(v.dtype)


def even_mixer(h, w_in, lat_g, w_uk, w_uv, o_g, w_out, rel_bias, lb, topk):
    B, S, _ = h.shape
    sizes = [A_HEADS * A_HEAD_DIM, A_LATENT, IDX_HEADS * IDX_DIM, IDX_DIM, IDX_HEADS,
             B_HEADS * B_DK, B_HEADS * B_DV, B_HEADS * B_DK, B_HEADS * B_DV]
    offsets = np.cumsum(sizes)[:-1].tolist()
    q_a, c, qi, ki, wi, f_b, i_b, q_b, g_b = jnp.split(h @ w_in, offsets, axis=-1)
    c = rmsnorm(c, lat_g)
    q_lat = jnp.einsum('bshd,hcd->bshc', q_a.reshape(B, S, A_HEADS, A_HEAD_DIM), w_uk)
    a_out = dsa_sparse_attention(q_lat, c, qi.reshape(B, S, IDX_HEADS, IDX_DIM), ki,
                                 wi * (IDX_HEADS ** -0.5), w_uv, rel_bias, topk)
    o = hgrn2_recurrence(q_b.reshape(B, S, B_HEADS, B_DK), f_b.reshape(B, S, B_HEADS, B_DK),
                         i_b.reshape(B, S, B_HEADS, B_DV), lb.reshape(B_HEADS, B_DK))
    b_out = (rmsnorm(o, o_g) * jax.nn.silu(g_b.reshape(B, S, B_HEADS, B_DV))).reshape(B, S, B_HEADS * B_DV)
    return jnp.concatenate([a_out, b_out], axis=-1) @ w_out


def gmlp_spatial_mixer(h, w_in, ln_g, ln_b, w_sp, b_sp, w_out):
    B, S, _ = h.shape
    u, v = jnp.split(jax.nn.gelu(h @ w_in, approximate=False), 2, axis=-1)
    v = layernorm(v, ln_g, ln_b)
    v = v.reshape(B, S // GM_CHUNK, GM_CHUNK, GM_GROUPS, GM_WIDTH // GM_GROUPS)
    w = jnp.where(jnp.tril(jnp.ones((GM_CHUNK, GM_CHUNK), dtype=bool)), w_sp, 0.0)
    mixed = jnp.einsum('gts,bnsgc->bntgc', w, v) + jnp.transpose(b_sp)[None, None, :, :, None]
    return (u * mixed.reshape(B, S, GM_WIDTH)) @ w_out


def memory_cross_attention(h, m, wq, wkv, wo):
    B, S, _ = h.shape
    M = m.shape[1]
    q = (h @ wq).reshape(B, S, X_HEADS, X_HEAD_DIM)
    k, v = jnp.split(m @ wkv, 2, axis=-1)
    k = k.reshape(B, M, X_HEADS, X_HEAD_DIM)
    v = v.reshape(B, M, X_HEADS, X_HEAD_DIM)
    logits = jnp.einsum('bshd,bmhd->bhsm', q, k).astype(jnp.float32) * (X_HEAD_DIM ** -0.5)
    p = jax.nn.softmax(logits, axis=-1).astype(v.dtype)
    return jnp.einsum('bhsm,bmhd->bshd', p, v).reshape(B, S, X_HEADS * X_HEAD_DIM) @ wo


def swiglu(h, w_gu, w_down):
    gate, up = jnp.split(h @ w_gu, 2, axis=-1)
    return (jax.nn.silu(gate) * up) @ w_down


def setup_inputs(seed: int = 0) -> dict:
    key = jax.random.key(seed)
    ks = list(jax.random.split(key, 40))
    cnt = [0]

    def nxt():
        k = ks[cnt[0]]
        cnt[0] += 1
        return k

    def nrm(shape, scale):
        return jax.random.normal(nxt(), shape, jnp.float32) * scale

    def gain(shape):
        return 1.0 + nrm(shape, 0.05)

    n_even = (DEPTH + 1) // 2
    n_odd = DEPTH // 2
    p_even = (A_HEADS * A_HEAD_DIM + A_LATENT + IDX_HEADS * IDX_DIM + IDX_DIM + IDX_HEADS
              + 2 * B_HEADS * B_DK + 2 * B_HEADS * B_DV)
    mix_even = A_HEADS * A_HEAD_DIM + B_HEADS * B_DV
    xw = X_HEADS * X_HEAD_DIM
    return {
        'x': nrm((BATCH, SEQ, D_MODEL), 1.0),
        'mem': nrm((BATCH, MEM_TOKENS, D_MODEL), 1.0),
        'rel_bias': nrm((REL_BUCKETS, A_HEADS), 0.5),
        'hgrn_lb': nrm((DEPTH + 1, B_HEADS * B_DK), 0.5),
        'mix_norm': gain((DEPTH, D_MODEL)),
        'e_w_in': nrm((n_even, D_MODEL, p_even), D_MODEL ** -0.5),
        'e_lat_norm': gain((n_even, A_LATENT)),
        'e_w_uk': nrm((n_even, A_HEADS, A_LATENT, A_HEAD_DIM), A_HEAD_DIM ** -0.5),
        'e_w_uv': nrm((n_even, A_HEADS, A_LATENT, A_HEAD_DIM), A_LATENT ** -0.5),
        'e_o_norm': gain((n_even, B_DV)),
        'e_w_out': nrm((n_even, mix_even, D_MODEL), mix_even ** -0.5),
        'o_w_in': nrm((n_odd, D_MODEL, 2 * GM_WIDTH), D_MODEL ** -0.5),
        'o_ln_g': gain((n_odd, GM_WIDTH)),
        'o_ln_b': nrm((n_odd, GM_WIDTH), 0.02),
        'o_w_sp': nrm((n_odd, GM_GROUPS, GM_CHUNK, GM_CHUNK), 0.5 * GM_CHUNK ** -0.5),
        'o_b_sp': 1.0 + nrm((n_odd, GM_GROUPS, GM_CHUNK), 0.1),
        'o_w_out': nrm((n_odd, GM_WIDTH, D_MODEL), GM_WIDTH ** -0.5),
        'x_norm': gain((DEPTH, D_MODEL)),
        'mem_norm': gain((DEPTH, D_MODEL)),
        'x_wq': nrm((DEPTH, D_MODEL, xw), D_MODEL ** -0.5),
        'x_wkv': nrm((DEPTH, D_MODEL, 2 * xw), D_MODEL ** -0.5),
        'x_wo': nrm((DEPTH, xw, D_MODEL), xw ** -0.5),
        'f_norm': gain((DEPTH, D_MODEL)),
        'f_w_gu': nrm((DEPTH, D_MODEL, 2 * FF_DIM), D_MODEL ** -0.5),
        'f_w_down': nrm((DEPTH, FF_DIM, D_MODEL), FF_DIM ** -0.5),
        'final_norm': gain((D_MODEL,)),
    }


def reference(x, mem, rel_bias, hgrn_lb, mix_norm, e_w_in, e_lat_norm, e_w_uk, e_w_uv, e_o_norm, e_w_out,
              o_w_in, o_ln_g, o_ln_b, o_w_sp, o_b_sp, o_w_out, x_norm, mem_norm, x_wq, x_wkv, x_wo,
              f_norm, f_w_gu, f_w_down, final_norm):
    topk = min(TOPK_MAX, x.shape[1] // 4)
    lb_all = jnp.cumsum(jax.nn.softmax(hgrn_lb.astype(jnp.float32), axis=0), axis=0)
    h = x
    for l in range(DEPTH):
        j = l // 2
        hn = rmsnorm(h, mix_norm[l])
        if l % 2 == 0:
            h = h + even_mixer(hn, e_w_in[j], e_lat_norm[j], e_w_uk[j], e_w_uv[j], e_o_norm[j], e_w_out[j],
                               rel_bias, lb_all[l], topk)
        else:
            h = h + gmlp_spatial_mixer(hn, o_w_in[j], o_ln_g[j], o_ln_b[j], o_w_sp[j], o_b_sp[j], o_w_out[j])
        h = h + memory_cross_attention(rmsnorm(h, x_norm[l]), rmsnorm(mem, mem_norm[l]), x_wq[l], x_wkv[l], x_wo[l])
        h = h + swiglu(rmsnorm(h, f_norm[l]), f_w_gu[l], f_w_down[l])
    return rmsnorm(h, final_norm)
```

```python
import functools

import jax
import jax.numpy as jnp
from jax import lax
from jax.experimental import pallas as pl
from jax.experimental.pallas import tpu as pltpu

F32 = jnp.float32
BF16 = jnp.bfloat16
I32 = jnp.int32

EPS = 1e-6
CHUNK = 64
TILE = 128
A_HEADS = 8
A_HEAD_DIM = 64
A_LATENT = 128
IDX_HEADS = 8
IDX_DIM = 64
TOPK_MAX = 256
REL_BUCKETS = 32
B_HEADS = 4
B_DK = 128
GM_CHUNK = 128
GM_GROUPS = 8
X_HEADS = 4
X_HEAD_DIM = 128
NEG = -1e30
INT_MIN = -2**31

VMEM_LIMIT = 56 * 1024 * 1024

NT = (((1,), (1,)), ((), ()))
TN = (((0,), (0,)), ((), ()))


def _rms(x, g):
    ms = jnp.mean(x * x, axis=-1, keepdims=True)
    return x * lax.rsqrt(ms + EPS) * g


def _params(sem, vmem=VMEM_LIMIT):
    return pltpu.CompilerParams(dimension_semantics=sem, vmem_limit_bytes=vmem)


def _norm_matmul_kernel(x_ref, g_ref, w_ref, o_ref, xn_ref):
    @pl.when(pl.program_id(1) == 0)
    def _():
        xn_ref[...] = _rms(x_ref[...], g_ref[...]).astype(BF16)

    o_ref[...] = jnp.dot(xn_ref[...], w_ref[...], preferred_element_type=F32).astype(o_ref.dtype)


def norm_matmul(x, g, w, *, tm, tn, out_dtype):
    n, k = x.shape
    nout = w.shape[1]
    return pl.pallas_call(
        _norm_matmul_kernel,
        out_shape=jax.ShapeDtypeStruct((n, nout), out_dtype),
        grid=(n // tm, nout // tn),
        in_specs=[
            pl.BlockSpec((tm, k), lambda i, j: (i, 0)),
            pl.BlockSpec((1, k), lambda i, j: (0, 0)),
            pl.BlockSpec((k, tn), lambda i, j: (0, j)),
        ],
        out_specs=pl.BlockSpec((tm, tn), lambda i, j: (i, j)),
        scratch_shapes=[pltpu.VMEM((tm, k), BF16)],
        compiler_params=_params(("parallel", "arbitrary")),
        name="norm_matmul",
    )(x, g.reshape(1, k), w)


def _bias_tiles_kernel(rb_ref, o_ref):
    s = lax.broadcasted_iota(I32, (TILE, TILE), 0)
    t = lax.broadcasted_iota(I32, (TILE, TILE), 1)
    nb = REL_BUCKETS // 2
    max_exact = nb // 2
    for kind in range(3):
        rel = s - t - TILE * kind
        n = jnp.abs(rel)
        n2 = n * n
        large = jnp.full((TILE, TILE), max_exact, I32)
        for j in range(1, nb - max_exact):
            large = large + jnp.where(n2 >= (max_exact * max_exact) * (2 ** j), 1, 0)
        bucket = jnp.where(rel > 0, nb, 0) + jnp.where(n < max_exact, n, large)
        for h in range(A_HEADS):
            acc = jnp.zeros((TILE, TILE), F32)
            for b in range(REL_BUCKETS):
                acc = jnp.where(bucket == b, rb_ref[b, h], acc)
            o_ref[kind, :, h * TILE:(h + 1) * TILE] = acc


def bias_tiles(rel_bias):
    return pl.pallas_call(
        _bias_tiles_kernel,
        out_shape=jax.ShapeDtypeStruct((3, TILE, A_HEADS * TILE), F32),
        in_specs=[pl.BlockSpec(memory_space=pltpu.SMEM)],
        out_specs=pl.BlockSpec(memory_space=pltpu.VMEM),
        name="bias_tiles",
    )(rel_bias)


def _dsa_kernel(qa_ref, qi_ref, wi_ref, c_ref, ki_ref, latg_ref, wuk_ref, wuvt_ref, bias_ref,
                o_ref, cn_ref, ct_ref, kib_ref, keys_ref, m_ref, l_ref, acc_ref, *, topk, ntiles):
    i = pl.program_id(1)
    hq = A_HEADS * TILE

    @pl.when(i == 0)
    def _prep():
        cn = _rms(c_ref[0], latg_ref[...])
        cn_ref[...] = cn.astype(BF16)
        for j in range(ntiles):
            ct_ref[j] = cn[j * TILE:(j + 1) * TILE, :].T.astype(BF16)
        kib_ref[...] = ki_ref[0].astype(BF16)

    qa = qa_ref[0].astype(BF16)
    ql = jnp.concatenate(
        [lax.dot_general(qa[:, h * A_HEAD_DIM:(h + 1) * A_HEAD_DIM], wuk_ref[h], NT,
                         preferred_element_type=F32) for h in range(A_HEADS)], axis=0)
    qlb = (ql * (A_HEAD_DIM ** -0.5)).astype(BF16)
    qi = qi_ref[0].astype(BF16)
    qib = jnp.concatenate([qi[:, h * IDX_DIM:(h + 1) * IDX_DIM] for h in range(IDX_HEADS)], axis=0)
    w = wi_ref[0] * (IDX_HEADS ** -0.5) * (IDX_DIM ** -0.5)
    wt = w.T

    row = lax.broadcasted_iota(I32, (TILE, TILE), 0)
    col = lax.broadcasted_iota(I32, (TILE, TILE), 1)
    inadm_diag = (row >= CHUNK) & (col < CHUNK)

    def p1(j, carry):
        kt = kib_ref[pl.ds(pl.multiple_of(j * TILE, TILE), TILE), :][:, :IDX_DIM]
        d = lax.dot_general(kt, qib, NT, preferred_element_type=F32)
        sc = jnp.zeros((TILE, TILE), F32)
        for h in range(IDX_HEADS):
            sc = sc + jnp.maximum(d[:, h * TILE:(h + 1) * TILE], 0.0) * wt[h:h + 1, :]
        sc = jnp.where(inadm_diag & (j == i), -jnp.inf, sc)
        bits = pltpu.bitcast(sc, I32)
        key = jnp.where(bits < 0, bits ^ 0x7FFFFFFF, bits)
        key = jnp.where(bits == INT_MIN, 0, key)
        keys_ref[j] = key
        return carry

    lax.fori_loop(0, i + 1, p1, 0)

    def count(pred):
        def body(j, acc):
            hit = jnp.where(pred(keys_ref[j]), 1, 0)
            return acc + hit.reshape(TILE // 8, 8, TILE).sum(axis=0)
        c8 = lax.fori_loop(0, i + 1, body, jnp.zeros((8, TILE), I32))
        return c8.sum(axis=0, keepdims=True)

    def bisect(step, lo):
        trial = lo + lax.shift_left(jnp.int32(1), 31 - step)
        cnt = count(lambda k: k >= trial)
        return jnp.where(cnt >= topk, trial, lo)

    thr = lax.fori_loop(0, 32, bisect, jnp.full((1, TILE), INT_MIN, I32))
    need = (topk - count(lambda k: k > thr)).astype(F32)

    m_ref[...] = jnp.full((1, hq), NEG, F32)
    l_ref[...] = jnp.zeros((1, hq), F32)
    acc_ref[...] = jnp.zeros((A_LATENT, hq), F32)
    tri = jnp.where(col < row, 1.0, 0.0).astype(BF16)

    def p3(j, run_eq):
        key = keys_ref[j]
        eq = key == thr
        eqf = jnp.where(eq, 1.0, 0.0)
        rank = jnp.dot(tri, eqf.astype(BF16), preferred_element_type=F32) + run_eq
        sel = (key > thr) | (eq & (rank < need))
        sel = sel & jnp.logical_not(inadm_diag & (j == i))
        negm = jnp.where(sel, 0.0, NEG)
        ct = cn_ref[pl.ds(pl.multiple_of(j * TILE, TILE), TILE), :]
        s = lax.dot_general(ct, qlb, NT, preferred_element_type=F32)
        kind = jnp.minimum(i - j, 2)
        bias = bias_ref[kind]
        m_old = m_ref[...]
        sh = [s[:, h * TILE:(h + 1) * TILE] + bias[:, h * TILE:(h + 1) * TILE] + negm
              for h in range(A_HEADS)]
        m_new = jnp.concatenate([x.max(axis=0, keepdims=True) for x in sh], axis=1)
        m_new = jnp.maximum(m_old, m_new)
        alpha = jnp.exp(m_old - m_new)
        p = jnp.concatenate([jnp.exp(sh[h] - m_new[:, h * TILE:(h + 1) * TILE])
                             for h in range(A_HEADS)], axis=1)
        l_ref[...] = alpha * l_ref[...] + p.sum(axis=0, keepdims=True)
        acc_ref[...] = alpha * acc_ref[...] + jnp.dot(ct_ref[j], p.astype(BF16),
                                                      preferred_element_type=F32)
        m_ref[...] = m_new
        return run_eq + eqf.sum(axis=0, keepdims=True)

    lax.fori_loop(0, i + 1, p3, jnp.zeros((1, TILE), F32))

    ot = (acc_ref[...] * (1.0 / l_ref[...])).astype(BF16)
    out_t = jnp.concatenate(
        [jnp.dot(wuvt_ref[h], ot[:, h * TILE:(h + 1) * TILE], preferred_element_type=F32)
         for h in range(A_HEADS)], axis=0)
    o_ref[0] = out_t.T


def dsa_attention(proj, lat_g, w_uk, w_uv, bias, *, topk):
    b, s, _ = proj.shape
    ntiles = s // TILE
    kern = functools.partial(_dsa_kernel, topk=topk, ntiles=ntiles)
    return pl.pallas_call(
        kern,
        out_shape=jax.ShapeDtypeStruct((b, s, A_HEADS * A_HEAD_DIM), F32),
        grid=(b, ntiles),
        in_specs=[
            pl.BlockSpec((1, TILE, 512), lambda bi, i: (bi, i, 0)),
            pl.BlockSpec((1, TILE, 512), lambda bi, i: (bi, i, 1)),
            pl.BlockSpec((1, TILE, TILE), lambda bi, i: (bi, i, 26)),
            pl.BlockSpec((1, s, TILE), lambda bi, i: (bi, 0, 24)),
            pl.BlockSpec((1, s, TILE), lambda bi, i: (bi, 0, 25)),
            pl.BlockSpec((1, A_LATENT), lambda bi, i: (0, 0)),
            pl.BlockSpec((A_HEADS, A_LATENT, A_HEAD_DIM), lambda bi, i: (0, 0, 0)),
            pl.BlockSpec((A_HEADS, A_HEAD_DIM, A_LATENT), lambda bi, i: (0, 0, 0)),
            pl.BlockSpec((3, TILE, A_HEADS * TILE), lambda bi, i: (0, 0, 0)),
        ],
        out_specs=pl.BlockSpec((1, TILE, A_HEADS * A_HEAD_DIM), lambda bi, i: (bi, i, 0)),
        scratch_shapes=[
            pltpu.VMEM((s, A_LATENT), BF16),
            pltpu.VMEM((ntiles, A_LATENT, TILE), BF16),
            pltpu.VMEM((s, TILE), BF16),
            pltpu.VMEM((ntiles, TILE, TILE), I32),
            pltpu.VMEM((1, A_HEADS * TILE), F32),
            pltpu.VMEM((1, A_HEADS * TILE), F32),
            pltpu.VMEM((A_LATENT, A_HEADS * TILE), F32),
        ],
        compiler_params=_params(("parallel", "arbitrary")),
        name="dsa_attention",
    )(proj, proj, proj, proj, proj, lat_g.reshape(1, A_LATENT), w_uk.astype(BF16),
      jnp.swapaxes(w_uv, 1, 2).astype(BF16), bias)


def _hgrn_kernel(f_ref, v_ref, q_ref, g_ref, lb_ref, og_ref, o_ref, st_ref, *, nchunk):
    @pl.when(pl.program_id(1) == 0)
    def _():
        st_ref[...] = jnp.zeros(st_ref.shape, F32)

    lb = lb_ref[...]
    og = og_ref[...]
    row = lax.broadcasted_iota(I32, (CHUNK, CHUNK), 0)
    col = lax.broadcasted_iota(I32, (CHUNK, CHUNK), 1)
    causal = col <= row
    tri = jnp.where(causal, 1.0, 0.0).astype(BF16)
    mid = CHUNK // 2 - 1

    for c in range(nchunk):
        rows = slice(c * CHUNK, (c + 1) * CHUNK)
        f = lb + (1.0 - lb) * jax.nn.sigmoid(f_ref[0, rows, :])
        g = jnp.log(f)
        k = 1.0 - f
        q = q_ref[0, rows, :]
        qf = q * jax.nn.sigmoid(q)
        vb = v_ref[0, rows, :].astype(BF16)
        g_hi = g.astype(BF16)
        g_lo = (g - g_hi.astype(F32)).astype(BF16)
        bc = (jnp.dot(tri, g_hi, preferred_element_type=F32)
              + jnp.dot(tri, g_lo, preferred_element_type=F32))
        b_mid = bc[mid:mid + 1, :]
        b_end = bc[CHUNK - 1:CHUNK, :]
        qe = (qf * jnp.exp(bc - b_mid)).astype(BF16)
        ke = (k * jnp.exp(b_mid - bc)).astype(BF16)
        qb = (qf * jnp.exp(bc)).astype(BF16)
        kd = (k * jnp.exp(b_end - bc)).astype(BF16)
        dec = jnp.exp(b_end)
        gate = g_ref[0, rows, :]
        gate = gate * jax.nn.sigmoid(gate)
        for h in range(B_HEADS):
            sl = slice(h * B_DK, (h + 1) * B_DK)
            sc = lax.dot_general(qe[:, sl], ke[:, sl], NT, preferred_element_type=F32)
            sc = jnp.where(causal, sc, 0.0).astype(BF16)
            st = st_ref[h]
            o = (jnp.dot(sc, vb[:, sl], preferred_element_type=F32)
                 + lax.dot_general(qb[:, sl], st.astype(BF16), NT, preferred_element_type=F32))
            st_ref[h] = st * dec[:, sl] + lax.dot_general(vb[:, sl], kd[:, sl], TN,
                                                          preferred_element_type=F32)
            o_ref[0, rows, sl] = _rms(o, og) * gate[:, sl]


def hgrn2(proj, lb, o_g, *, rows):
    b, s, _ = proj.shape
    width = B_HEADS * B_DK
    kern = functools.partial(_hgrn_kernel, nchunk=rows // CHUNK)
    spec = lambda blk: pl.BlockSpec((1, rows, width), lambda bi, r: (bi, r, blk))
    return pl.pallas_call(
        kern,
        out_shape=jax.ShapeDtypeStruct((b, s, width), F32),
        grid=(b, s // rows),
        in_specs=[spec(2), spec(3), spec(4), spec(5),
                  pl.BlockSpec((1, width), lambda bi, r: (0, 0)),
                  pl.BlockSpec((1, B_DK), lambda bi, r: (0, 0))],
        out_specs=pl.BlockSpec((1, rows, width), lambda bi, r: (bi, r, 0)),
        scratch_shapes=[pltpu.VMEM((B_HEADS, B_DK, B_DK), F32)],
        compiler_params=_params(("parallel", "arbitrary")),
        name="hgrn2",
    )(proj, proj, proj, proj, lb.reshape(1, width), o_g.reshape(1, B_DK))


def _proj2_res_kernel(res_ref, a_ref, b_ref, wa_ref, wb_ref, o_ref):
    o_ref[...] = (res_ref[...]
                  + jnp.dot(a_ref[...].astype(BF16), wa_ref[...], preferred_element_type=F32)
                  + jnp.dot(b_ref[...].astype(BF16), wb_ref[...], preferred_element_type=F32))


def proj2_residual(res, a, b, wa, wb, *, tm):
    n, d = res.shape
    ka, kb = a.shape[1], b.shape[1]
    return pl.pallas_call(
        _proj2_res_kernel,
        out_shape=jax.ShapeDtypeStruct((n, d), F32),
        grid=(n // tm,),
        in_specs=[pl.BlockSpec((tm, d), lambda i: (i, 0)),
                  pl.BlockSpec((tm, ka), lambda i: (i, 0)),
                  pl.BlockSpec((tm, kb), lambda i: (i, 0)),
                  pl.BlockSpec((ka, d), lambda i: (0, 0)),
                  pl.BlockSpec((kb, d), lambda i: (0, 0))],
        out_specs=pl.BlockSpec((tm, d), lambda i: (i, 0)),
        compiler_params=_params(("parallel",)),
        name="proj2_residual",
    )(res, a, b, wa, wb)


def _gmlp_kernel(h_ref, g_ref, win_ref, lng_ref, lnb_ref, wsp_ref, bsp_ref, wout_ref, o_ref,
                 gated_ref, *, nchunk):
    x = h_ref[0]
    width = x.shape[1]
    xn = _rms(x, g_ref[...]).astype(BF16)
    uv = jnp.dot(xn, win_ref[...], preferred_element_type=F32)
    uv = 0.5 * uv * (1.0 + lax.erf(uv * (0.5 ** 0.5)))
    u = uv[:, :width]
    v = uv[:, width:]
    mu = jnp.mean(v, axis=-1, keepdims=True)
    vc = v - mu
    var = jnp.mean(vc * vc, axis=-1, keepdims=True)
    vn = (vc * lax.rsqrt(var + EPS) * lng_ref[...] + lnb_ref[...]).astype(BF16)
    row = lax.broadcasted_iota(I32, (GM_CHUNK, GM_CHUNK), 0)
    col = lax.broadcasted_iota(I32, (GM_CHUNK, GM_CHUNK), 1)
    gw = width // GM_GROUPS
    for gi in range(GM_GROUPS):
        wg = jnp.where(col <= row, wsp_ref[gi], 0.0).astype(BF16)
        cs = slice(gi * gw, (gi + 1) * gw)
        for c in range(nchunk):
            rs = slice(c * GM_CHUNK, (c + 1) * GM_CHUNK)
            mixed = jnp.dot(wg, vn[rs, cs], preferred_element_type=F32) + bsp_ref[:, cs]
            gated_ref[rs, cs] = (u[rs, cs] * mixed).astype(BF16)
    o_ref[0] = x + jnp.dot(gated_ref[...], wout_ref[...], preferred_element_type=F32)


def gmlp(h, g, w_in, ln_g, ln_b, w_sp, b_sp, w_out, *, rows):
    b, s, d = h.shape
    width = w_out.shape[0]
    gw = width // GM_GROUPS
    bsp_full = jnp.repeat(jnp.transpose(b_sp), gw, axis=1)
    kern = functools.partial(_gmlp_kernel, nchunk=rows // GM_CHUNK)
    const = lambda shape: pl.BlockSpec(shape, lambda bi, r: (0,) * len(shape))
    return pl.pallas_call(
        kern,
        out_shape=jax.ShapeDtypeStruct((b, s, d), F32),
        grid=(b, s // rows),
        in_specs=[pl.BlockSpec((1, rows, d), lambda bi, r: (bi, r, 0)),
                  const((1, d)), const((d, 2 * width)), const((1, width)), const((1, width)),
                  const((GM_GROUPS, GM_CHUNK, GM_CHUNK)), const((GM_CHUNK, width)),
                  const((width, d))],
        out_specs=pl.BlockSpec((1, rows, d), lambda bi, r: (bi, r, 0)),
        scratch_shapes=[pltpu.VMEM((rows, width), BF16)],
        compiler_params=_params(("parallel", "parallel")),
        name="gmlp",
    )(h, g.reshape(1, d), w_in.astype(BF16), ln_g.reshape(1, width), ln_b.reshape(1, width),
      w_sp, bsp_full, w_out.astype(BF16))


def _xattn_kernel(h_ref, g_ref, wq_ref, k_ref, v_ref, wo_ref, o_ref):
    x = h_ref[0]
    xn = _rms(x, g_ref[...]).astype(BF16)
    q = jnp.dot(xn, wq_ref[...], preferred_element_type=F32).astype(BF16)
    outs = []
    for h in range(X_HEADS):
        sl = slice(h * X_HEAD_DIM, (h + 1) * X_HEAD_DIM)
        s = lax.dot_general(q[:, sl], k_ref[0, :, sl], NT, preferred_element_type=F32)
        s = s * (X_HEAD_DIM ** -0.5)
        p = jnp.exp(s - s.max(axis=-1, keepdims=True))
        l = p.sum(axis=-1, keepdims=True)
        o = jnp.dot(p.astype(BF16), v_ref[0, :, sl], preferred_element_type=F32)
        outs.append((o * (1.0 / l)).astype(BF16))
    att = jnp.concatenate(outs, axis=1)
    o_ref[0] = x + jnp.dot(att, wo_ref[...], preferred_element_type=F32)


def xattn(h, g, wq, kv, wo, *, rows):
    b, s, d = h.shape
    m = kv.shape[1]
    xw = wq.shape[1]
    const = lambda shape: pl.BlockSpec(shape, lambda bi, r: (0,) * len(shape))
    return pl.pallas_call(
        _xattn_kernel,
        out_shape=jax.ShapeDtypeStruct((b, s, d), F32),
        grid=(b, s // rows),
        in_specs=[pl.BlockSpec((1, rows, d), lambda bi, r: (bi, r, 0)),
                  const((1, d)), const((d, xw)),
                  pl.BlockSpec((1, m, xw), lambda bi, r: (bi, 0, 0)),
                  pl.BlockSpec((1, m, xw), lambda bi, r: (bi, 0, 1)),
                  const((xw, d))],
        out_specs=pl.BlockSpec((1, rows, d), lambda bi, r: (bi, r, 0)),
        compiler_params=_params(("parallel", "parallel")),
        name="xattn",
    )(h, g.reshape(1, d), wq.astype(BF16), kv, kv, wo.astype(BF16))


def _swiglu_kernel(x_ref, g_ref, wgu_ref, wd_ref, fg_ref, o_ref, a_ref, *, ff, fchunk, final):
    x = x_ref[...]
    xn = _rms(x, g_ref[...]).astype(BF16)
    for c in range(ff // fchunk):
        gate = jnp.dot(xn, wgu_ref[:, c * fchunk:(c + 1) * fchunk], preferred_element_type=F32)
        up = jnp.dot(xn, wgu_ref[:, ff + c * fchunk:ff + (c + 1) * fchunk],
                     preferred_element_type=F32)
        a_ref[:, c * fchunk:(c + 1) * fchunk] = (gate * jax.nn.sigmoid(gate) * up).astype(BF16)
    y = x + jnp.dot(a_ref[...], wd_ref[...], preferred_element_type=F32)
    if final:
        y = _rms(y, fg_ref[...])
    o_ref[...] = y


def swiglu(x, g, w_gu, w_down, final_g, *, tm, final):
    n, d = x.shape
    ff = w_down.shape[0]
    kern = functools.partial(_swiglu_kernel, ff=ff, fchunk=256, final=final)
    once = dict(pipeline_mode=pl.Buffered(1))
    return pl.pallas_call(
        kern,
        out_shape=jax.ShapeDtypeStruct((n, d), F32),
        grid=(n // tm,),
        in_specs=[pl.BlockSpec((tm, d), lambda i: (i, 0)),
                  pl.BlockSpec((1, d), lambda i: (0, 0)),
                  pl.BlockSpec((d, 2 * ff), lambda i: (0, 0), **once),
                  pl.BlockSpec((ff, d), lambda i: (0, 0), **once),
                  pl.BlockSpec((1, d), lambda i: (0, 0))],
        out_specs=pl.BlockSpec((tm, d), lambda i: (i, 0)),
        scratch_shapes=[pltpu.VMEM((tm, ff), BF16)],
        compiler_params=_params(("parallel",)),
        name="swiglu",
    )(x, g.reshape(1, d), w_gu.astype(BF16), w_down.astype(BF16), final_g.reshape(1, d))


def _pack_in_proj(w):
    sizes = [A_HEADS * A_HEAD_DIM, A_LATENT, IDX_HEADS * IDX_DIM, IDX_DIM, IDX_HEADS,
             B_HEADS * B_DK, B_HEADS * B_DK, B_HEADS * B_DK, B_HEADS * B_DK]
    offs = [0]
    for sz in sizes:
        offs.append(offs[-1] + sz)
    q_a, c, qi, ki, wi, f_b, i_b, q_b, g_b = [w[:, offs[n]:offs[n + 1]] for n in range(9)]
    z = lambda n: jnp.zeros((w.shape[0], n), w.dtype)
    return jnp.concatenate([q_a, qi, f_b, i_b, q_b, g_b, c, ki, z(TILE - IDX_DIM),
                            wi, z(TILE - IDX_HEADS), z(TILE)], axis=1)


def kernel(x, mem, rel_bias, hgrn_lb, mix_norm, e_w_in, e_lat_norm, e_w_uk, e_w_uv, e_o_norm, e_w_out, o_w_in, o_ln_g, o_ln_b, o_w_sp, o_b_sp, o_w_out, x_norm, mem_norm, x_wq, x_wkv, x_wo, f_norm, f_w_gu, f_w_down, final_norm):
    b, s, d = x.shape
    m = mem.shape[1]
    n = b * s
    depth = mix_norm.shape[0]
    topk = min(TOPK_MAX, s // 4)
    lb_all = jnp.cumsum(jax.nn.softmax(hgrn_lb.astype(F32), axis=0), axis=0)
    bias = bias_tiles(rel_bias)
    mem2 = mem.reshape(b * m, d)

    h = x.reshape(n, d)
    for l in range(depth):
        j = l // 2
        if l % 2 == 0:
            proj = norm_matmul(h, mix_norm[l], _pack_in_proj(e_w_in[j]).astype(BF16),
                               tm=512, tn=512, out_dtype=F32).reshape(b, s, -1)
            a_out = dsa_attention(proj, e_lat_norm[j], e_w_uk[j], e_w_uv[j], bias, topk=topk)
            b_out = hgrn2(proj, lb_all[l], e_o_norm[j], rows=256)
            wa = e_w_out[j][:A_HEADS * A_HEAD_DIM].astype(BF16)
            wb = e_w_out[j][A_HEADS * A_HEAD_DIM:].astype(BF16)
            h = proj2_residual(h, a_out.reshape(n, -1), b_out.reshape(n, -1), wa, wb, tm=512)
        else:
            h = gmlp(h.reshape(b, s, d), mix_norm[l], o_w_in[j], o_ln_g[j], o_ln_b[j], o_w_sp[j],
                     o_b_sp[j], o_w_out[j], rows=512).reshape(n, d)
        kv = norm_matmul(mem2, mem_norm[l], x_wkv[l].astype(BF16), tm=512, tn=1024,
                         out_dtype=BF16).reshape(b, m, -1)
        h = xattn(h.reshape(b, s, d), x_norm[l], x_wq[l], kv, x_wo[l], rows=512).reshape(n, d)
        h = swiglu(h, f_norm[l], f_w_gu[l], f_w_down[l], final_norm, tm=512,
                   final=(l == depth - 1))
    return h.reshape(b, s, d)
```

```python
import functools

import jax
import jax.numpy as jnp
from jax import lax
from jax.experimental import pallas as pl
from jax.experimental.pallas import tpu as pltpu

F32 = jnp.float32
BF16 = jnp.bfloat16
I32 = jnp.int32

EPS = 1e-6
CHUNK = 64
TILE = 128
KT = 256
A_HEADS = 8
A_HEAD_DIM = 64
A_LATENT = 128
IDX_HEADS = 8
IDX_DIM = 64
TOPK_MAX = 256
REL_BUCKETS = 32
B_HEADS = 4
B_DK = 128
GM_CHUNK = 128
GM_GROUPS = 8
X_HEADS = 4
X_HEAD_DIM = 128
NEG = -1e30
INT_MIN = -2**31

VMEM_LIMIT = 56 * 1024 * 1024

NT = (((1,), (1,)), ((), ()))
TN = (((0,), (0,)), ((), ()))


def _rms(x, g):
    ms = jnp.mean(x * x, axis=-1, keepdims=True)
    return x * lax.rsqrt(ms + EPS) * g


def _params(sem, vmem=VMEM_LIMIT):
    return pltpu.CompilerParams(dimension_semantics=sem, vmem_limit_bytes=vmem)


def _norm_matmul_kernel(x_ref, g_ref, w_ref, o_ref, *, tn):
    xn = _rms(x_ref[...], g_ref[...]).astype(BF16)
    for c in range(o_ref.shape[1] // tn):
        cols = slice(c * tn, (c + 1) * tn)
        o_ref[:, cols] = jnp.dot(xn, w_ref[:, cols], preferred_element_type=F32).astype(o_ref.dtype)


def norm_matmul(x, g, w, *, tm, tn, out_dtype):
    n, k = x.shape
    nout = w.shape[1]
    return pl.pallas_call(
        functools.partial(_norm_matmul_kernel, tn=tn),
        out_shape=jax.ShapeDtypeStruct((n, nout), out_dtype),
        grid=(n // tm,),
        in_specs=[
            pl.BlockSpec((tm, k), lambda i: (i, 0)),
            pl.BlockSpec((1, k), lambda i: (0, 0)),
            pl.BlockSpec((k, nout), lambda i: (0, 0), pipeline_mode=pl.Buffered(1)),
        ],
        out_specs=pl.BlockSpec((tm, nout), lambda i: (i, 0)),
        compiler_params=_params(("parallel",)),
        name="norm_matmul",
    )(x, g.reshape(1, k), w)


def _bias_tiles_kernel(rb_ref, o_ref):
    s = lax.broadcasted_iota(I32, (TILE, TILE), 0)
    t = lax.broadcasted_iota(I32, (TILE, TILE), 1)
    nb = REL_BUCKETS // 2
    max_exact = nb // 2
    for kind in range(3):
        rel = s - t - TILE * kind
        n = jnp.abs(rel)
        n2 = n * n
        large = jnp.full((TILE, TILE), max_exact, I32)
        for j in range(1, nb - max_exact):
            large = large + jnp.where(n2 >= (max_exact * max_exact) * (2 ** j), 1, 0)
        bucket = jnp.where(rel > 0, nb, 0) + jnp.where(n < max_exact, n, large)
        for h in range(A_HEADS):
            acc = jnp.zeros((TILE, TILE), F32)
            for b in range(REL_BUCKETS):
                acc = jnp.where(bucket == b, rb_ref[b, h], acc)
            o_ref[kind, :, h * TILE:(h + 1) * TILE] = acc


def bias_tiles(rel_bias):
    return pl.pallas_call(
        _bias_tiles_kernel,
        out_shape=jax.ShapeDtypeStruct((3, TILE, A_HEADS * TILE), F32),
        in_specs=[pl.BlockSpec(memory_space=pltpu.SMEM)],
        out_specs=pl.BlockSpec(memory_space=pltpu.VMEM),
        name="bias_tiles",
    )(rel_bias)


def _dsa_kernel(qa_ref, qi_ref, wi_ref, c_ref, ki_ref, latg_ref, wuk_ref, wuvt_ref, bias_ref,
                o_ref, cn_ref, ct_ref, kib_ref, keys_ref, s_ref, acc_ref, *, topk, ntiles):
    i = pl.program_id(1)
    hq = A_HEADS * TILE
    nt = lax.shift_right_logical(i + 2, 1)
    last = nt - 1

    @pl.when(i == 0)
    def _prep():
        cn = _rms(c_ref[0], latg_ref[...])
        cn_ref[...] = cn.astype(BF16)
        for j in range(ntiles):
            ct_ref[j] = cn[j * KT:(j + 1) * KT, :].T.astype(BF16)
        kib_ref[...] = ki_ref[0].astype(BF16)

    qa = qa_ref[0].astype(BF16)
    ql = jnp.concatenate(
        [lax.dot_general(qa[:, h * A_HEAD_DIM:(h + 1) * A_HEAD_DIM], wuk_ref[h], NT,
                         preferred_element_type=F32) for h in range(A_HEADS)], axis=0)
    qlb = (ql * (A_HEAD_DIM ** -0.5)).astype(BF16)
    qi = qi_ref[0].astype(BF16)
    qib = jnp.concatenate([qi[:, h * IDX_DIM:(h + 1) * IDX_DIM] for h in range(IDX_HEADS)], axis=0)
    w = wi_ref[0] * (IDX_HEADS ** -0.5) * (IDX_DIM ** -0.5)
    wt = w.T

    row = lax.broadcasted_iota(I32, (KT, TILE), 0)
    col = lax.broadcasted_iota(I32, (KT, TILE), 1)
    first_half = (row >= TILE) | ((row >= CHUNK) & (col < CHUNK))
    second_half = (row >= TILE + CHUNK) & (col < CHUNK)
    inadm_last = jnp.where((i & 1) == 1, jnp.where(second_half, 1, 0), jnp.where(first_half, 1, 0))

    def inadmissible(j):
        return (inadm_last * jnp.where(j == last, 1, 0)) != 0

    def rows(j):
        return pl.ds(pl.multiple_of(j * KT, KT), KT)

    def p1(j, carry):
        kt = kib_ref[rows(j), :][:, :IDX_DIM]
        d = lax.dot_general(kt, qib, NT, preferred_element_type=F32)
        sc = jnp.zeros((KT, TILE), F32)
        for h in range(IDX_HEADS):
            sc = sc + jnp.maximum(d[:, h * TILE:(h + 1) * TILE], 0.0) * wt[h:h + 1, :]
        sc = jnp.where(inadmissible(j), -jnp.inf, sc)
        bits = pltpu.bitcast(sc, I32)
        key = jnp.where(bits < 0, bits ^ 0x7FFFFFFF, bits)
        key = jnp.where(bits == INT_MIN, 0, key)
        keys_ref[j] = key
        return carry

    lax.fori_loop(0, nt, p1, 0)

    def count(pred):
        def body(j, acc):
            hit = jnp.where(pred(keys_ref[j]), 1, 0)
            return acc + hit.reshape(KT // 8, 8, TILE).sum(axis=0)
        c8 = lax.fori_loop(0, nt, body, jnp.zeros((8, TILE), I32))
        return c8.sum(axis=0, keepdims=True)

    def bisect(step, lo):
        trial = lo + lax.shift_left(jnp.int32(1), 31 - step)
        cnt = count(lambda k: k >= trial)
        return jnp.where(cnt >= topk, trial, lo)

    thr = lax.fori_loop(0, 32, bisect, jnp.full((1, TILE), INT_MIN, I32))
    need = (topk - count(lambda k: k > thr)).astype(F32)

    trow = lax.broadcasted_iota(I32, (KT, KT), 0)
    tcol = lax.broadcasted_iota(I32, (KT, KT), 1)
    tri = jnp.where(tcol < trow, 1.0, 0.0).astype(BF16)

    def p3a(j, carry):
        run_eq, m8 = carry
        key = keys_ref[j]
        eq = key == thr
        eqf = jnp.where(eq, 1.0, 0.0)
        rank = jnp.dot(tri, eqf.astype(BF16), preferred_element_type=F32) + run_eq
        sel = (key > thr) | (eq & (rank < need))
        sel = sel & jnp.logical_not(inadmissible(j))
        negm = jnp.where(sel, 0.0, NEG)
        s = lax.dot_general(cn_ref[rows(j), :], qlb, NT, preferred_element_type=F32)
        m8_new = []
        for h in range(A_HEADS):
            cols = slice(h * TILE, (h + 1) * TILE)
            mh = m8[:, cols]
            for half in range(KT // TILE):
                rs = slice(half * TILE, (half + 1) * TILE)
                kind = jnp.clip(i - 2 * j - half, 0, 2)
                blk = s[rs, cols] + bias_ref[kind, :, cols] + negm[rs, :]
                s_ref[j, rs, cols] = blk
                mh = jnp.maximum(mh, blk.reshape(TILE // 8, 8, TILE).max(axis=0))
            m8_new.append(mh)
        return run_eq + eqf.sum(axis=0, keepdims=True), jnp.concatenate(m8_new, axis=1)

    _, m8 = lax.fori_loop(0, nt, p3a, (jnp.zeros((1, TILE), F32), jnp.full((8, hq), NEG, F32)))
    m = m8.max(axis=0, keepdims=True)

    acc_ref[...] = jnp.zeros((A_LATENT, hq), F32)

    def p3b(j, l8):
        p = jnp.exp(s_ref[j] - m)
        acc_ref[...] += jnp.dot(ct_ref[j], p.astype(BF16), preferred_element_type=F32)
        return l8 + p.reshape(KT // 8, 8, hq).sum(axis=0)

    l8 = lax.fori_loop(0, nt, p3b, jnp.zeros((8, hq), F32))
    l = l8.sum(axis=0, keepdims=True)

    ot = (acc_ref[...] * (1.0 / l)).astype(BF16)
    out_t = jnp.concatenate(
        [jnp.dot(wuvt_ref[h], ot[:, h * TILE:(h + 1) * TILE], preferred_element_type=F32)
         for h in range(A_HEADS)], axis=0)
    o_ref[0] = out_t.T


def dsa_attention(proj, lat_g, w_uk, w_uv, bias, *, topk):
    b, s, _ = proj.shape
    ntiles = s // KT
    kern = functools.partial(_dsa_kernel, topk=topk, ntiles=ntiles)
    return pl.pallas_call(
        kern,
        out_shape=jax.ShapeDtypeStruct((b, s, A_HEADS * A_HEAD_DIM), F32),
        grid=(b, s // TILE),
        in_specs=[
            pl.BlockSpec((1, TILE, 512), lambda bi, i: (bi, i, 0)),
            pl.BlockSpec((1, TILE, 512), lambda bi, i: (bi, i, 1)),
            pl.BlockSpec((1, TILE, TILE), lambda bi, i: (bi, i, 26)),
            pl.BlockSpec((1, s, TILE), lambda bi, i: (bi, 0, 24)),
            pl.BlockSpec((1, s, TILE), lambda bi, i: (bi, 0, 25)),
            pl.BlockSpec((1, A_LATENT), lambda bi, i: (0, 0)),
            pl.BlockSpec((A_HEADS, A_LATENT, A_HEAD_DIM), lambda bi, i: (0, 0, 0)),
            pl.BlockSpec((A_HEADS, A_HEAD_DIM, A_LATENT), lambda bi, i: (0, 0, 0)),
            pl.BlockSpec((3, TILE, A_HEADS * TILE), lambda bi, i: (0, 0, 0)),
        ],
        out_specs=pl.BlockSpec((1, TILE, A_HEADS * A_HEAD_DIM), lambda bi, i: (bi, i, 0)),
        scratch_shapes=[
            pltpu.VMEM((s, A_LATENT), BF16),
            pltpu.VMEM((ntiles, A_LATENT, KT), BF16),
            pltpu.VMEM((s, TILE), BF16),
            pltpu.VMEM((ntiles, KT, TILE), I32),
            pltpu.VMEM((ntiles, KT, A_HEADS * TILE), F32),
            pltpu.VMEM((A_LATENT, A_HEADS * TILE), F32),
        ],
        compiler_params=_params(("parallel", "arbitrary")),
        name="dsa_attention",
    )(proj, proj, proj, proj, proj, lat_g.reshape(1, A_LATENT), w_uk.astype(BF16),
      jnp.swapaxes(w_uv, 1, 2).astype(BF16), bias)


def _hgrn_kernel(f_ref, v_ref, q_ref, g_ref, lb_ref, og_ref, o_ref, st_ref, *, nchunk):
    @pl.when(pl.program_id(1) == 0)
    def _():
        st_ref[...] = jnp.zeros(st_ref.shape, F32)

    lb = lb_ref[...]
    og = og_ref[...]
    row = lax.broadcasted_iota(I32, (CHUNK, CHUNK), 0)
    col = lax.broadcasted_iota(I32, (CHUNK, CHUNK), 1)
    causal = col <= row
    tri = jnp.where(causal, 1.0, 0.0).astype(BF16)
    mid = CHUNK // 2 - 1

    for c in range(nchunk):
        rows = slice(c * CHUNK, (c + 1) * CHUNK)
        f = lb + (1.0 - lb) * jax.nn.sigmoid(f_ref[0, rows, :])
        g = jnp.log(f)
        k = 1.0 - f
        q = q_ref[0, rows, :]
        qf = q * jax.nn.sigmoid(q)
        vb = v_ref[0, rows, :].astype(BF16)
        g_hi = g.astype(BF16)
        g_lo = (g - g_hi.astype(F32)).astype(BF16)
        bc = (jnp.dot(tri, g_hi, preferred_element_type=F32)
              + jnp.dot(tri, g_lo, preferred_element_type=F32))
        b_mid = bc[mid:mid + 1, :]
        b_end = bc[CHUNK - 1:CHUNK, :]
        qe = (qf * jnp.exp(bc - b_mid)).astype(BF16)
        ke = (k * jnp.exp(b_mid - bc)).astype(BF16)
        qb = (qf * jnp.exp(bc)).astype(BF16)
        kd = (k * jnp.exp(b_end - bc)).astype(BF16)
        dec = jnp.exp(b_end)
        gate = g_ref[0, rows, :]
        gate = gate * jax.nn.sigmoid(gate)
        for h in range(B_HEADS):
            sl = slice(h * B_DK, (h + 1) * B_DK)
            sc = lax.dot_general(qe[:, sl], ke[:, sl], NT, preferred_element_type=F32)
            sc = jnp.where(causal, sc, 0.0).astype(BF16)
            st = st_ref[h]
            o = (jnp.dot(sc, vb[:, sl], preferred_element_type=F32)
                 + lax.dot_general(qb[:, sl], st.astype(BF16), NT, preferred_element_type=F32))
            st_ref[h] = st * dec[:, sl] + lax.dot_general(vb[:, sl], kd[:, sl], TN,
                                                          preferred_element_type=F32)
            o_ref[0, rows, sl] = _rms(o, og) * gate[:, sl]


def hgrn2(proj, lb, o_g, *, rows):
    b, s, _ = proj.shape
    width = B_HEADS * B_DK
    kern = functools.partial(_hgrn_kernel, nchunk=rows // CHUNK)
    spec = lambda blk: pl.BlockSpec((1, rows, width), lambda bi, r: (bi, r, blk))
    return pl.pallas_call(
        kern,
        out_shape=jax.ShapeDtypeStruct((b, s, width), F32),
        grid=(b, s // rows),
        in_specs=[spec(2), spec(3), spec(4), spec(5),
                  pl.BlockSpec((1, width), lambda bi, r: (0, 0)),
                  pl.BlockSpec((1, B_DK), lambda bi, r: (0, 0))],
        out_specs=pl.BlockSpec((1, rows, width), lambda bi, r: (bi, r, 0)),
        scratch_shapes=[pltpu.VMEM((B_HEADS, B_DK, B_DK), F32)],
        compiler_params=_params(("parallel", "arbitrary")),
        name="hgrn2",
    )(proj, proj, proj, proj, lb.reshape(1, width), o_g.reshape(1, B_DK))


def _proj2_res_kernel(res_ref, a_ref, b_ref, wa_ref, wb_ref, o_ref):
    o_ref[...] = (res_ref[...]
                  + jnp.dot(a_ref[...].astype(BF16), wa_ref[...], preferred_element_type=F32)
                  + jnp.dot(b_ref[...].astype(BF16), wb_ref[...], preferred_element_type=F32))


def proj2_residual(res, a, b, wa, wb, *, tm):
    n, d = res.shape
    ka, kb = a.shape[1], b.shape[1]
    return pl.pallas_call(
        _proj2_res_kernel,
        out_shape=jax.ShapeDtypeStruct((n, d), F32),
        grid=(n // tm,),
        in_specs=[pl.BlockSpec((tm, d), lambda i: (i, 0)),
                  pl.BlockSpec((tm, ka), lambda i: (i, 0)),
                  pl.BlockSpec((tm, kb), lambda i: (i, 0)),
                  pl.BlockSpec((ka, d), lambda i: (0, 0)),
                  pl.BlockSpec((kb, d), lambda i: (0, 0))],
        out_specs=pl.BlockSpec((tm, d), lambda i: (i, 0)),
        compiler_params=_params(("parallel",)),
        name="proj2_residual",
    )(res, a, b, wa, wb)


def _gmlp_kernel(h_ref, g_ref, win_ref, lng_ref, lnb_ref, wsp_ref, bsp_ref, wout_ref, o_ref,
                 gated_ref, *, nchunk):
    x = h_ref[0]
    width = x.shape[1]
    xn = _rms(x, g_ref[...]).astype(BF16)
    uv = jnp.dot(xn, win_ref[...], preferred_element_type=F32)
    uv = 0.5 * uv * (1.0 + lax.erf(uv * (0.5 ** 0.5)))
    u = uv[:, :width]
    v = uv[:, width:]
    mu = jnp.mean(v, axis=-1, keepdims=True)
    vc = v - mu
    var = jnp.mean(vc * vc, axis=-1, keepdims=True)
    vn = (vc * lax.rsqrt(var + EPS) * lng_ref[...] + lnb_ref[...]).astype(BF16)
    row = lax.broadcasted_iota(I32, (GM_CHUNK, GM_CHUNK), 0)
    col = lax.broadcasted_iota(I32, (GM_CHUNK, GM_CHUNK), 1)
    gw = width // GM_GROUPS
    for gi in range(GM_GROUPS):
        wg = jnp.where(col <= row, wsp_ref[gi], 0.0).astype(BF16)
        cs = slice(gi * gw, (gi + 1) * gw)
        for c in range(nchunk):
            rs = slice(c * GM_CHUNK, (c + 1) * GM_CHUNK)
            mixed = jnp.dot(wg, vn[rs, cs], preferred_element_type=F32) + bsp_ref[:, cs]
            gated_ref[rs, cs] = (u[rs, cs] * mixed).astype(BF16)
    o_ref[0] = x + jnp.dot(gated_ref[...], wout_ref[...], preferred_element_type=F32)


def gmlp(h, g, w_in, ln_g, ln_b, w_sp, b_sp, w_out, *, rows):
    b, s, d = h.shape
    width = w_out.shape[0]
    gw = width // GM_GROUPS
    bsp_full = jnp.repeat(jnp.transpose(b_sp), gw, axis=1)
    kern = functools.partial(_gmlp_kernel, nchunk=rows // GM_CHUNK)
    const = lambda shape: pl.BlockSpec(shape, lambda bi, r: (0,) * len(shape))
    return pl.pallas_call(
        kern,
        out_shape=jax.ShapeDtypeStruct((b, s, d), F32),
        grid=(b, s // rows),
        in_specs=[pl.BlockSpec((1, rows, d), lambda bi, r: (bi, r, 0)),
                  const((1, d)), const((d, 2 * width)), const((1, width)), const((1, width)),
                  const((GM_GROUPS, GM_CHUNK, GM_CHUNK)), const((GM_CHUNK, width)),
                  const((width, d))],
        out_specs=pl.BlockSpec((1, rows, d), lambda bi, r: (bi, r, 0)),
        scratch_shapes=[pltpu.VMEM((rows, width), BF16)],
        compiler_params=_params(("parallel", "parallel")),
        name="gmlp",
    )(h, g.reshape(1, d), w_in.astype(BF16), ln_g.reshape(1, width), ln_b.reshape(1, width),
      w_sp, bsp_full, w_out.astype(BF16))


def _xattn_kernel(h_ref, g_ref, wq_ref, k_ref, v_ref, wo_ref, o_ref):
    x = h_ref[0]
    xn = _rms(x, g_ref[...]).astype(BF16)
    q = jnp.dot(xn, wq_ref[...], preferred_element_type=F32).astype(BF16)
    outs = []
    for h in range(X_HEADS):
        sl = slice(h * X_HEAD_DIM, (h + 1) * X_HEAD_DIM)
        s = lax.dot_general(q[:, sl], k_ref[0, :, sl], NT, preferred_element_type=F32)
        s = s * (X_HEAD_DIM ** -0.5)
        p = jnp.exp(s - s.max(axis=-1, keepdims=True))
        l = p.sum(axis=-1, keepdims=True)
        o = jnp.dot(p.astype(BF16), v_ref[0, :, sl], preferred_element_type=F32)
        outs.append((o * (1.0 / l)).astype(BF16))
    att = jnp.concatenate(outs, axis=1)
    o_ref[0] = x + jnp.dot(att, wo_ref[...], preferred_element_type=F32)


def xattn(h, g, wq, kv, wo, *, rows):
    b, s, d = h.shape
    m = kv.shape[1]
    xw = wq.shape[1]
    const = lambda shape: pl.BlockSpec(shape, lambda bi, r: (0,) * len(shape))
    return pl.pallas_call(
        _xattn_kernel,
        out_shape=jax.ShapeDtypeStruct((b, s, d), F32),
        grid=(b, s // rows),
        in_specs=[pl.BlockSpec((1, rows, d), lambda bi, r: (bi, r, 0)),
                  const((1, d)), const((d, xw)),
                  pl.BlockSpec((1, m, xw), lambda bi, r: (bi, 0, 0)),
                  pl.BlockSpec((1, m, xw), lambda bi, r: (bi, 0, 1)),
                  const((xw, d))],
        out_specs=pl.BlockSpec((1, rows, d), lambda bi, r: (bi, r, 0)),
        compiler_params=_params(("parallel", "parallel")),
        name="xattn",
    )(h, g.reshape(1, d), wq.astype(BF16), kv, kv, wo.astype(BF16))


def _swiglu_kernel(x_ref, g_ref, wgu_ref, wd_ref, fg_ref, o_ref, a_ref, *, ff, fchunk, final):
    x = x_ref[...]
    xn = _rms(x, g_ref[...]).astype(BF16)
    for c in range(ff // fchunk):
        gate = jnp.dot(xn, wgu_ref[:, c * fchunk:(c + 1) * fchunk], preferred_element_type=F32)
        up = jnp.dot(xn, wgu_ref[:, ff + c * fchunk:ff + (c + 1) * fchunk],
                     preferred_element_type=F32)
        a_ref[:, c * fchunk:(c + 1) * fchunk] = (gate * jax.nn.sigmoid(gate) * up).astype(BF16)
    y = x + jnp.dot(a_ref[...], wd_ref[...], preferred_element_type=F32)
    if final:
        y = _rms(y, fg_ref[...])
    o_ref[...] = y


def swiglu(x, g, w_gu, w_down, final_g, *, tm, final):
    n, d = x.shape
    ff = w_down.shape[0]
    kern = functools.partial(_swiglu_kernel, ff=ff, fchunk=256, final=final)
    once = dict(pipeline_mode=pl.Buffered(1))
    return pl.pallas_call(
        kern,
        out_shape=jax.ShapeDtypeStruct((n, d), F32),
        grid=(n // tm,),
        in_specs=[pl.BlockSpec((tm, d), lambda i: (i, 0)),
                  pl.BlockSpec((1, d), lambda i: (0, 0)),
                  pl.BlockSpec((d, 2 * ff), lambda i: (0, 0), **once),
                  pl.BlockSpec((ff, d), lambda i: (0, 0), **once),
                  pl.BlockSpec((1, d), lambda i: (0, 0))],
        out_specs=pl.BlockSpec((tm, d), lambda i: (i, 0)),
        scratch_shapes=[pltpu.VMEM((tm, ff), BF16)],
        compiler_params=_params(("parallel",)),
        name="swiglu",
    )(x, g.reshape(1, d), w_gu.astype(BF16), w_down.astype(BF16), final_g.reshape(1, d))


def _pack_in_proj(w):
    sizes = [A_HEADS * A_HEAD_DIM, A_LATENT, IDX_HEADS * IDX_DIM, IDX_DIM, IDX_HEADS,
             B_HEADS * B_DK, B_HEADS * B_DK, B_HEADS * B_DK, B_HEADS * B_DK]
    offs = [0]
    for sz in sizes:
        offs.append(offs[-1] + sz)
    q_a, c, qi, ki, wi, f_b, i_b, q_b, g_b = [w[:, offs[n]:offs[n + 1]] for n in range(9)]
    z = lambda n: jnp.zeros((w.shape[0], n), w.dtype)
    return jnp.concatenate([q_a, qi, f_b, i_b, q_b, g_b, c, ki, z(TILE - IDX_DIM),
                            wi, z(TILE - IDX_HEADS), z(TILE)], axis=1)


def kernel(x, mem, rel_bias, hgrn_lb, mix_norm, e_w_in, e_lat_norm, e_w_uk, e_w_uv, e_o_norm, e_w_out, o_w_in, o_ln_g, o_ln_b, o_w_sp, o_b_sp, o_w_out, x_norm, mem_norm, x_wq, x_wkv, x_wo, f_norm, f_w_gu, f_w_down, final_norm):
    b, s, d = x.shape
    m = mem.shape[1]
    n = b * s
    depth = mix_norm.shape[0]
    topk = min(TOPK_MAX, s // 4)
    lb_all = jnp.cumsum(jax.nn.softmax(hgrn_lb.astype(F32), axis=0), axis=0)
    bias = bias_tiles(rel_bias)
    mem2 = mem.reshape(b * m, d)

    h = x.reshape(n, d)
    for l in range(depth):
        j = l // 2
        if l % 2 == 0:
            proj = norm_matmul(h, mix_norm[l], _pack_in_proj(e_w_in[j]).astype(BF16),
                               tm=512, tn=512, out_dtype=F32).reshape(b, s, -1)
            a_out = dsa_attention(proj, e_lat_norm[j], e_w_uk[j], e_w_uv[j], bias, topk=topk)
            b_out = hgrn2(proj, lb_all[l], e_o_norm[j], rows=256)
            wa = e_w_out[j][:A_HEADS * A_HEAD_DIM].astype(BF16)
            wb = e_w_out[j][A_HEADS * A_HEAD_DIM:].astype(BF16)
            h = proj2_residual(h, a_out.reshape(n, -1), b_out.reshape(n, -1), wa, wb, tm=512)
        else:
            h = gmlp(h.reshape(b, s, d), mix_norm[l], o_w_in[j], o_ln_g[j], o_ln_b[j], o_w_sp[j],
                     o_b_sp[j], o_w_out[j], rows=512).reshape(n, d)
        kv = norm_matmul(mem2, mem_norm[l], x_wkv[l].astype(BF16), tm=512, tn=1024,
                         out_dtype=BF16).reshape(b, m, -1)
        h = xattn(h.reshape(b, s, d), x_norm[l], x_wq[l], kv, x_wo[l], rows=512).reshape(n, d)
        h = swiglu(h, f_norm[l], f_w_gu[l], f_w_down[l], final_norm, tm=512,
                   final=(l == depth - 1))
    return h.reshape(b, s, d)
```

```python
import functools

import jax
import jax.numpy as jnp
from jax import lax
from jax.experimental import pallas as pl
from jax.experimental.pallas import tpu as pltpu

F32 = jnp.float32
BF16 = jnp.bfloat16
I32 = jnp.int32

EPS = 1e-6
CHUNK = 64
TILE = 128
KT = 256
A_HEADS = 8
A_HEAD_DIM = 64
A_LATENT = 128
IDX_HEADS = 8
IDX_DIM = 64
TOPK_MAX = 256
REL_BUCKETS = 32
B_HEADS = 4
B_DK = 128
GM_CHUNK = 128
GM_GROUPS = 8
X_HEADS = 4
X_HEAD_DIM = 128
NEG = -1e30
INT_MIN = -2**31
LOG2E = 1.4426950408889634
ACC_ROWS = A_LATENT + 16
PROJ_C, PROJ_KI_LO, PROJ_WI, PROJ_KI_HI = 24, 25, 26, 27

VMEM_LIMIT = 56 * 1024 * 1024

NT = (((1,), (1,)), ((), ()))
TN = (((0,), (0,)), ((), ()))


def _rms(x, g):
    ms = jnp.mean(x * x, axis=-1, keepdims=True)
    return x * lax.rsqrt(ms + EPS) * g


def _params(sem, vmem=VMEM_LIMIT):
    return pltpu.CompilerParams(dimension_semantics=sem, vmem_limit_bytes=vmem)


def _norm_matmul_kernel(x_ref, g_ref, w_ref, o_ref, *, tn):
    xn = _rms(x_ref[...], g_ref[...]).astype(BF16)
    for c in range(o_ref.shape[1] // tn):
        cols = slice(c * tn, (c + 1) * tn)
        o_ref[:, cols] = jnp.dot(xn, w_ref[:, cols], preferred_element_type=F32).astype(o_ref.dtype)


def norm_matmul(x, g, w, *, tm, tn, out_dtype):
    n, k = x.shape
    nout = w.shape[1]
    return pl.pallas_call(
        functools.partial(_norm_matmul_kernel, tn=tn),
        out_shape=jax.ShapeDtypeStruct((n, nout), out_dtype),
        grid=(n // tm,),
        in_specs=[
            pl.BlockSpec((tm, k), lambda i: (i, 0)),
            pl.BlockSpec((1, k), lambda i: (0, 0)),
            pl.BlockSpec((k, nout), lambda i: (0, 0), pipeline_mode=pl.Buffered(1)),
        ],
        out_specs=pl.BlockSpec((tm, nout), lambda i: (i, 0)),
        compiler_params=_params(("parallel",)),
        name="norm_matmul",
    )(x, g.reshape(1, k), w)


def _bias_tiles_kernel(rb_ref, o_ref):
    s = lax.broadcasted_iota(I32, (TILE, TILE), 0)
    t = lax.broadcasted_iota(I32, (TILE, TILE), 1)
    nb = REL_BUCKETS // 2
    max_exact = nb // 2
    for kind in range(3):
        rel = s - t - TILE * kind
        n = jnp.abs(rel)
        n2 = n * n
        large = jnp.full((TILE, TILE), max_exact, I32)
        for j in range(1, nb - max_exact):
            large = large + jnp.where(n2 >= (max_exact * max_exact) * (2 ** j), 1, 0)
        bucket = jnp.where(rel > 0, nb, 0) + jnp.where(n < max_exact, n, large)
        for h in range(A_HEADS):
            acc = jnp.zeros((TILE, TILE), F32)
            for b in range(REL_BUCKETS):
                acc = jnp.where(bucket == b, rb_ref[b, h] * LOG2E, acc)
            o_ref[kind, :, h * TILE:(h + 1) * TILE] = acc


def bias_tiles(rel_bias):
    return pl.pallas_call(
        _bias_tiles_kernel,
        out_shape=jax.ShapeDtypeStruct((3, TILE, A_HEADS * TILE), F32),
        in_specs=[pl.BlockSpec(memory_space=pltpu.SMEM)],
        out_specs=pl.BlockSpec(memory_space=pltpu.VMEM),
        name="bias_tiles",
    )(rel_bias)


def _dsa_kernel(qa_ref, qi_ref, wi_ref, c_ref, kia_ref, kib_ref, latg_ref, wuk_ref, wuvt_ref, bias_ref,
                o_ref, cn_ref, ct_ref, ka_ref, kb_ref, keys_ref, hi_ref, s_ref, acc_ref,
                *, topk, ntiles):
    i = pl.program_id(1)
    hq = A_HEADS * TILE
    nt = lax.shift_right_logical(i + 2, 1)
    last = nt - 1

    @pl.when(i == 0)
    def _prep():
        cn = _rms(c_ref[0], latg_ref[...])
        cn_ref[...] = cn.astype(BF16)
        ones = jnp.ones((ACC_ROWS - A_LATENT, KT), BF16)
        for j in range(ntiles):
            ct_ref[j, :A_LATENT, :] = cn[j * KT:(j + 1) * KT, :].T.astype(BF16)
            ct_ref[j, A_LATENT:, :] = ones
        ka_ref[...] = kia_ref[0].astype(BF16)
        kb_ref[...] = kib_ref[0].astype(BF16)

    qa = qa_ref[0].astype(BF16)
    ql = jnp.concatenate(
        [jnp.dot(qa[:, (h // 2) * TILE:(h // 2 + 1) * TILE], wuk_ref[h],
                 preferred_element_type=F32) for h in range(A_HEADS)], axis=0)
    qlb = (ql * (A_HEAD_DIM ** -0.5 * LOG2E)).astype(BF16)
    qi = qi_ref[0].astype(BF16)
    qp = jnp.concatenate([qi[:, p * TILE:(p + 1) * TILE] for p in range(IDX_HEADS // 2)], axis=0)
    w = wi_ref[0] * (IDX_HEADS ** -0.5) * (IDX_DIM ** -0.5)
    wt = w.T

    row = lax.broadcasted_iota(I32, (KT, TILE), 0)
    col = lax.broadcasted_iota(I32, (KT, TILE), 1)
    first_half = (row >= TILE) | ((row >= CHUNK) & (col < CHUNK))
    second_half = (row >= TILE + CHUNK) & (col < CHUNK)
    inadm_last = jnp.where((i & 1) == 1, jnp.where(second_half, 1, 0), jnp.where(first_half, 1, 0))

    def inadmissible(j):
        return (inadm_last * jnp.where(j == last, 1, 0)) != 0

    def rows(j):
        return pl.ds(pl.multiple_of(j * KT, KT), KT)

    def p1(j, carry):
        d_even = lax.dot_general(ka_ref[rows(j), :], qp, NT, preferred_element_type=F32)
        d_odd = lax.dot_general(kb_ref[rows(j), :], qp, NT, preferred_element_type=F32)
        sc = jnp.zeros((KT, TILE), F32)
        for h in range(IDX_HEADS):
            d = (d_odd if h % 2 else d_even)[:, (h // 2) * TILE:(h // 2 + 1) * TILE]
            sc = sc + jnp.maximum(d, 0.0) * wt[h:h + 1, :]
        sc = jnp.where(inadmissible(j), -jnp.inf, sc)
        bits = pltpu.bitcast(sc, I32)
        key = jnp.where(bits < 0, bits ^ 0x7FFFFFFF, bits)
        key = jnp.where(bits == INT_MIN, 0, key)
        keys_ref[j] = key
        hi_ref[j] = lax.shift_right_arithmetic(key, 16).astype(jnp.int16)
        return carry

    lax.fori_loop(0, nt, p1, 0)

    half_min = -2 ** 15

    @pl.when(nt < ntiles)
    def _pad():
        hi_ref[nt] = jnp.full((KT, TILE), half_min, jnp.int16)

    def select(cap):
        def count16(pred):
            acc = jnp.zeros((16, TILE), jnp.int16)
            for j in range(cap):
                hit = jnp.where(pred(hi_ref[j]), jnp.int16(1), jnp.int16(0))
                parts = [hit[r * 16:(r + 1) * 16, :] for r in range(KT // 16)]
                while len(parts) > 1:
                    parts = [a + b for a, b in zip(parts[::2], parts[1::2])]
                acc = acc + parts[0]
            return acc.astype(I32).sum(axis=0, keepdims=True)

        def bisect16(target):
            def step_fn(step, lo):
                trial = lo + lax.shift_left(jnp.int32(1), 15 - step)
                t16 = trial.astype(jnp.int16)
                return jnp.where(count16(lambda k: k >= t16) >= target, trial, lo)
            return lax.fori_loop(0, 16, step_fn, jnp.full((1, TILE), half_min, I32))

        thr_hi = bisect16(topk)

        def low_halves(j, acc):
            key = keys_ref[j]
            hi = lax.shift_right_arithmetic(key, 16)
            lo = (key & 0xFFFF) + half_min
            hi_ref[j] = jnp.where(hi == thr_hi, lo, half_min).astype(jnp.int16)
            above = jnp.where(hi > thr_hi, 1, 0)
            return acc + above.reshape(KT // 8, 8, TILE).sum(axis=0)

        above_hi = lax.fori_loop(0, nt, low_halves, jnp.zeros((8, TILE), I32))
        above_hi = above_hi.sum(axis=0, keepdims=True)
        thr_lo = bisect16(topk - above_hi)
        thr_lo16 = thr_lo.astype(jnp.int16)
        above = above_hi + count16(lambda k: k > thr_lo16)
        return lax.shift_left(thr_hi, 16) + (thr_lo - half_min), (topk - above).astype(F32)

    caps = sorted(set(min(c, ntiles) for c in range(2, ntiles + 2, 2)))
    thr, need = lax.switch(lax.shift_right_logical(nt - 1, 1),
                           [functools.partial(select, c) for c in caps])

    trow = lax.broadcasted_iota(I32, (KT, KT), 0)
    tcol = lax.broadcasted_iota(I32, (KT, KT), 1)
    tri = jnp.where(tcol < trow, 1.0, 0.0).astype(BF16)

    def p3a(j, carry):
        run_eq, m8 = carry
        key = keys_ref[j]
        eq = key == thr
        eqf = jnp.where(eq, 1.0, 0.0)
        rank = jnp.dot(tri, eqf.astype(BF16), preferred_element_type=F32) + run_eq
        sel = (key > thr) | (eq & (rank < need))
        sel = sel & jnp.logical_not(inadmissible(j))
        negm = jnp.where(sel, 0.0, NEG)
        s = lax.dot_general(cn_ref[rows(j), :], qlb, NT, preferred_element_type=F32)
        m8_new = []
        for h in range(A_HEADS):
            cols = slice(h * TILE, (h + 1) * TILE)
            mh = m8[:, cols]
            for half in range(KT // TILE):
                rs = slice(half * TILE, (half + 1) * TILE)
                kind = jnp.clip(i - 2 * j - half, 0, 2)
                blk = s[rs, cols] + bias_ref[kind, :, cols] + negm[rs, :]
                s_ref[j, rs, cols] = blk
                mh = jnp.maximum(mh, blk.reshape(TILE // 8, 8, TILE).max(axis=0))
            m8_new.append(mh)
        return run_eq + eqf.sum(axis=0, keepdims=True), jnp.concatenate(m8_new, axis=1)

    _, m8 = lax.fori_loop(0, nt, p3a, (jnp.zeros((1, TILE), F32), jnp.full((8, hq), NEG, F32)))
    m = m8.max(axis=0, keepdims=True)

    def weighted(j):
        p = jnp.exp2(s_ref[j] - m)
        return jnp.dot(ct_ref[j], p.astype(BF16), preferred_element_type=F32)

    acc_ref[...] = weighted(0)

    def p3b(j, carry):
        acc_ref[...] += weighted(j)
        return carry

    lax.fori_loop(1, nt, p3b, 0)

    inv_l = 1.0 / acc_ref[A_LATENT:A_LATENT + 1, :]
    ot = (acc_ref[:A_LATENT, :] * inv_l).astype(BF16)
    for h in range(A_HEADS):
        o_ref[0, h * A_HEAD_DIM:(h + 1) * A_HEAD_DIM, :] = jnp.dot(
            wuvt_ref[h], ot[:, h * TILE:(h + 1) * TILE], preferred_element_type=F32
        ).astype(o_ref.dtype)


def dsa_attention(proj, lat_g, w_uk, w_uv, bias, *, topk):
    b, s, _ = proj.shape
    ntiles = s // KT
    kern = functools.partial(_dsa_kernel, topk=topk, ntiles=ntiles)
    wuk_t = jnp.swapaxes(w_uk, 1, 2)
    zero = jnp.zeros_like(wuk_t)
    odd = (jnp.arange(A_HEADS) % 2 == 1)[:, None, None]
    wuk_pad = jnp.concatenate([jnp.where(odd, zero, wuk_t), jnp.where(odd, wuk_t, zero)], axis=1)
    col = lambda blk: (lambda bi, i: (bi, 0, blk))
    return pl.pallas_call(
        kern,
        out_shape=jax.ShapeDtypeStruct((b, A_HEADS * A_HEAD_DIM, s), BF16),
        grid=(b, s // TILE),
        in_specs=[
            pl.BlockSpec((1, TILE, 512), lambda bi, i: (bi, i, 0)),
            pl.BlockSpec((1, TILE, 512), lambda bi, i: (bi, i, 1)),
            pl.BlockSpec((1, TILE, TILE), lambda bi, i: (bi, i, PROJ_WI)),
            pl.BlockSpec((1, s, TILE), col(PROJ_C)),
            pl.BlockSpec((1, s, TILE), col(PROJ_KI_LO)),
            pl.BlockSpec((1, s, TILE), col(PROJ_KI_HI)),
            pl.BlockSpec((1, A_LATENT), lambda bi, i: (0, 0)),
            pl.BlockSpec((A_HEADS, TILE, A_LATENT), lambda bi, i: (0, 0, 0)),
            pl.BlockSpec((A_HEADS, A_HEAD_DIM, A_LATENT), lambda bi, i: (0, 0, 0)),
            pl.BlockSpec((3, TILE, A_HEADS * TILE), lambda bi, i: (0, 0, 0)),
        ],
        out_specs=pl.BlockSpec((1, A_HEADS * A_HEAD_DIM, TILE), lambda bi, i: (bi, 0, i)),
        scratch_shapes=[
            pltpu.VMEM((s, A_LATENT), BF16),
            pltpu.VMEM((ntiles, ACC_ROWS, KT), BF16),
            pltpu.VMEM((s, TILE), BF16),
            pltpu.VMEM((s, TILE), BF16),
            pltpu.VMEM((ntiles, KT, TILE), I32),
            pltpu.VMEM((ntiles, KT, TILE), jnp.int16),
            pltpu.VMEM((ntiles, KT, A_HEADS * TILE), F32),
            pltpu.VMEM((ACC_ROWS, A_HEADS * TILE), F32),
        ],
        compiler_params=_params(("parallel", "arbitrary")),
        name="dsa_attention",
    )(proj, proj, proj, proj, proj, proj, lat_g.reshape(1, A_LATENT), wuk_pad.astype(BF16),
      jnp.swapaxes(w_uv, 1, 2).astype(BF16), bias)


def _hgrn_kernel(f_ref, v_ref, q_ref, g_ref, lb_ref, og_ref, o_ref, st_ref, *, nchunk):
    @pl.when(pl.program_id(1) == 0)
    def _():
        st_ref[...] = jnp.zeros(st_ref.shape, F32)

    lb = lb_ref[...]
    og = og_ref[...]
    row = lax.broadcasted_iota(I32, (CHUNK, CHUNK), 0)
    col = lax.broadcasted_iota(I32, (CHUNK, CHUNK), 1)
    causal = col <= row
    tri = jnp.where(causal, 1.0, 0.0).astype(BF16)
    mid = CHUNK // 2 - 1

    for c in range(nchunk):
        rows = slice(c * CHUNK, (c + 1) * CHUNK)
        f = lb + (1.0 - lb) * jax.nn.sigmoid(f_ref[0, rows, :])
        g = jnp.log(f)
        k = 1.0 - f
        q = q_ref[0, rows, :]
        qf = q * jax.nn.sigmoid(q)
        vb = v_ref[0, rows, :].astype(BF16)
        g_hi = g.astype(BF16)
        g_lo = (g - g_hi.astype(F32)).astype(BF16)
        bc = (jnp.dot(tri, g_hi, preferred_element_type=F32)
              + jnp.dot(tri, g_lo, preferred_element_type=F32))
        b_mid = bc[mid:mid + 1, :]
        b_end = bc[CHUNK - 1:CHUNK, :]
        qe = (qf * jnp.exp(bc - b_mid)).astype(BF16)
        ke = (k * jnp.exp(b_mid - bc)).astype(BF16)
        qb = (qf * jnp.exp(bc)).astype(BF16)
        kd = (k * jnp.exp(b_end - bc)).astype(BF16)
        dec = jnp.exp(b_end)
        gate = g_ref[0, rows, :]
        gate = gate * jax.nn.sigmoid(gate)
        for h in range(B_HEADS):
            sl = slice(h * B_DK, (h + 1) * B_DK)
            sc = lax.dot_general(qe[:, sl], ke[:, sl], NT, preferred_element_type=F32)
            sc = jnp.where(causal, sc, 0.0).astype(BF16)
            st = st_ref[h]
            o = (jnp.dot(sc, vb[:, sl], preferred_element_type=F32)
                 + lax.dot_general(qb[:, sl], st.astype(BF16), NT, preferred_element_type=F32))
            st_ref[h] = st * dec[:, sl] + lax.dot_general(vb[:, sl], kd[:, sl], TN,
                                                          preferred_element_type=F32)
            o_ref[0, rows, sl] = (_rms(o, og) * gate[:, sl]).astype(o_ref.dtype)


def hgrn2(proj, lb, o_g, *, rows):
    b, s, _ = proj.shape
    width = B_HEADS * B_DK
    kern = functools.partial(_hgrn_kernel, nchunk=rows // CHUNK)
    spec = lambda blk: pl.BlockSpec((1, rows, width), lambda bi, r: (bi, r, blk))
    return pl.pallas_call(
        kern,
        out_shape=jax.ShapeDtypeStruct((b, s, width), BF16),
        grid=(b, s // rows),
        in_specs=[spec(2), spec(3), spec(4), spec(5),
                  pl.BlockSpec((1, width), lambda bi, r: (0, 0)),
                  pl.BlockSpec((1, B_DK), lambda bi, r: (0, 0))],
        out_specs=pl.BlockSpec((1, rows, width), lambda bi, r: (bi, r, 0)),
        scratch_shapes=[pltpu.VMEM((B_HEADS, B_DK, B_DK), F32)],
        compiler_params=_params(("parallel", "arbitrary")),
        name="hgrn2",
    )(proj, proj, proj, proj, lb.reshape(1, width), o_g.reshape(1, B_DK))


def _proj2_res_kernel(res_ref, at_ref, b_ref, wa_ref, wb_ref, o_ref):
    o_ref[0] = (res_ref[0]
                + lax.dot_general(at_ref[0], wa_ref[...], TN, preferred_element_type=F32)
                + jnp.dot(b_ref[0], wb_ref[...], preferred_element_type=F32))


def proj2_residual(res, a_t, b, wa, wb, *, tm):
    bsz, s, d = res.shape
    ka, kb = a_t.shape[1], b.shape[2]
    return pl.pallas_call(
        _proj2_res_kernel,
        out_shape=jax.ShapeDtypeStruct((bsz, s, d), F32),
        grid=(bsz, s // tm),
        in_specs=[pl.BlockSpec((1, tm, d), lambda bi, i: (bi, i, 0)),
                  pl.BlockSpec((1, ka, tm), lambda bi, i: (bi, 0, i)),
                  pl.BlockSpec((1, tm, kb), lambda bi, i: (bi, i, 0)),
                  pl.BlockSpec((ka, d), lambda bi, i: (0, 0)),
                  pl.BlockSpec((kb, d), lambda bi, i: (0, 0))],
        out_specs=pl.BlockSpec((1, tm, d), lambda bi, i: (bi, i, 0)),
        compiler_params=_params(("parallel", "parallel")),
        name="proj2_residual",
    )(res, a_t, b, wa, wb)


def _gmlp_kernel(h_ref, g_ref, win_ref, lng_ref, lnb_ref, wsp_ref, bsp_ref, wout_ref, o_ref,
                 gated_ref, *, nchunk):
    x = h_ref[0]
    width = x.shape[1]
    xn = _rms(x, g_ref[...]).astype(BF16)
    uv = jnp.dot(xn, win_ref[...], preferred_element_type=F32)
    uv = 0.5 * uv * (1.0 + lax.erf(uv * (0.5 ** 0.5)))
    u = uv[:, :width]
    v = uv[:, width:]
    mu = jnp.mean(v, axis=-1, keepdims=True)
    vc = v - mu
    var = jnp.mean(vc * vc, axis=-1, keepdims=True)
    vn = (vc * lax.rsqrt(var + EPS) * lng_ref[...] + lnb_ref[...]).astype(BF16)
    row = lax.broadcasted_iota(I32, (GM_CHUNK, GM_CHUNK), 0)
    col = lax.broadcasted_iota(I32, (GM_CHUNK, GM_CHUNK), 1)
    gw = width // GM_GROUPS
    for gi in range(GM_GROUPS):
        wg = jnp.where(col <= row, wsp_ref[gi], 0.0).astype(BF16)
        cs = slice(gi * gw, (gi + 1) * gw)
        for c in range(nchunk):
            rs = slice(c * GM_CHUNK, (c + 1) * GM_CHUNK)
            mixed = jnp.dot(wg, vn[rs, cs], preferred_element_type=F32) + bsp_ref[:, cs]
            gated_ref[rs, cs] = (u[rs, cs] * mixed).astype(BF16)
    o_ref[0] = x + jnp.dot(gated_ref[...], wout_ref[...], preferred_element_type=F32)


def gmlp(h, g, w_in, ln_g, ln_b, w_sp, b_sp, w_out, *, rows):
    b, s, d = h.shape
    width = w_out.shape[0]
    gw = width // GM_GROUPS
    bsp_full = jnp.repeat(jnp.transpose(b_sp), gw, axis=1)
    kern = functools.partial(_gmlp_kernel, nchunk=rows // GM_CHUNK)
    const = lambda shape: pl.BlockSpec(shape, lambda bi, r: (0,) * len(shape))
    return pl.pallas_call(
        kern,
        out_shape=jax.ShapeDtypeStruct((b, s, d), F32),
        grid=(b, s // rows),
        in_specs=[pl.BlockSpec((1, rows, d), lambda bi, r: (bi, r, 0)),
                  const((1, d)), const((d, 2 * width)), const((1, width)), const((1, width)),
                  const((GM_GROUPS, GM_CHUNK, GM_CHUNK)), const((GM_CHUNK, width)),
                  const((width, d))],
        out_specs=pl.BlockSpec((1, rows, d), lambda bi, r: (bi, r, 0)),
        scratch_shapes=[pltpu.VMEM((rows, width), BF16)],
        compiler_params=_params(("parallel", "parallel")),
        name="gmlp",
    )(h, g.reshape(1, d), w_in.astype(BF16), ln_g.reshape(1, width), ln_b.reshape(1, width),
      w_sp, bsp_full, w_out.astype(BF16))


def _xattn_kernel(h_ref, g_ref, wq_ref, k_ref, v_ref, wo_ref, o_ref):
    x = h_ref[0]
    xn = _rms(x, g_ref[...]).astype(BF16)
    q = jnp.dot(xn, wq_ref[...], preferred_element_type=F32).astype(BF16)
    outs = []
    for h in range(X_HEADS):
        sl = slice(h * X_HEAD_DIM, (h + 1) * X_HEAD_DIM)
        s = lax.dot_general(q[:, sl], k_ref[0, :, sl], NT, preferred_element_type=F32)
        s = s * (X_HEAD_DIM ** -0.5)
        p = jnp.exp(s - s.max(axis=-1, keepdims=True))
        l = p.sum(axis=-1, keepdims=True)
        o = jnp.dot(p.astype(BF16), v_ref[0, :, sl], preferred_element_type=F32)
        outs.append((o * (1.0 / l)).astype(BF16))
    att = jnp.concatenate(outs, axis=1)
    o_ref[0] = x + jnp.dot(att, wo_ref[...], preferred_element_type=F32)


def xattn(h, g, wq, kv, wo, *, rows):
    b, s, d = h.shape
    m = kv.shape[1]
    xw = wq.shape[1]
    const = lambda shape: pl.BlockSpec(shape, lambda bi, r: (0,) * len(shape))
    return pl.pallas_call(
        _xattn_kernel,
        out_shape=jax.ShapeDtypeStruct((b, s, d), F32),
        grid=(b, s // rows),
        in_specs=[pl.BlockSpec((1, rows, d), lambda bi, r: (bi, r, 0)),
                  const((1, d)), const((d, xw)),
                  pl.BlockSpec((1, m, xw), lambda bi, r: (bi, 0, 0)),
                  pl.BlockSpec((1, m, xw), lambda bi, r: (bi, 0, 1)),
                  const((xw, d))],
        out_specs=pl.BlockSpec((1, rows, d), lambda bi, r: (bi, r, 0)),
        compiler_params=_params(("parallel", "parallel")),
        name="xattn",
    )(h, g.reshape(1, d), wq.astype(BF16), kv, kv, wo.astype(BF16))


def _swiglu_kernel(x_ref, g_ref, wgu_ref, wd_ref, fg_ref, o_ref, a_ref, *, ff, fchunk, final):
    x = x_ref[...]
    xn = _rms(x, g_ref[...]).astype(BF16)
    for c in range(ff // fchunk):
        gate = jnp.dot(xn, wgu_ref[:, c * fchunk:(c + 1) * fchunk], preferred_element_type=F32)
        up = jnp.dot(xn, wgu_ref[:, ff + c * fchunk:ff + (c + 1) * fchunk],
                     preferred_element_type=F32)
        a_ref[:, c * fchunk:(c + 1) * fchunk] = (gate * jax.nn.sigmoid(gate) * up).astype(BF16)
    y = x + jnp.dot(a_ref[...], wd_ref[...], preferred_element_type=F32)
    if final:
        y = _rms(y, fg_ref[...])
    o_ref[...] = y


def swiglu(x, g, w_gu, w_down, final_g, *, tm, final):
    n, d = x.shape
    ff = w_down.shape[0]
    kern = functools.partial(_swiglu_kernel, ff=ff, fchunk=256, final=final)
    once = dict(pipeline_mode=pl.Buffered(1))
    return pl.pallas_call(
        kern,
        out_shape=jax.ShapeDtypeStruct((n, d), F32),
        grid=(n // tm,),
        in_specs=[pl.BlockSpec((tm, d), lambda i: (i, 0)),
                  pl.BlockSpec((1, d), lambda i: (0, 0)),
                  pl.BlockSpec((d, 2 * ff), lambda i: (0, 0), **once),
                  pl.BlockSpec((ff, d), lambda i: (0, 0), **once),
                  pl.BlockSpec((1, d), lambda i: (0, 0))],
        out_specs=pl.BlockSpec((tm, d), lambda i: (i, 0)),
        scratch_shapes=[pltpu.VMEM((tm, ff), BF16)],
        compiler_params=_params(("parallel",)),
        name="swiglu",
    )(x, g.reshape(1, d), w_gu.astype(BF16), w_down.astype(BF16), final_g.reshape(1, d))


def _pack_in_proj(w):
    sizes = [A_HEADS * A_HEAD_DIM, A_LATENT, IDX_HEADS * IDX_DIM, IDX_DIM, IDX_HEADS,
             B_HEADS * B_DK, B_HEADS * B_DK, B_HEADS * B_DK, B_HEADS * B_DK]
    offs = [0]
    for sz in sizes:
        offs.append(offs[-1] + sz)
    q_a, c, qi, ki, wi, f_b, i_b, q_b, g_b = [w[:, offs[n]:offs[n + 1]] for n in range(9)]
    z = lambda n: jnp.zeros((w.shape[0], n), w.dtype)
    return jnp.concatenate([q_a, qi, f_b, i_b, q_b, g_b, c, ki, z(TILE - IDX_DIM),
                            wi, z(TILE - IDX_HEADS), z(TILE - IDX_DIM), ki], axis=1)


def kernel(x, mem, rel_bias, hgrn_lb, mix_norm, e_w_in, e_lat_norm, e_w_uk, e_w_uv, e_o_norm, e_w_out, o_w_in, o_ln_g, o_ln_b, o_w_sp, o_b_sp, o_w_out, x_norm, mem_norm, x_wq, x_wkv, x_wo, f_norm, f_w_gu, f_w_down, final_norm):
    b, s, d = x.shape
    m = mem.shape[1]
    n = b * s
    depth = mix_norm.shape[0]
    topk = min(TOPK_MAX, s // 4)
    lb_all = jnp.cumsum(jax.nn.softmax(hgrn_lb.astype(F32), axis=0), axis=0)
    bias = bias_tiles(rel_bias)
    mem2 = mem.reshape(b * m, d)

    h = x.reshape(n, d)
    for l in range(depth):
        j = l // 2
        if l % 2 == 0:
            proj = norm_matmul(h, mix_norm[l], _pack_in_proj(e_w_in[j]).astype(BF16),
                               tm=512, tn=512, out_dtype=F32).reshape(b, s, -1)
            a_out = dsa_attention(proj, e_lat_norm[j], e_w_uk[j], e_w_uv[j], bias, topk=topk)
            b_out = hgrn2(proj, lb_all[l], e_o_norm[j], rows=256)
            wa = e_w_out[j][:A_HEADS * A_HEAD_DIM].astype(BF16)
            wb = e_w_out[j][A_HEADS * A_HEAD_DIM:].astype(BF16)
            h = proj2_residual(h.reshape(b, s, d), a_out, b_out, wa, wb, tm=512).reshape(n, d)
        else:
            h = gmlp(h.reshape(b, s, d), mix_norm[l], o_w_in[j], o_ln_g[j], o_ln_b[j], o_w_sp[j],
                     o_b_sp[j], o_w_out[j], rows=512).reshape(n, d)
        kv = norm_matmul(mem2, mem_norm[l], x_wkv[l].astype(BF16), tm=512, tn=1024,
                         out_dtype=BF16).reshape(b, m, -1)
        h = xattn(h.reshape(b, s, d), x_norm[l], x_wq[l], kv, x_wo[l], rows=512).reshape(n, d)
        h = swiglu(h, f_norm[l], f_w_gu[l], f_w_down[l], final_norm, tm=512,
                   final=(l == depth - 1))
    return h.reshape(b, s, d)
```

```python
import functools

import jax
import jax.numpy as jnp
from jax import lax
from jax.experimental import pallas as pl
from jax.experimental.pallas import tpu as pltpu

F32 = jnp.float32
BF16 = jnp.bfloat16
I32 = jnp.int32

EPS = 1e-6
CHUNK = 64
TILE = 128
KT = 256
A_HEADS = 8
A_HEAD_DIM = 64
A_LATENT = 128
IDX_HEADS = 8
IDX_DIM = 64
TOPK_MAX = 256
REL_BUCKETS = 32
B_HEADS = 4
B_DK = 128
GM_CHUNK = 128
GM_GROUPS = 8
X_HEADS = 4
X_HEAD_DIM = 128
NEG = -1e30
INT_MIN = -2**31
LOG2E = 1.4426950408889634
ACC_ROWS = A_LATENT + 16
PROJ_C, PROJ_KI_LO, PROJ_WI, PROJ_KI_HI = 24, 25, 26, 27

VMEM_LIMIT = 56 * 1024 * 1024

NT = (((1,), (1,)), ((), ()))
TN = (((0,), (0,)), ((), ()))


def _rms(x, g):
    ms = jnp.mean(x * x, axis=-1, keepdims=True)
    return x * lax.rsqrt(ms + EPS) * g


def _params(sem, vmem=VMEM_LIMIT):
    return pltpu.CompilerParams(dimension_semantics=sem, vmem_limit_bytes=vmem)


def _norm_matmul_kernel(x_ref, g_ref, w_ref, o_ref, *, tn):
    xn = _rms(x_ref[...], g_ref[...]).astype(BF16)
    for c in range(o_ref.shape[1] // tn):
        cols = slice(c * tn, (c + 1) * tn)
        o_ref[:, cols] = jnp.dot(xn, w_ref[:, cols], preferred_element_type=F32).astype(o_ref.dtype)


def norm_matmul(x, g, w, *, tm, tn, out_dtype):
    n, k = x.shape
    nout = w.shape[1]
    return pl.pallas_call(
        functools.partial(_norm_matmul_kernel, tn=tn),
        out_shape=jax.ShapeDtypeStruct((n, nout), out_dtype),
        grid=(n // tm,),
        in_specs=[
            pl.BlockSpec((tm, k), lambda i: (i, 0)),
            pl.BlockSpec((1, k), lambda i: (0, 0)),
            pl.BlockSpec((k, nout), lambda i: (0, 0), pipeline_mode=pl.Buffered(1)),
        ],
        out_specs=pl.BlockSpec((tm, nout), lambda i: (i, 0)),
        compiler_params=_params(("parallel",)),
        name="norm_matmul",
    )(x, g.reshape(1, k), w)


def _bias_tiles_kernel(rb_ref, o_ref):
    s = lax.broadcasted_iota(I32, (TILE, TILE), 0)
    t = lax.broadcasted_iota(I32, (TILE, TILE), 1)
    nb = REL_BUCKETS // 2
    max_exact = nb // 2
    for kind in range(3):
        rel = s - t - TILE * kind
        n = jnp.abs(rel)
        n2 = n * n
        large = jnp.full((TILE, TILE), max_exact, I32)
        for j in range(1, nb - max_exact):
            large = large + jnp.where(n2 >= (max_exact * max_exact) * (2 ** j), 1, 0)
        bucket = jnp.where(rel > 0, nb, 0) + jnp.where(n < max_exact, n, large)
        for h in range(A_HEADS):
            acc = jnp.zeros((TILE, TILE), F32)
            for b in range(REL_BUCKETS):
                acc = jnp.where(bucket == b, rb_ref[b, h] * LOG2E, acc)
            o_ref[kind, :, h * TILE:(h + 1) * TILE] = acc


def bias_tiles(rel_bias):
    return pl.pallas_call(
        _bias_tiles_kernel,
        out_shape=jax.ShapeDtypeStruct((3, TILE, A_HEADS * TILE), F32),
        in_specs=[pl.BlockSpec(memory_space=pltpu.SMEM)],
        out_specs=pl.BlockSpec(memory_space=pltpu.VMEM),
        name="bias_tiles",
    )(rel_bias)


def _dsa_kernel(qa_ref, qi_ref, wi_ref, c_ref, kia_ref, kib_ref, latg_ref, wuk_ref, wuvt_ref, bias_ref,
                o_ref, cn_ref, ct_ref, ka_ref, kb_ref, keys_ref, hi_ref, s_ref, acc_ref,
                d0_ref, d1_ref, r0_ref, r1_ref, p0_ref, p1_ref, *, topk, ntiles):
    i = pl.program_id(1)
    hq = A_HEADS * TILE
    nt = lax.shift_right_logical(i + 2, 1)
    last = nt - 1

    @pl.when(i == 0)
    def _prep():
        cn = _rms(c_ref[0], latg_ref[...])
        cn_ref[...] = cn.astype(BF16)
        ones = jnp.ones((ACC_ROWS - A_LATENT, KT), BF16)
        for j in range(ntiles):
            ct_ref[j, :A_LATENT, :] = cn[j * KT:(j + 1) * KT, :].T.astype(BF16)
            ct_ref[j, A_LATENT:, :] = ones
        ka_ref[...] = kia_ref[0].astype(BF16)
        kb_ref[...] = kib_ref[0].astype(BF16)

    qa = qa_ref[0].astype(BF16)
    ql = jnp.concatenate(
        [jnp.dot(qa[:, (h // 2) * TILE:(h // 2 + 1) * TILE], wuk_ref[h],
                 preferred_element_type=F32) for h in range(A_HEADS)], axis=0)
    qlb = (ql * (A_HEAD_DIM ** -0.5 * LOG2E)).astype(BF16)
    qi = qi_ref[0].astype(BF16)
    qp = jnp.concatenate([qi[:, p * TILE:(p + 1) * TILE] for p in range(IDX_HEADS // 2)], axis=0)
    w = wi_ref[0] * (IDX_HEADS ** -0.5) * (IDX_DIM ** -0.5)
    wt = w.T

    row = lax.broadcasted_iota(I32, (KT, TILE), 0)
    col = lax.broadcasted_iota(I32, (KT, TILE), 1)
    first_half = (row >= TILE) | ((row >= CHUNK) & (col < CHUNK))
    second_half = (row >= TILE + CHUNK) & (col < CHUNK)
    inadm_last = jnp.where((i & 1) == 1, jnp.where(second_half, 1, 0), jnp.where(first_half, 1, 0))

    def inadmissible(j):
        return (inadm_last * jnp.where(j == last, 1, 0)) != 0

    def rows(j):
        return pl.ds(pl.multiple_of(j * KT, KT), KT)

    def pipelined(prepare, consume, carry):
        prepare(0, 0)

        def pair(t, carry):
            j = 2 * t
            prepare(j + 1, 1)
            carry = consume(j, 0, carry)
            prepare(jnp.minimum(j + 2, last), 0)
            return consume(j + 1, 1, carry)

        carry = lax.fori_loop(0, lax.shift_right_logical(nt, 1), pair, carry)
        return lax.cond((nt & 1) == 1, lambda c: consume(last, 0, c), lambda c: c, carry)

    half_w = (IDX_HEADS // 2) * TILE
    d_refs = (d0_ref, d1_ref)

    def dots(j, slot):
        ks = rows(j)
        d_ref = d_refs[slot]
        d_ref[:, :half_w] = lax.dot_general(ka_ref[ks, :], qp, NT, preferred_element_type=F32)
        d_ref[:, half_w:] = lax.dot_general(kb_ref[ks, :], qp, NT, preferred_element_type=F32)
        s = lax.dot_general(cn_ref[ks, :], qlb, NT, preferred_element_type=F32)
        for half in range(KT // TILE):
            rs = slice(half * TILE, (half + 1) * TILE)
            s_ref[j, rs, :] = s[rs, :] + bias_ref[jnp.clip(i - 2 * j - half, 0, 2)]

    def score_keys(j, slot, carry):
        d_ref = d_refs[slot]
        sc = jnp.zeros((KT, TILE), F32)
        for h in range(IDX_HEADS):
            c0 = (h % 2) * half_w + (h // 2) * TILE
            sc = sc + jnp.maximum(d_ref[:, c0:c0 + TILE], 0.0) * wt[h:h + 1, :]
        sc = jnp.where(inadmissible(j), -jnp.inf, sc)
        bits = pltpu.bitcast(sc, I32)
        key = jnp.where(bits < 0, bits ^ 0x7FFFFFFF, bits)
        key = jnp.where(bits == INT_MIN, 0, key)
        keys_ref[j] = key
        hi_ref[j] = lax.shift_right_arithmetic(key, 16).astype(jnp.int16)
        return carry

    pipelined(dots, score_keys, 0)

    half_min = -2 ** 15

    @pl.when(nt < ntiles)
    def _pad():
        hi_ref[nt] = jnp.full((KT, TILE), half_min, jnp.int16)

    def select(cap):
        def count16(pred):
            acc = jnp.zeros((16, TILE), jnp.int16)
            for j in range(cap):
                hit = jnp.where(pred(hi_ref[j]), jnp.int16(1), jnp.int16(0))
                parts = [hit[r * 16:(r + 1) * 16, :] for r in range(KT // 16)]
                while len(parts) > 1:
                    parts = [a + b for a, b in zip(parts[::2], parts[1::2])]
                acc = acc + parts[0]
            return acc.astype(I32).sum(axis=0, keepdims=True)

        def bisect16(target):
            def step_fn(step, lo):
                trial = lo + lax.shift_left(jnp.int32(1), 15 - step)
                t16 = trial.astype(jnp.int16)
                return jnp.where(count16(lambda k: k >= t16) >= target, trial, lo)
            return lax.fori_loop(0, 16, step_fn, jnp.full((1, TILE), half_min, I32))

        thr_hi = bisect16(topk)

        def low_halves(j, acc):
            key = keys_ref[j]
            hi = lax.shift_right_arithmetic(key, 16)
            lo = (key & 0xFFFF) + half_min
            hi_ref[j] = jnp.where(hi == thr_hi, lo, half_min).astype(jnp.int16)
            above = jnp.where(hi > thr_hi, 1, 0)
            return acc + above.reshape(KT // 8, 8, TILE).sum(axis=0)

        above_hi = lax.fori_loop(0, nt, low_halves, jnp.zeros((8, TILE), I32))
        above_hi = above_hi.sum(axis=0, keepdims=True)
        thr_lo = bisect16(topk - above_hi)
        thr_lo16 = thr_lo.astype(jnp.int16)
        above = above_hi + count16(lambda k: k > thr_lo16)
        return lax.shift_left(thr_hi, 16) + (thr_lo - half_min), (topk - above).astype(F32)

    caps = sorted(set(min(c, ntiles) for c in range(2, ntiles + 2, 2)))
    thr, need = lax.switch(lax.shift_right_logical(nt - 1, 1),
                           [functools.partial(select, c) for c in caps])

    trow = lax.broadcasted_iota(I32, (KT, KT), 0)
    tcol = lax.broadcasted_iota(I32, (KT, KT), 1)
    tri = jnp.where(tcol < trow, 1.0, 0.0).astype(BF16)

    r_refs = (r0_ref, r1_ref)
    p_refs = (p0_ref, p1_ref)

    def tie_ranks(j, slot):
        eqf = jnp.where(keys_ref[j] == thr, 1.0, 0.0).astype(BF16)
        r_refs[slot][...] = jnp.dot(tri, eqf, preferred_element_type=F32)

    def mask_tile(j, slot, carry):
        run_eq, m8 = carry
        key = keys_ref[j]
        eq = key == thr
        sel = (key > thr) | (eq & (r_refs[slot][...] + run_eq < need))
        sel = sel & jnp.logical_not(inadmissible(j))
        negm = jnp.where(sel, 0.0, NEG)
        m8_new = []
        for h in range(A_HEADS):
            cols = slice(h * TILE, (h + 1) * TILE)
            mh = m8[:, cols]
            for half in range(KT // TILE):
                rs = slice(half * TILE, (half + 1) * TILE)
                blk = s_ref[j, rs, cols] + negm[rs, :]
                s_ref[j, rs, cols] = blk
                mh = jnp.maximum(mh, blk.reshape(TILE // 8, 8, TILE).max(axis=0))
            m8_new.append(mh)
        run_eq = run_eq + jnp.where(eq, 1.0, 0.0).sum(axis=0, keepdims=True)
        return run_eq, jnp.concatenate(m8_new, axis=1)

    _, m8 = pipelined(tie_ranks, mask_tile,
                      (jnp.zeros((1, TILE), F32), jnp.full((8, hq), NEG, F32)))
    m = m8.max(axis=0, keepdims=True)

    def weights(j, slot):
        p_refs[slot][...] = jnp.exp2(s_ref[j] - m).astype(BF16)

    def weighted(j, slot, carry):
        acc_ref[...] += jnp.dot(ct_ref[j], p_refs[slot][...], preferred_element_type=F32)
        return carry

    acc_ref[...] = jnp.zeros(acc_ref.shape, F32)
    pipelined(weights, weighted, 0)

    inv_l = 1.0 / acc_ref[A_LATENT:A_LATENT + 1, :]
    ot = (acc_ref[:A_LATENT, :] * inv_l).astype(BF16)
    for h in range(A_HEADS):
        o_ref[0, h * A_HEAD_DIM:(h + 1) * A_HEAD_DIM, :] = jnp.dot(
            wuvt_ref[h], ot[:, h * TILE:(h + 1) * TILE], preferred_element_type=F32
        ).astype(o_ref.dtype)


def dsa_attention(proj, lat_g, w_uk, w_uv, bias, *, topk):
    b, s, _ = proj.shape
    ntiles = s // KT
    kern = functools.partial(_dsa_kernel, topk=topk, ntiles=ntiles)
    wuk_t = jnp.swapaxes(w_uk, 1, 2)
    zero = jnp.zeros_like(wuk_t)
    odd = (jnp.arange(A_HEADS) % 2 == 1)[:, None, None]
    wuk_pad = jnp.concatenate([jnp.where(odd, zero, wuk_t), jnp.where(odd, wuk_t, zero)], axis=1)
    col = lambda blk: (lambda bi, i: (bi, 0, blk))
    return pl.pallas_call(
        kern,
        out_shape=jax.ShapeDtypeStruct((b, A_HEADS * A_HEAD_DIM, s), BF16),
        grid=(b, s // TILE),
        in_specs=[
            pl.BlockSpec((1, TILE, 512), lambda bi, i: (bi, i, 0)),
            pl.BlockSpec((1, TILE, 512), lambda bi, i: (bi, i, 1)),
            pl.BlockSpec((1, TILE, TILE), lambda bi, i: (bi, i, PROJ_WI)),
            pl.BlockSpec((1, s, TILE), col(PROJ_C)),
            pl.BlockSpec((1, s, TILE), col(PROJ_KI_LO)),
            pl.BlockSpec((1, s, TILE), col(PROJ_KI_HI)),
            pl.BlockSpec((1, A_LATENT), lambda bi, i: (0, 0)),
            pl.BlockSpec((A_HEADS, TILE, A_LATENT), lambda bi, i: (0, 0, 0)),
            pl.BlockSpec((A_HEADS, A_HEAD_DIM, A_LATENT), lambda bi, i: (0, 0, 0)),
            pl.BlockSpec((3, TILE, A_HEADS * TILE), lambda bi, i: (0, 0, 0)),
        ],
        out_specs=pl.BlockSpec((1, A_HEADS * A_HEAD_DIM, TILE), lambda bi, i: (bi, 0, i)),
        scratch_shapes=[
            pltpu.VMEM((s, A_LATENT), BF16),
            pltpu.VMEM((ntiles, ACC_ROWS, KT), BF16),
            pltpu.VMEM((s, TILE), BF16),
            pltpu.VMEM((s, TILE), BF16),
            pltpu.VMEM((ntiles, KT, TILE), I32),
            pltpu.VMEM((ntiles, KT, TILE), jnp.int16),
            pltpu.VMEM((ntiles, KT, A_HEADS * TILE), F32),
            pltpu.VMEM((ACC_ROWS, A_HEADS * TILE), F32),
            pltpu.VMEM((KT, IDX_HEADS * TILE), F32),
            pltpu.VMEM((KT, IDX_HEADS * TILE), F32),
            pltpu.VMEM((KT, TILE), F32),
            pltpu.VMEM((KT, TILE), F32),
            pltpu.VMEM((KT, A_HEADS * TILE), BF16),
            pltpu.VMEM((KT, A_HEADS * TILE), BF16),
        ],
        compiler_params=_params(("parallel", "arbitrary")),
        name="dsa_attention",
    )(proj, proj, proj, proj, proj, proj, lat_g.reshape(1, A_LATENT), wuk_pad.astype(BF16),
      jnp.swapaxes(w_uv, 1, 2).astype(BF16), bias)


def _hgrn_kernel(f_ref, v_ref, q_ref, g_ref, lb_ref, og_ref, o_ref, st_ref, *, nchunk):
    @pl.when(pl.program_id(1) == 0)
    def _():
        st_ref[...] = jnp.zeros(st_ref.shape, F32)

    lb = lb_ref[...]
    og = og_ref[...]
    row = lax.broadcasted_iota(I32, (CHUNK, CHUNK), 0)
    col = lax.broadcasted_iota(I32, (CHUNK, CHUNK), 1)
    causal = col <= row
    tri = jnp.where(causal, 1.0, 0.0).astype(BF16)
    mid = CHUNK // 2 - 1

    for c in range(nchunk):
        rows = slice(c * CHUNK, (c + 1) * CHUNK)
        f = lb + (1.0 - lb) * jax.nn.sigmoid(f_ref[0, rows, :])
        g = jnp.log(f)
        k = 1.0 - f
        q = q_ref[0, rows, :]
        qf = q * jax.nn.sigmoid(q)
        vb = v_ref[0, rows, :].astype(BF16)
        g_hi = g.astype(BF16)
        g_lo = (g - g_hi.astype(F32)).astype(BF16)
        bc = (jnp.dot(tri, g_hi, preferred_element_type=F32)
              + jnp.dot(tri, g_lo, preferred_element_type=F32))
        b_mid = bc[mid:mid + 1, :]
        b_end = bc[CHUNK - 1:CHUNK, :]
        qe = (qf * jnp.exp(bc - b_mid)).astype(BF16)
        ke = (k * jnp.exp(b_mid - bc)).astype(BF16)
        qb = (qf * jnp.exp(bc)).astype(BF16)
        kd = (k * jnp.exp(b_end - bc)).astype(BF16)
        dec = jnp.exp(b_end)
        gate = g_ref[0, rows, :]
        gate = gate * jax.nn.sigmoid(gate)
        for h in range(B_HEADS):
            sl = slice(h * B_DK, (h + 1) * B_DK)
            sc = lax.dot_general(qe[:, sl], ke[:, sl], NT, preferred_element_type=F32)
            sc = jnp.where(causal, sc, 0.0).astype(BF16)
            st = st_ref[h]
            o = (jnp.dot(sc, vb[:, sl], preferred_element_type=F32)
                 + lax.dot_general(qb[:, sl], st.astype(BF16), NT, preferred_element_type=F32))
            st_ref[h] = st * dec[:, sl] + lax.dot_general(vb[:, sl], kd[:, sl], TN,
                                                          preferred_element_type=F32)
            o_ref[0, rows, sl] = (_rms(o, og) * gate[:, sl]).astype(o_ref.dtype)


def hgrn2(proj, lb, o_g, *, rows):
    b, s, _ = proj.shape
    width = B_HEADS * B_DK
    kern = functools.partial(_hgrn_kernel, nchunk=rows // CHUNK)
    spec = lambda blk: pl.BlockSpec((1, rows, width), lambda bi, r: (bi, r, blk))
    return pl.pallas_call(
        kern,
        out_shape=jax.ShapeDtypeStruct((b, s, width), BF16),
        grid=(b, s // rows),
        in_specs=[spec(2), spec(3), spec(4), spec(5),
                  pl.BlockSpec((1, width), lambda bi, r: (0, 0)),
                  pl.BlockSpec((1, B_DK), lambda bi, r: (0, 0))],
        out_specs=pl.BlockSpec((1, rows, width), lambda bi, r: (bi, r, 0)),
        scratch_shapes=[pltpu.VMEM((B_HEADS, B_DK, B_DK), F32)],
        compiler_params=_params(("parallel", "arbitrary")),
        name="hgrn2",
    )(proj, proj, proj, proj, lb.reshape(1, width), o_g.reshape(1, B_DK))


def _proj2_res_kernel(res_ref, at_ref, b_ref, wa_ref, wb_ref, o_ref):
    o_ref[0] = (res_ref[0]
                + lax.dot_general(at_ref[0], wa_ref[...], TN, preferred_element_type=F32)
                + jnp.dot(b_ref[0], wb_ref[...], preferred_element_type=F32))


def proj2_residual(res, a_t, b, wa, wb, *, tm):
    bsz, s, d = res.shape
    ka, kb = a_t.shape[1], b.shape[2]
    return pl.pallas_call(
        _proj2_res_kernel,
        out_shape=jax.ShapeDtypeStruct((bsz, s, d), F32),
        grid=(bsz, s // tm),
        in_specs=[pl.BlockSpec((1, tm, d), lambda bi, i: (bi, i, 0)),
                  pl.BlockSpec((1, ka, tm), lambda bi, i: (bi, 0, i)),
                  pl.BlockSpec((1, tm, kb), lambda bi, i: (bi, i, 0)),
                  pl.BlockSpec((ka, d), lambda bi, i: (0, 0)),
                  pl.BlockSpec((kb, d), lambda bi, i: (0, 0))],
        out_specs=pl.BlockSpec((1, tm, d), lambda bi, i: (bi, i, 0)),
        compiler_params=_params(("parallel", "parallel")),
        name="proj2_residual",
    )(res, a_t, b, wa, wb)


def _gmlp_kernel(h_ref, g_ref, win_ref, lng_ref, lnb_ref, wsp_ref, bsp_ref, wout_ref, o_ref,
                 gated_ref, *, nchunk):
    x = h_ref[0]
    width = x.shape[1]
    xn = _rms(x, g_ref[...]).astype(BF16)
    uv = jnp.dot(xn, win_ref[...], preferred_element_type=F32)
    uv = 0.5 * uv * (1.0 + lax.erf(uv * (0.5 ** 0.5)))
    u = uv[:, :width]
    v = uv[:, width:]
    mu = jnp.mean(v, axis=-1, keepdims=True)
    vc = v - mu
    var = jnp.mean(vc * vc, axis=-1, keepdims=True)
    vn = (vc * lax.rsqrt(var + EPS) * lng_ref[...] + lnb_ref[...]).astype(BF16)
    row = lax.broadcasted_iota(I32, (GM_CHUNK, GM_CHUNK), 0)
    col = lax.broadcasted_iota(I32, (GM_CHUNK, GM_CHUNK), 1)
    gw = width // GM_GROUPS
    for gi in range(GM_GROUPS):
        wg = jnp.where(col <= row, wsp_ref[gi], 0.0).astype(BF16)
        cs = slice(gi * gw, (gi + 1) * gw)
        for c in range(nchunk):
            rs = slice(c * GM_CHUNK, (c + 1) * GM_CHUNK)
            mixed = jnp.dot(wg, vn[rs, cs], preferred_element_type=F32) + bsp_ref[:, cs]
            gated_ref[rs, cs] = (u[rs, cs] * mixed).astype(BF16)
    o_ref[0] = x + jnp.dot(gated_ref[...], wout_ref[...], preferred_element_type=F32)


def gmlp(h, g, w_in, ln_g, ln_b, w_sp, b_sp, w_out, *, rows):
    b, s, d = h.shape
    width = w_out.shape[0]
    gw = width // GM_GROUPS
    bsp_full = jnp.repeat(jnp.transpose(b_sp), gw, axis=1)
    kern = functools.partial(_gmlp_kernel, nchunk=rows // GM_CHUNK)
    const = lambda shape: pl.BlockSpec(shape, lambda bi, r: (0,) * len(shape))
    return pl.pallas_call(
        kern,
        out_shape=jax.ShapeDtypeStruct((b, s, d), F32),
        grid=(b, s // rows),
        in_specs=[pl.BlockSpec((1, rows, d), lambda bi, r: (bi, r, 0)),
                  const((1, d)), const((d, 2 * width)), const((1, width)), const((1, width)),
                  const((GM_GROUPS, GM_CHUNK, GM_CHUNK)), const((GM_CHUNK, width)),
                  const((width, d))],
        out_specs=pl.BlockSpec((1, rows, d), lambda bi, r: (bi, r, 0)),
        scratch_shapes=[pltpu.VMEM((rows, width), BF16)],
        compiler_params=_params(("parallel", "parallel")),
        name="gmlp",
    )(h, g.reshape(1, d), w_in.astype(BF16), ln_g.reshape(1, width), ln_b.reshape(1, width),
      w_sp, bsp_full, w_out.astype(BF16))


def _xattn_kernel(h_ref, g_ref, wq_ref, k_ref, v_ref, wo_ref, o_ref):
    x = h_ref[0]
    xn = _rms(x, g_ref[...]).astype(BF16)
    q = jnp.dot(xn, wq_ref[...], preferred_element_type=F32).astype(BF16)
    outs = []
    for h in range(X_HEADS):
        sl = slice(h * X_HEAD_DIM, (h + 1) * X_HEAD_DIM)
        s = lax.dot_general(q[:, sl], k_ref[0, :, sl], NT, preferred_element_type=F32)
        s = s * (X_HEAD_DIM ** -0.5)
        p = jnp.exp(s - s.max(axis=-1, keepdims=True))
        l = p.sum(axis=-1, keepdims=True)
        o = jnp.dot(p.astype(BF16), v_ref[0, :, sl], preferred_element_type=F32)
        outs.append((o * (1.0 / l)).astype(BF16))
    att = jnp.concatenate(outs, axis=1)
    o_ref[0] = x + jnp.dot(att, wo_ref[...], preferred_element_type=F32)


def xattn(h, g, wq, kv, wo, *, rows):
    b, s, d = h.shape
    m = kv.shape[1]
    xw = wq.shape[1]
    const = lambda shape: pl.BlockSpec(shape, lambda bi, r: (0,) * len(shape))
    return pl.pallas_call(
        _xattn_kernel,
        out_shape=jax.ShapeDtypeStruct((b, s, d), F32),
        grid=(b, s // rows),
        in_specs=[pl.BlockSpec((1, rows, d), lambda bi, r: (bi, r, 0)),
                  const((1, d)), const((d, xw)),
                  pl.BlockSpec((1, m, xw), lambda bi, r: (bi, 0, 0)),
                  pl.BlockSpec((1, m, xw), lambda bi, r: (bi, 0, 1)),
                  const((xw, d))],
        out_specs=pl.BlockSpec((1, rows, d), lambda bi, r: (bi, r, 0)),
        compiler_params=_params(("parallel", "parallel")),
        name="xattn",
    )(h, g.reshape(1, d), wq.astype(BF16), kv, kv, wo.astype(BF16))


def _swiglu_kernel(x_ref, g_ref, wgu_ref, wd_ref, fg_ref, o_ref, a_ref, *, ff, fchunk, final):
    x = x_ref[...]
    xn = _rms(x, g_ref[...]).astype(BF16)
    for c in range(ff // fchunk):
        gate = jnp.dot(xn, wgu_ref[:, c * fchunk:(c + 1) * fchunk], preferred_element_type=F32)
        up = jnp.dot(xn, wgu_ref[:, ff + c * fchunk:ff + (c + 1) * fchunk],
                     preferred_element_type=F32)
        a_ref[:, c * fchunk:(c + 1) * fchunk] = (gate * jax.nn.sigmoid(gate) * up).astype(BF16)
    y = x + jnp.dot(a_ref[...], wd_ref[...], preferred_element_type=F32)
    if final:
        y = _rms(y, fg_ref[...])
    o_ref[...] = y


def swiglu(x, g, w_gu, w_down, final_g, *, tm, final):
    n, d = x.shape
    ff = w_down.shape[0]
    kern = functools.partial(_swiglu_kernel, ff=ff, fchunk=256, final=final)
    once = dict(pipeline_mode=pl.Buffered(1))
    return pl.pallas_call(
        kern,
        out_shape=jax.ShapeDtypeStruct((n, d), F32),
        grid=(n // tm,),
        in_specs=[pl.BlockSpec((tm, d), lambda i: (i, 0)),
                  pl.BlockSpec((1, d), lambda i: (0, 0)),
                  pl.BlockSpec((d, 2 * ff), lambda i: (0, 0), **once),
                  pl.BlockSpec((ff, d), lambda i: (0, 0), **once),
                  pl.BlockSpec((1, d), lambda i: (0, 0))],
        out_specs=pl.BlockSpec((tm, d), lambda i: (i, 0)),
        scratch_shapes=[pltpu.VMEM((tm, ff), BF16)],
        compiler_params=_params(("parallel",)),
        name="swiglu",
    )(x, g.reshape(1, d), w_gu.astype(BF16), w_down.astype(BF16), final_g.reshape(1, d))


def _pack_in_proj(w):
    sizes = [A_HEADS * A_HEAD_DIM, A_LATENT, IDX_HEADS * IDX_DIM, IDX_DIM, IDX_HEADS,
             B_HEADS * B_DK, B_HEADS * B_DK, B_HEADS * B_DK, B_HEADS * B_DK]
    offs = [0]
    for sz in sizes:
        offs.append(offs[-1] + sz)
    q_a, c, qi, ki, wi, f_b, i_b, q_b, g_b = [w[:, offs[n]:offs[n + 1]] for n in range(9)]
    z = lambda n: jnp.zeros((w.shape[0], n), w.dtype)
    return jnp.concatenate([q_a, qi, f_b, i_b, q_b, g_b, c, ki, z(TILE - IDX_DIM),
                            wi, z(TILE - IDX_HEADS), z(TILE - IDX_DIM), ki], axis=1)


def kernel(x, mem, rel_bias, hgrn_lb, mix_norm, e_w_in, e_lat_norm, e_w_uk, e_w_uv, e_o_norm, e_w_out, o_w_in, o_ln_g, o_ln_b, o_w_sp, o_b_sp, o_w_out, x_norm, mem_norm, x_wq, x_wkv, x_wo, f_norm, f_w_gu, f_w_down, final_norm):
    b, s, d = x.shape
    m = mem.shape[1]
    n = b * s
    depth = mix_norm.shape[0]
    topk = min(TOPK_MAX, s // 4)
    lb_all = jnp.cumsum(jax.nn.softmax(hgrn_lb.astype(F32), axis=0), axis=0)
    bias = bias_tiles(rel_bias)
    mem2 = mem.reshape(b * m, d)

    h = x.reshape(n, d)
    for l in range(depth):
        j = l // 2
        if l % 2 == 0:
            proj = norm_matmul(h, mix_norm[l], _pack_in_proj(e_w_in[j]).astype(BF16),
                               tm=512, tn=512, out_dtype=F32).reshape(b, s, -1)
            a_out = dsa_attention(proj, e_lat_norm[j], e_w_uk[j], e_w_uv[j], bias, topk=topk)
            b_out = hgrn2(proj, lb_all[l], e_o_norm[j], rows=256)
            wa = e_w_out[j][:A_HEADS * A_HEAD_DIM].astype(BF16)
            wb = e_w_out[j][A_HEADS * A_HEAD_DIM:].astype(BF16)
            h = proj2_residual(h.reshape(b, s, d), a_out, b_out, wa, wb, tm=512).reshape(n, d)
        else:
            h = gmlp(h.reshape(b, s, d), mix_norm[l], o_w_in[j], o_ln_g[j], o_ln_b[j], o_w_sp[j],
                     o_b_sp[j], o_w_out[j], rows=512).reshape(n, d)
        kv = norm_matmul(mem2, mem_norm[l], x_wkv[l].astype(BF16), tm=512, tn=1024,
                         out_dtype=BF16).reshape(b, m, -1)
        h = xattn(h.reshape(b, s, d), x_norm[l], x_wq[l], kv, x_wo[l], rows=512).reshape(n, d)
        h = swiglu(h, f_norm[l], f_w_gu[l], f_w_down[l], final_norm, tm=512,
                   final=(l == depth - 1))
    return h.reshape(b, s, d)
```

```python
import functools

import jax
import jax.numpy as jnp
from jax import lax
from jax.experimental import pallas as pl
from jax.experimental.pallas import tpu as pltpu

F32 = jnp.float32
BF16 = jnp.bfloat16
I32 = jnp.int32

EPS = 1e-6
CHUNK = 64
TILE = 128
KT = 256
A_HEADS = 8
A_HEAD_DIM = 64
A_LATENT = 128
IDX_HEADS = 8
IDX_DIM = 64
TOPK_MAX = 256
REL_BUCKETS = 32
B_HEADS = 4
B_DK = 128
GM_CHUNK = 128
GM_GROUPS = 8
X_HEADS = 4
X_HEAD_DIM = 128
NEG = -1e30
INT_MIN = -2**31
LOG2E = 1.4426950408889634
ACC_ROWS = A_LATENT + 16
PROJ_H_COLS = PROJ_F_COLS = 1792
PROJ_H_QA, PROJ_H_QI, PROJ_H_V = 0, 1, 2
PROJ_H_KI_LO, PROJ_H_KI_HI = 12, 13
PROJ_F_F, PROJ_F_Q, PROJ_F_GATE = 0, 1, 2
PROJ_F_C, PROJ_F_WI = 12, 13

VMEM_LIMIT = 56 * 1024 * 1024

NT = (((1,), (1,)), ((), ()))
TN = (((0,), (0,)), ((), ()))


def _rms(x, g):
    ms = jnp.mean(x * x, axis=-1, keepdims=True)
    return x * lax.rsqrt(ms + EPS) * g


def _params(sem, vmem=VMEM_LIMIT):
    return pltpu.CompilerParams(dimension_semantics=sem, vmem_limit_bytes=vmem)


def _norm_matmul_kernel(x_ref, g_ref, w_ref, *o_refs, tn):
    xn = _rms(x_ref[...], g_ref[...]).astype(BF16)
    base = 0
    for o_ref in o_refs:
        for c in range(o_ref.shape[1] // tn):
            y = jnp.dot(xn, w_ref[:, base + c * tn:base + (c + 1) * tn], preferred_element_type=F32)
            o_ref[:, c * tn:(c + 1) * tn] = y.astype(o_ref.dtype)
        base += o_ref.shape[1]


def norm_matmul(x, g, w, *, tm, tn, groups):
    n, k = x.shape
    nout = w.shape[1]
    assert sum(nc for nc, _ in groups) == nout and all(nc % tn == 0 for nc, _ in groups)
    return pl.pallas_call(
        functools.partial(_norm_matmul_kernel, tn=tn),
        out_shape=tuple(jax.ShapeDtypeStruct((n, nc), dt) for nc, dt in groups),
        grid=(n // tm,),
        in_specs=[
            pl.BlockSpec((tm, k), lambda i: (i, 0)),
            pl.BlockSpec((1, k), lambda i: (0, 0)),
            pl.BlockSpec((k, nout), lambda i: (0, 0), pipeline_mode=pl.Buffered(1)),
        ],
        out_specs=tuple(pl.BlockSpec((tm, nc), lambda i: (i, 0)) for nc, _ in groups),
        compiler_params=_params(("parallel",)),
        name="norm_matmul",
    )(x, g.reshape(1, k), w)


def _bias_tiles_kernel(rb_ref, o_ref):
    s = lax.broadcasted_iota(I32, (TILE, TILE), 0)
    t = lax.broadcasted_iota(I32, (TILE, TILE), 1)
    nb = REL_BUCKETS // 2
    max_exact = nb // 2
    for kind in range(3):
        rel = s - t - TILE * kind
        n = jnp.abs(rel)
        n2 = n * n
        large = jnp.full((TILE, TILE), max_exact, I32)
        for j in range(1, nb - max_exact):
            large = large + jnp.where(n2 >= (max_exact * max_exact) * (2 ** j), 1, 0)
        bucket = jnp.where(rel > 0, nb, 0) + jnp.where(n < max_exact, n, large)
        for h in range(A_HEADS):
            acc = jnp.zeros((TILE, TILE), F32)
            for b in range(REL_BUCKETS):
                acc = jnp.where(bucket == b, rb_ref[b, h] * LOG2E, acc)
            o_ref[kind, :, h * TILE:(h + 1) * TILE] = acc


def bias_tiles(rel_bias):
    return pl.pallas_call(
        _bias_tiles_kernel,
        out_shape=jax.ShapeDtypeStruct((3, TILE, A_HEADS * TILE), F32),
        in_specs=[pl.BlockSpec(memory_space=pltpu.SMEM)],
        out_specs=pl.BlockSpec(memory_space=pltpu.VMEM),
        name="bias_tiles",
    )(rel_bias)


def _dsa_kernel(qa_ref, qi_ref, wi_ref, c_ref, ka_ref, kb_ref, latg_ref, wuk_ref, wuvt_ref, bias_ref,
                o_ref, cn_ref, ct_ref, keys_ref, hi_ref, s_ref, acc_ref,
                d0_ref, d1_ref, r0_ref, r1_ref, p0_ref, p1_ref, *, topk, ntiles):
    i = pl.program_id(1)
    hq = A_HEADS * TILE
    nt = lax.shift_right_logical(i + 2, 1)
    last = nt - 1

    @pl.when(i == 0)
    def _prep():
        cn = _rms(c_ref[0], latg_ref[...])
        cn_ref[...] = cn.astype(BF16)
        ones = jnp.ones((ACC_ROWS - A_LATENT, KT), BF16)
        for j in range(ntiles):
            ct_ref[j, :A_LATENT, :] = cn[j * KT:(j + 1) * KT, :].T.astype(BF16)
            ct_ref[j, A_LATENT:, :] = ones

    qa = qa_ref[0]
    ql = jnp.concatenate(
        [jnp.dot(qa[:, (h // 2) * TILE:(h // 2 + 1) * TILE], wuk_ref[h],
                 preferred_element_type=F32) for h in range(A_HEADS)], axis=0)
    qlb = (ql * (A_HEAD_DIM ** -0.5 * LOG2E)).astype(BF16)
    qi = qi_ref[0]
    qp = jnp.concatenate([qi[:, p * TILE:(p + 1) * TILE] for p in range(IDX_HEADS // 2)], axis=0)
    w = wi_ref[0] * (IDX_HEADS ** -0.5) * (IDX_DIM ** -0.5)
    wt = w.T

    row = lax.broadcasted_iota(I32, (KT, TILE), 0)
    col = lax.broadcasted_iota(I32, (KT, TILE), 1)
    first_half = (row >= TILE) | ((row >= CHUNK) & (col < CHUNK))
    second_half = (row >= TILE + CHUNK) & (col < CHUNK)
    inadm_last = jnp.where((i & 1) == 1, jnp.where(second_half, 1, 0), jnp.where(first_half, 1, 0))

    def inadmissible(j):
        return (inadm_last * jnp.where(j == last, 1, 0)) != 0

    def rows(j):
        return pl.ds(pl.multiple_of(j * KT, KT), KT)

    def pipelined(prepare, consume, carry):
        prepare(0, 0)

        def pair(t, carry):
            j = 2 * t
            prepare(j + 1, 1)
            carry = consume(j, 0, carry)
            prepare(jnp.minimum(j + 2, last), 0)
            return consume(j + 1, 1, carry)

        carry = lax.fori_loop(0, lax.shift_right_logical(nt, 1), pair, carry)
        return lax.cond((nt & 1) == 1, lambda c: consume(last, 0, c), lambda c: c, carry)

    half_w = (IDX_HEADS // 2) * TILE
    d_refs = (d0_ref, d1_ref)

    def dots(j, slot):
        ks = rows(j)
        d_ref = d_refs[slot]
        d_ref[:, :half_w] = lax.dot_general(ka_ref[0, ks, :], qp, NT, preferred_element_type=F32)
        d_ref[:, half_w:] = lax.dot_general(kb_ref[0, ks, :], qp, NT, preferred_element_type=F32)
        s = lax.dot_general(cn_ref[ks, :], qlb, NT, preferred_element_type=F32)
        for half in range(KT // TILE):
            rs = slice(half * TILE, (half + 1) * TILE)
            s_ref[j, rs, :] = s[rs, :] + bias_ref[jnp.clip(i - 2 * j - half, 0, 2)]

    def score_keys(j, slot, carry):
        d_ref = d_refs[slot]
        sc = jnp.zeros((KT, TILE), F32)
        for h in range(IDX_HEADS):
            c0 = (h % 2) * half_w + (h // 2) * TILE
            sc = sc + jnp.maximum(d_ref[:, c0:c0 + TILE], 0.0) * wt[h:h + 1, :]
        sc = jnp.where(inadmissible(j), -jnp.inf, sc)
        bits = pltpu.bitcast(sc, I32)
        key = jnp.where(bits < 0, bits ^ 0x7FFFFFFF, bits)
        key = jnp.where(bits == INT_MIN, 0, key)
        keys_ref[j] = key
        hi_ref[j] = lax.shift_right_arithmetic(key, 16).astype(jnp.int16)
        return carry

    pipelined(dots, score_keys, 0)

    half_min = -2 ** 15

    @pl.when(nt < ntiles)
    def _pad():
        hi_ref[nt] = jnp.full((KT, TILE), half_min, jnp.int16)

    def select(cap):
        def count16(pred):
            acc = jnp.zeros((16, TILE), jnp.int16)
            for j in range(cap):
                hit = jnp.where(pred(hi_ref[j]), jnp.int16(1), jnp.int16(0))
                parts = [hit[r * 16:(r + 1) * 16, :] for r in range(KT // 16)]
                while len(parts) > 1:
                    parts = [a + b for a, b in zip(parts[::2], parts[1::2])]
                acc = acc + parts[0]
            return acc.astype(I32).sum(axis=0, keepdims=True)

        def bisect16(target):
            def step_fn(step, lo):
                trial = lo + lax.shift_left(jnp.int32(1), 15 - step)
                t16 = trial.astype(jnp.int16)
                return jnp.where(count16(lambda k: k >= t16) >= target, trial, lo)
            return lax.fori_loop(0, 16, step_fn, jnp.full((1, TILE), half_min, I32))

        thr_hi = bisect16(topk)

        def low_halves(j, acc):
            key = keys_ref[j]
            hi = lax.shift_right_arithmetic(key, 16)
            lo = (key & 0xFFFF) + half_min
            hi_ref[j] = jnp.where(hi == thr_hi, lo, half_min).astype(jnp.int16)
            above = jnp.where(hi > thr_hi, 1, 0)
            return acc + above.reshape(KT // 8, 8, TILE).sum(axis=0)

        above_hi = lax.fori_loop(0, nt, low_halves, jnp.zeros((8, TILE), I32))
        above_hi = above_hi.sum(axis=0, keepdims=True)
        thr_lo = bisect16(topk - above_hi)
        thr_lo16 = thr_lo.astype(jnp.int16)
        above = above_hi + count16(lambda k: k > thr_lo16)
        return lax.shift_left(thr_hi, 16) + (thr_lo - half_min), (topk - above).astype(F32)

    caps = sorted(set(min(c, ntiles) for c in range(2, ntiles + 2, 2)))
    thr, need = lax.switch(lax.shift_right_logical(nt - 1, 1),
                           [functools.partial(select, c) for c in caps])

    trow = lax.broadcasted_iota(I32, (KT, KT), 0)
    tcol = lax.broadcasted_iota(I32, (KT, KT), 1)
    tri = jnp.where(tcol < trow, 1.0, 0.0).astype(BF16)

    r_refs = (r0_ref, r1_ref)
    p_refs = (p0_ref, p1_ref)

    def tie_ranks(j, slot):
        eqf = jnp.where(keys_ref[j] == thr, 1.0, 0.0).astype(BF16)
        r_refs[slot][...] = jnp.dot(tri, eqf, preferred_element_type=F32)

    def mask_tile(j, slot, carry):
        run_eq, m8 = carry
        key = keys_ref[j]
        eq = key == thr
        sel = (key > thr) | (eq & (r_refs[slot][...] + run_eq < need))
        sel = sel & jnp.logical_not(inadmissible(j))
        negm = jnp.where(sel, 0.0, NEG)
        m8_new = []
        for h in range(A_HEADS):
            cols = slice(h * TILE, (h + 1) * TILE)
            mh = m8[:, cols]
            for half in range(KT // TILE):
                rs = slice(half * TILE, (half + 1) * TILE)
                blk = s_ref[j, rs, cols] + negm[rs, :]
                s_ref[j, rs, cols] = blk
                mh = jnp.maximum(mh, blk.reshape(TILE // 8, 8, TILE).max(axis=0))
            m8_new.append(mh)
        run_eq = run_eq + jnp.where(eq, 1.0, 0.0).sum(axis=0, keepdims=True)
        return run_eq, jnp.concatenate(m8_new, axis=1)

    _, m8 = pipelined(tie_ranks, mask_tile,
                      (jnp.zeros((1, TILE), F32), jnp.full((8, hq), NEG, F32)))
    m = m8.max(axis=0, keepdims=True)

    def weights(j, slot):
        p_refs[slot][...] = jnp.exp2(s_ref[j] - m).astype(BF16)

    def weighted(j, slot, carry):
        acc_ref[...] += jnp.dot(ct_ref[j], p_refs[slot][...], preferred_element_type=F32)
        return carry

    acc_ref[...] = jnp.zeros(acc_ref.shape, F32)
    pipelined(weights, weighted, 0)

    inv_l = 1.0 / acc_ref[A_LATENT:A_LATENT + 1, :]
    ot = (acc_ref[:A_LATENT, :] * inv_l).astype(BF16)
    for h in range(A_HEADS):
        o_ref[0, h * A_HEAD_DIM:(h + 1) * A_HEAD_DIM, :] = jnp.dot(
            wuvt_ref[h], ot[:, h * TILE:(h + 1) * TILE], preferred_element_type=F32
        ).astype(o_ref.dtype)


def dsa_attention(proj_h, proj_f, lat_g, w_uk, w_uv, bias, *, topk):
    b, s, _ = proj_h.shape
    ntiles = s // KT
    kern = functools.partial(_dsa_kernel, topk=topk, ntiles=ntiles)
    wuk_t = jnp.swapaxes(w_uk, 1, 2)
    zero = jnp.zeros_like(wuk_t)
    odd = (jnp.arange(A_HEADS) % 2 == 1)[:, None, None]
    wuk_pad = jnp.concatenate([jnp.where(odd, zero, wuk_t), jnp.where(odd, wuk_t, zero)], axis=1)
    col = lambda blk: (lambda bi, i: (bi, 0, blk))
    return pl.pallas_call(
        kern,
        out_shape=jax.ShapeDtypeStruct((b, A_HEADS * A_HEAD_DIM, s), BF16),
        grid=(b, s // TILE),
        in_specs=[
            pl.BlockSpec((1, TILE, 512), lambda bi, i: (bi, i, PROJ_H_QA)),
            pl.BlockSpec((1, TILE, 512), lambda bi, i: (bi, i, PROJ_H_QI)),
            pl.BlockSpec((1, TILE, TILE), lambda bi, i: (bi, i, PROJ_F_WI)),
            pl.BlockSpec((1, s, TILE), col(PROJ_F_C)),
            pl.BlockSpec((1, s, TILE), col(PROJ_H_KI_LO)),
            pl.BlockSpec((1, s, TILE), col(PROJ_H_KI_HI)),
            pl.BlockSpec((1, A_LATENT), lambda bi, i: (0, 0)),
            pl.BlockSpec((A_HEADS, TILE, A_LATENT), lambda bi, i: (0, 0, 0)),
            pl.BlockSpec((A_HEADS, A_HEAD_DIM, A_LATENT), lambda bi, i: (0, 0, 0)),
            pl.BlockSpec((3, TILE, A_HEADS * TILE), lambda bi, i: (0, 0, 0)),
        ],
        out_specs=pl.BlockSpec((1, A_HEADS * A_HEAD_DIM, TILE), lambda bi, i: (bi, 0, i)),
        scratch_shapes=[
            pltpu.VMEM((s, A_LATENT), BF16),
            pltpu.VMEM((ntiles, ACC_ROWS, KT), BF16),
            pltpu.VMEM((ntiles, KT, TILE), I32),
            pltpu.VMEM((ntiles, KT, TILE), jnp.int16),
            pltpu.VMEM((ntiles, KT, A_HEADS * TILE), F32),
            pltpu.VMEM((ACC_ROWS, A_HEADS * TILE), F32),
            pltpu.VMEM((KT, IDX_HEADS * TILE), F32),
            pltpu.VMEM((KT, IDX_HEADS * TILE), F32),
            pltpu.VMEM((KT, TILE), F32),
            pltpu.VMEM((KT, TILE), F32),
            pltpu.VMEM((KT, A_HEADS * TILE), BF16),
            pltpu.VMEM((KT, A_HEADS * TILE), BF16),
        ],
        compiler_params=_params(("parallel", "arbitrary")),
        name="dsa_attention",
    )(proj_h, proj_h, proj_f, proj_f, proj_h, proj_h, lat_g.reshape(1, A_LATENT),
      wuk_pad.astype(BF16), jnp.swapaxes(w_uv, 1, 2).astype(BF16), bias)


def _hgrn_kernel(f_ref, v_ref, q_ref, g_ref, lb_ref, og_ref, o_ref, st_ref, *, nchunk):
    @pl.when(pl.program_id(1) == 0)
    def _():
        st_ref[...] = jnp.zeros(st_ref.shape, F32)

    lb = lb_ref[...]
    og = og_ref[...]
    row = lax.broadcasted_iota(I32, (CHUNK, CHUNK), 0)
    col = lax.broadcasted_iota(I32, (CHUNK, CHUNK), 1)
    causal = col <= row
    tri = jnp.where(causal, 1.0, 0.0).astype(BF16)
    hc = CHUNK // 2
    top = lax.broadcasted_iota(I32, (CHUNK, B_HEADS * B_DK), 0) < hc
    zeros_h = jnp.zeros((hc, B_DK), BF16)

    for c in range(nchunk):
        rows = slice(c * CHUNK, (c + 1) * CHUNK)
        f = lb + (1.0 - lb) * jax.nn.sigmoid(f_ref[0, rows, :])
        g = jnp.log(f)
        k = 1.0 - f
        q = q_ref[0, rows, :]
        qf = q * jax.nn.sigmoid(q)
        vb = v_ref[0, rows, :]
        g_hi = g.astype(BF16)
        g_lo = (g - g_hi.astype(F32)).astype(BF16)
        bc = (jnp.dot(tri, g_hi, preferred_element_type=F32)
              + jnp.dot(tri, g_lo, preferred_element_type=F32))
        b_end = bc[CHUNK - 1:CHUNK, :]
        b_half = bc[hc - 1:hc, :]
        e_in = bc - jnp.where(top, bc[hc // 2 - 1:hc // 2, :], bc[hc + hc // 2 - 1:hc + hc // 2, :])
        q_in = (qf * jnp.exp(e_in)).astype(BF16)
        k_in = (k * jnp.exp(-e_in)).astype(BF16)
        q_x = (qf[hc:, :] * jnp.exp(bc[hc:, :] - b_half)).astype(BF16)
        k_x = (k[:hc, :] * jnp.exp(b_half - bc[:hc, :])).astype(BF16)
        qb = (qf * jnp.exp(bc)).astype(BF16)
        kd = (k * jnp.exp(b_end - bc)).astype(BF16)
        dec = jnp.exp(b_end)
        gate = g_ref[0, rows, :]
        gate = gate * jax.nn.sigmoid(gate)
        for h in range(B_HEADS):
            sl = slice(h * B_DK, (h + 1) * B_DK)
            q3 = jnp.concatenate([
                jnp.concatenate([q_in[:hc, sl], zeros_h, zeros_h], axis=1),
                jnp.concatenate([zeros_h, q_x[:, sl], q_in[hc:, sl]], axis=1)], axis=0)
            k3 = jnp.concatenate([
                jnp.concatenate([k_in[:hc, sl], k_x[:, sl], zeros_h], axis=1),
                jnp.concatenate([zeros_h, zeros_h, k_in[hc:, sl]], axis=1)], axis=0)
            sc = lax.dot_general(q3, k3, NT, preferred_element_type=F32)
            sc = jnp.where(causal, sc, 0.0).astype(BF16)
            st = st_ref[h]
            o = (jnp.dot(sc, vb[:, sl], preferred_element_type=F32)
                 + lax.dot_general(qb[:, sl], st.astype(BF16), NT, preferred_element_type=F32))
            st_ref[h] = st * dec[:, sl] + lax.dot_general(vb[:, sl], kd[:, sl], TN,
                                                          preferred_element_type=F32)
            o_ref[0, rows, sl] = (_rms(o, og) * gate[:, sl]).astype(o_ref.dtype)


def hgrn2(proj_h, proj_f, lb, o_g, *, rows):
    b, s, _ = proj_h.shape
    width = B_HEADS * B_DK
    kern = functools.partial(_hgrn_kernel, nchunk=rows // CHUNK)
    spec = lambda blk: pl.BlockSpec((1, rows, width), lambda bi, r: (bi, r, blk))
    return pl.pallas_call(
        kern,
        out_shape=jax.ShapeDtypeStruct((b, s, width), BF16),
        grid=(b, s // rows),
        in_specs=[spec(PROJ_F_F), spec(PROJ_H_V), spec(PROJ_F_Q), spec(PROJ_F_GATE),
                  pl.BlockSpec((1, width), lambda bi, r: (0, 0)),
                  pl.BlockSpec((1, B_DK), lambda bi, r: (0, 0))],
        out_specs=pl.BlockSpec((1, rows, width), lambda bi, r: (bi, r, 0)),
        scratch_shapes=[pltpu.VMEM((B_HEADS, B_DK, B_DK), F32)],
        compiler_params=_params(("parallel", "arbitrary")),
        name="hgrn2",
    )(proj_f, proj_h, proj_f, proj_f, lb.reshape(1, width), o_g.reshape(1, B_DK))


def _gmlp_kernel(h_ref, g_ref, win_ref, lng_ref, lnb_ref, wsp_ref, bsp_ref, wout_ref, o_ref,
                 gated_ref, *, nchunk):
    x = h_ref[0]
    width = x.shape[1]
    xn = _rms(x, g_ref[...]).astype(BF16)
    uv = jnp.dot(xn, win_ref[...], preferred_element_type=F32)
    uv = 0.5 * uv * (1.0 + lax.erf(uv * (0.5 ** 0.5)))
    u = uv[:, :width]
    v = uv[:, width:]
    mu = jnp.mean(v, axis=-1, keepdims=True)
    vc = v - mu
    var = jnp.mean(vc * vc, axis=-1, keepdims=True)
    vn = (vc * lax.rsqrt(var + EPS) * lng_ref[...] + lnb_ref[...]).astype(BF16)
    row = lax.broadcasted_iota(I32, (GM_CHUNK, GM_CHUNK), 0)
    col = lax.broadcasted_iota(I32, (GM_CHUNK, GM_CHUNK), 1)
    gw = width // GM_GROUPS
    for gi in range(GM_GROUPS):
        wg = jnp.where(col <= row, wsp_ref[gi], 0.0).astype(BF16)
        cs = slice(gi * gw, (gi + 1) * gw)
        for c in range(nchunk):
            rs = slice(c * GM_CHUNK, (c + 1) * GM_CHUNK)
            mixed = jnp.dot(wg, vn[rs, cs], preferred_element_type=F32) + bsp_ref[:, cs]
            gated_ref[rs, cs] = (u[rs, cs] * mixed).astype(BF16)
    o_ref[0] = x + jnp.dot(gated_ref[...], wout_ref[...], preferred_element_type=F32)


def gmlp(h, g, w_in, ln_g, ln_b, w_sp, b_sp, w_out, *, rows):
    b, s, d = h.shape
    width = w_out.shape[0]
    gw = width // GM_GROUPS
    bsp_full = jnp.repeat(jnp.transpose(b_sp), gw, axis=1)
    kern = functools.partial(_gmlp_kernel, nchunk=rows // GM_CHUNK)
    const = lambda shape: pl.BlockSpec(shape, lambda bi, r: (0,) * len(shape))
    return pl.pallas_call(
        kern,
        out_shape=jax.ShapeDtypeStruct((b, s, d), F32),
        grid=(b, s // rows),
        in_specs=[pl.BlockSpec((1, rows, d), lambda bi, r: (bi, r, 0)),
                  const((1, d)), const((d, 2 * width)), const((1, width)), const((1, width)),
                  const((GM_GROUPS, GM_CHUNK, GM_CHUNK)), const((GM_CHUNK, width)),
                  const((width, d))],
        out_specs=pl.BlockSpec((1, rows, d), lambda bi, r: (bi, r, 0)),
        scratch_shapes=[pltpu.VMEM((rows, width), BF16)],
        compiler_params=_params(("parallel", "parallel")),
        name="gmlp",
    )(h, g.reshape(1, d), w_in.astype(BF16), ln_g.reshape(1, width), ln_b.reshape(1, width),
      w_sp, bsp_full, w_out.astype(BF16))


def _xattn_kernel(*refs, mixed):
    if mixed:
        h_ref, at_ref, b_ref, wa_ref, wb_ref, g_ref, wq_ref, k_ref, v_ref, wo_ref, o_ref = refs
        x = (h_ref[0] + lax.dot_general(at_ref[0], wa_ref[...], TN, preferred_element_type=F32)
             + jnp.dot(b_ref[0], wb_ref[...], preferred_element_type=F32))
    else:
        h_ref, g_ref, wq_ref, k_ref, v_ref, wo_ref, o_ref = refs
        x = h_ref[0]
    xn = _rms(x, g_ref[...]).astype(BF16)
    q = jnp.dot(xn, wq_ref[...], preferred_element_type=F32).astype(BF16)
    outs = []
    for h in range(X_HEADS):
        sl = slice(h * X_HEAD_DIM, (h + 1) * X_HEAD_DIM)
        s = lax.dot_general(q[:, sl], k_ref[0, :, sl], NT, preferred_element_type=F32)
        s = s * (X_HEAD_DIM ** -0.5)
        p = jnp.exp(s - s.max(axis=-1, keepdims=True))
        l = p.sum(axis=-1, keepdims=True)
        o = jnp.dot(p.astype(BF16), v_ref[0, :, sl], preferred_element_type=F32)
        outs.append((o * (1.0 / l)).astype(BF16))
    att = jnp.concatenate(outs, axis=1)
    o_ref[0] = x + jnp.dot(att, wo_ref[...], preferred_element_type=F32)


def xattn(h, g, wq, kv, wo, *, rows, mix=None):
    b, s, d = h.shape
    m = kv.shape[1]
    xw = wq.shape[1]
    const = lambda shape: pl.BlockSpec(shape, lambda bi, r: (0,) * len(shape))
    rows_spec = lambda width: pl.BlockSpec((1, rows, width), lambda bi, r: (bi, r, 0))
    mix_specs, mix_args = [], []
    if mix is not None:
        a_t, b_out, wa, wb = mix
        mix_specs = [pl.BlockSpec((1, a_t.shape[1], rows), lambda bi, r: (bi, 0, r)),
                     rows_spec(b_out.shape[2]), const(wa.shape), const(wb.shape)]
        mix_args = [a_t, b_out, wa, wb]
    return pl.pallas_call(
        functools.partial(_xattn_kernel, mixed=mix is not None),
        out_shape=jax.ShapeDtypeStruct((b, s, d), F32),
        grid=(b, s // rows),
        in_specs=[rows_spec(d), *mix_specs,
                  const((1, d)), const((d, xw)),
                  pl.BlockSpec((1, m, xw), lambda bi, r: (bi, 0, 0)),
                  pl.BlockSpec((1, m, xw), lambda bi, r: (bi, 0, 1)),
                  const((xw, d))],
        out_specs=rows_spec(d),
        compiler_params=_params(("parallel", "parallel")),
        name="xattn",
    )(h, *mix_args, g.reshape(1, d), wq.astype(BF16), kv, kv, wo.astype(BF16))


def _swiglu_kernel(x_ref, g_ref, wgu_ref, wd_ref, fg_ref, o_ref, a_ref, *, ff, fchunk, final):
    x = x_ref[...]
    xn = _rms(x, g_ref[...]).astype(BF16)
    for c in range(ff // fchunk):
        gate = jnp.dot(xn, wgu_ref[:, c * fchunk:(c + 1) * fchunk], preferred_element_type=F32)
        up = jnp.dot(xn, wgu_ref[:, ff + c * fchunk:ff + (c + 1) * fchunk],
                     preferred_element_type=F32)
        a_ref[:, c * fchunk:(c + 1) * fchunk] = (gate * jax.nn.sigmoid(gate) * up).astype(BF16)
    y = x + jnp.dot(a_ref[...], wd_ref[...], preferred_element_type=F32)
    if final:
        y = _rms(y, fg_ref[...])
    o_ref[...] = y


def swiglu(x, g, w_gu, w_down, final_g, *, tm, final):
    n, d = x.shape
    ff = w_down.shape[0]
    kern = functools.partial(_swiglu_kernel, ff=ff, fchunk=256, final=final)
    once = dict(pipeline_mode=pl.Buffered(1))
    return pl.pallas_call(
        kern,
        out_shape=jax.ShapeDtypeStruct((n, d), F32),
        grid=(n // tm,),
        in_specs=[pl.BlockSpec((tm, d), lambda i: (i, 0)),
                  pl.BlockSpec((1, d), lambda i: (0, 0)),
                  pl.BlockSpec((d, 2 * ff), lambda i: (0, 0), **once),
                  pl.BlockSpec((ff, d), lambda i: (0, 0), **once),
                  pl.BlockSpec((1, d), lambda i: (0, 0))],
        out_specs=pl.BlockSpec((tm, d), lambda i: (i, 0)),
        scratch_shapes=[pltpu.VMEM((tm, ff), BF16)],
        compiler_params=_params(("parallel",)),
        name="swiglu",
    )(x, g.reshape(1, d), w_gu.astype(BF16), w_down.astype(BF16), final_g.reshape(1, d))


def _pack_in_proj(w):
    sizes = [A_HEADS * A_HEAD_DIM, A_LATENT, IDX_HEADS * IDX_DIM, IDX_DIM, IDX_HEADS,
             B_HEADS * B_DK, B_HEADS * B_DK, B_HEADS * B_DK, B_HEADS * B_DK]
    offs = [0]
    for sz in sizes:
        offs.append(offs[-1] + sz)
    q_a, c, qi, ki, wi, f_b, i_b, q_b, g_b = [w[:, offs[n]:offs[n + 1]] for n in range(9)]
    z = lambda n: jnp.zeros((w.shape[0], n), w.dtype)
    return jnp.concatenate([q_a, qi, i_b, ki, z(TILE - IDX_DIM), z(TILE - IDX_DIM), ki,
                            f_b, q_b, g_b, c, wi, z(TILE - IDX_HEADS)], axis=1)


def kernel(x, mem, rel_bias, hgrn_lb, mix_norm, e_w_in, e_lat_norm, e_w_uk, e_w_uv, e_o_norm, e_w_out, o_w_in, o_ln_g, o_ln_b, o_w_sp, o_b_sp, o_w_out, x_norm, mem_norm, x_wq, x_wkv, x_wo, f_norm, f_w_gu, f_w_down, final_norm):
    b, s, d = x.shape
    m = mem.shape[1]
    n = b * s
    depth = mix_norm.shape[0]
    topk = min(TOPK_MAX, s // 4)
    lb_all = jnp.cumsum(jax.nn.softmax(hgrn_lb.astype(F32), axis=0), axis=0)
    bias = bias_tiles(rel_bias)
    mem2 = mem.reshape(b * m, d)

    h = x.reshape(n, d)
    for l in range(depth):
        j = l // 2
        mix = None
        if l % 2 == 0:
            proj_h, proj_f = norm_matmul(h, mix_norm[l], _pack_in_proj(e_w_in[j]).astype(BF16),
                                         tm=512, tn=256,
                                         groups=((PROJ_H_COLS, BF16), (PROJ_F_COLS, F32)))
            proj_h = proj_h.reshape(b, s, -1)
            proj_f = proj_f.reshape(b, s, -1)
            a_out = dsa_attention(proj_h, proj_f, e_lat_norm[j], e_w_uk[j], e_w_uv[j], bias,
                                  topk=topk)
            b_out = hgrn2(proj_h, proj_f, lb_all[l], e_o_norm[j], rows=256)
            wa = e_w_out[j][:A_HEADS * A_HEAD_DIM].astype(BF16)
            wb = e_w_out[j][A_HEADS * A_HEAD_DIM:].astype(BF16)
            mix = (a_out, b_out, wa, wb)
        else:
            h = gmlp(h.reshape(b, s, d), mix_norm[l], o_w_in[j], o_ln_g[j], o_ln_b[j], o_w_sp[j],
                     o_b_sp[j], o_w_out[j], rows=512).reshape(n, d)
        (kv,) = norm_matmul(mem2, mem_norm[l], x_wkv[l].astype(BF16), tm=512, tn=512,
                            groups=((x_wkv.shape[2], BF16),))
        kv = kv.reshape(b, m, -1)
        h = xattn(h.reshape(b, s, d), x_norm[l], x_wq[l], kv, x_wo[l], rows=512,
                  mix=mix).reshape(n, d)
        h = swiglu(h, f_norm[l], f_w_gu[l], f_w_down[l], final_norm, tm=512,
                   final=(l == depth - 1))
    return h.reshape(b, s, d)
```

```python
import functools

import jax
import jax.numpy as jnp
from jax import lax
from jax.experimental import pallas as pl
from jax.experimental.pallas import tpu as pltpu

F32 = jnp.float32
BF16 = jnp.bfloat16
I32 = jnp.int32

EPS = 1e-6
CHUNK = 64
LANE = 128
KT = 256
QB = 256
A_HEADS = 8
A_HEAD_DIM = 64
A_LATENT = 128
IDX_HEADS = 8
IDX_DIM = 64
TOPK_MAX = 256
REL_BUCKETS = 32
B_HEADS = 4
B_DK = 128
GM_CHUNK = 128
GM_GROUPS = 8
X_HEADS = 4
X_HEAD_DIM = 128
NEG = -1e30
INT_MIN = -2**31
LOG2E = 1.4426950408889634
ACC_ROWS = A_LATENT + 16
PROJ_H_COLS = PROJ_F_COLS = 1792
PROJ_H_QA, PROJ_H_QI, PROJ_H_V = 0, 1, 2
PROJ_H_KI_LO, PROJ_H_KI_HI = 12, 13
PROJ_F_F, PROJ_F_Q, PROJ_F_GATE = 0, 1, 2
PROJ_F_C, PROJ_F_WI = 12, 13

VMEM_LIMIT = 56 * 1024 * 1024

NT = (((1,), (1,)), ((), ()))
TN = (((0,), (0,)), ((), ()))


def _rms(x, g):
    ms = jnp.mean(x * x, axis=-1, keepdims=True)
    return x * lax.rsqrt(ms + EPS) * g


def _params(sem, vmem=VMEM_LIMIT):
    return pltpu.CompilerParams(dimension_semantics=sem, vmem_limit_bytes=vmem)


def _norm_matmul_kernel(x_ref, g_ref, w_ref, *o_refs, tn):
    xn = _rms(x_ref[...], g_ref[...]).astype(BF16)
    base = 0
    for o_ref in o_refs:
        for c in range(o_ref.shape[1] // tn):
            y = jnp.dot(xn, w_ref[:, base + c * tn:base + (c + 1) * tn], preferred_element_type=F32)
            o_ref[:, c * tn:(c + 1) * tn] = y.astype(o_ref.dtype)
        base += o_ref.shape[1]


def norm_matmul(x, g, w, *, tm, tn, groups):
    n, k = x.shape
    nout = w.shape[1]
    assert sum(nc for nc, _ in groups) == nout and all(nc % tn == 0 for nc, _ in groups)
    return pl.pallas_call(
        functools.partial(_norm_matmul_kernel, tn=tn),
        out_shape=tuple(jax.ShapeDtypeStruct((n, nc), dt) for nc, dt in groups),
        grid=(n // tm,),
        in_specs=[
            pl.BlockSpec((tm, k), lambda i: (i, 0)),
            pl.BlockSpec((1, k), lambda i: (0, 0)),
            pl.BlockSpec((k, nout), lambda i: (0, 0), pipeline_mode=pl.Buffered(1)),
        ],
        out_specs=tuple(pl.BlockSpec((tm, nc), lambda i: (i, 0)) for nc, _ in groups),
        compiler_params=_params(("parallel",)),
        name="norm_matmul",
    )(x, g.reshape(1, k), w)


def _bias_tiles_kernel(rb_ref, o_ref):
    s = lax.broadcasted_iota(I32, (KT, QB), 0)
    t = lax.broadcasted_iota(I32, (KT, QB), 1)
    nb = REL_BUCKETS // 2
    max_exact = nb // 2
    for kind in range(3):
        rel = s - t - KT * kind
        n = jnp.abs(rel)
        n2 = n * n
        large = jnp.full((KT, QB), max_exact, I32)
        for j in range(1, nb - max_exact):
            large = large + jnp.where(n2 >= (max_exact * max_exact) * (2 ** j), 1, 0)
        bucket = jnp.where(rel > 0, nb, 0) + jnp.where(n < max_exact, n, large)
        for h in range(A_HEADS):
            acc = jnp.zeros((KT, QB), F32)
            for b in range(REL_BUCKETS):
                acc = jnp.where(bucket == b, rb_ref[b, h] * LOG2E, acc)
            o_ref[kind, :, h * QB:(h + 1) * QB] = acc


def bias_tiles(rel_bias):
    return pl.pallas_call(
        _bias_tiles_kernel,
        out_shape=jax.ShapeDtypeStruct((3, KT, A_HEADS * QB), F32),
        in_specs=[pl.BlockSpec(memory_space=pltpu.SMEM)],
        out_specs=pl.BlockSpec(memory_space=pltpu.VMEM),
        name="bias_tiles",
    )(rel_bias)


def _dsa_kernel(qa_ref, qi_ref, wi_ref, c_ref, ka_ref, kb_ref, latg_ref, wuk_ref, wuvt_ref, bias_ref,
                o_ref, cn_ref, ct_ref, keys_ref, hi_ref, s_ref, acc_ref,
                d0_ref, d1_ref, r0_ref, r1_ref, p0_ref, p1_ref, *, topk, ntiles):
    i = pl.program_id(1)
    hq = A_HEADS * QB
    nt = i + 1
    last = i

    @pl.when(i == 0)
    def _prep():
        cn = _rms(c_ref[0], latg_ref[...])
        cn_ref[...] = cn.astype(BF16)
        ones = jnp.ones((ACC_ROWS - A_LATENT, KT), BF16)
        for j in range(ntiles):
            ct_ref[j, :A_LATENT, :] = cn[j * KT:(j + 1) * KT, :].T.astype(BF16)
            ct_ref[j, A_LATENT:, :] = ones

    qa = qa_ref[0]
    ql = jnp.concatenate(
        [jnp.dot(qa[:, (h // 2) * LANE:(h // 2 + 1) * LANE], wuk_ref[h],
                 preferred_element_type=F32) for h in range(A_HEADS)], axis=0)
    qlb = (ql * (A_HEAD_DIM ** -0.5 * LOG2E)).astype(BF16)
    qi = qi_ref[0]
    qp = jnp.concatenate([qi[:, p * LANE:(p + 1) * LANE] for p in range(IDX_HEADS // 2)], axis=0)
    w = wi_ref[0] * (IDX_HEADS ** -0.5) * (IDX_DIM ** -0.5)
    wt = w.T

    row = lax.broadcasted_iota(I32, (KT, QB), 0)
    col = lax.broadcasted_iota(I32, (KT, QB), 1)
    chunk_bits = CHUNK.bit_length() - 1
    inadm_last = jnp.where(lax.shift_right_logical(row, chunk_bits)
                           > lax.shift_right_logical(col, chunk_bits), 1, 0)

    def inadmissible(j):
        return (inadm_last * jnp.where(j == last, 1, 0)) != 0

    def rows(j):
        return pl.ds(pl.multiple_of(j * KT, KT), KT)

    def pipelined(prepare, consume, carry):
        prepare(0, 0)

        def pair(t, carry):
            j = 2 * t
            prepare(j + 1, 1)
            carry = consume(j, 0, carry)
            prepare(jnp.minimum(j + 2, last), 0)
            return consume(j + 1, 1, carry)

        carry = lax.fori_loop(0, lax.shift_right_logical(nt, 1), pair, carry)
        return lax.cond((nt & 1) == 1, lambda c: consume(last, 0, c), lambda c: c, carry)

    half_w = (IDX_HEADS // 2) * QB
    d_refs = (d0_ref, d1_ref)

    def dots(j, slot):
        ks = rows(j)
        d_ref = d_refs[slot]
        d_ref[:, :half_w] = lax.dot_general(ka_ref[0, ks, :], qp, NT, preferred_element_type=F32)
        d_ref[:, half_w:] = lax.dot_general(kb_ref[0, ks, :], qp, NT, preferred_element_type=F32)
        s = lax.dot_general(cn_ref[ks, :], qlb, NT, preferred_element_type=F32)
        s_ref[j] = s + bias_ref[jnp.minimum(i - j, 2)]

    def score_keys(j, slot, carry):
        d_ref = d_refs[slot]
        sc = jnp.zeros((KT, QB), F32)
        for h in range(IDX_HEADS):
            c0 = (h % 2) * half_w + (h // 2) * QB
            sc = sc + jnp.maximum(d_ref[:, c0:c0 + QB], 0.0) * wt[h:h + 1, :]
        sc = jnp.where(inadmissible(j), -jnp.inf, sc)
        bits = pltpu.bitcast(sc, I32)
        key = jnp.where(bits < 0, bits ^ 0x7FFFFFFF, bits)
        key = jnp.where(bits == INT_MIN, 0, key)
        keys_ref[j] = key
        hi_ref[j] = lax.shift_right_arithmetic(key, 16).astype(jnp.int16)
        return carry

    pipelined(dots, score_keys, 0)

    half_min = -2 ** 15

    @pl.when(nt < ntiles)
    def _pad():
        hi_ref[nt] = jnp.full((KT, QB), half_min, jnp.int16)

    def select(cap):
        def count16(pred):
            acc = jnp.zeros((16, QB), jnp.int16)
            for j in range(cap):
                hit = jnp.where(pred(hi_ref[j]), jnp.int16(1), jnp.int16(0))
                parts = [hit[r * 16:(r + 1) * 16, :] for r in range(KT // 16)]
                while len(parts) > 1:
                    parts = [a + b for a, b in zip(parts[::2], parts[1::2])]
                acc = acc + parts[0]
            return acc.astype(I32).sum(axis=0, keepdims=True)

        def bisect16(target):
            def step_fn(step, lo):
                trial = lo + lax.shift_left(jnp.int32(1), 15 - step)
                t16 = trial.astype(jnp.int16)
                return jnp.where(count16(lambda k: k >= t16) >= target, trial, lo)
            return lax.fori_loop(0, 16, step_fn, jnp.full((1, QB), half_min, I32))

        thr_hi = bisect16(topk)

        def low_halves(j, acc):
            key = keys_ref[j]
            hi = lax.shift_right_arithmetic(key, 16)
            lo = (key & 0xFFFF) + half_min
            hi_ref[j] = jnp.where(hi == thr_hi, lo, half_min).astype(jnp.int16)
            above = jnp.where(hi > thr_hi, 1, 0)
            return acc + above.reshape(KT // 8, 8, QB).sum(axis=0)

        above_hi = lax.fori_loop(0, nt, low_halves, jnp.zeros((8, QB), I32))
        above_hi = above_hi.sum(axis=0, keepdims=True)
        thr_lo = bisect16(topk - above_hi)
        thr_lo16 = thr_lo.astype(jnp.int16)
        above = above_hi + count16(lambda k: k > thr_lo16)
        return lax.shift_left(thr_hi, 16) + (thr_lo - half_min), (topk - above).astype(F32)

    caps = sorted(set(min(c, ntiles) for c in range(2, ntiles + 2, 2)))
    thr, need = lax.switch(lax.shift_right_logical(nt - 1, 1),
                           [functools.partial(select, c) for c in caps])

    trow = lax.broadcasted_iota(I32, (KT, KT), 0)
    tcol = lax.broadcasted_iota(I32, (KT, KT), 1)
    tri = jnp.where(tcol < trow, 1.0, 0.0).astype(BF16)

    r_refs = (r0_ref, r1_ref)
    p_refs = (p0_ref, p1_ref)

    def tie_ranks(j, slot):
        eqf = jnp.where(keys_ref[j] == thr, 1.0, 0.0).astype(BF16)
        r_refs[slot][...] = jnp.dot(tri, eqf, preferred_element_type=F32)

    def mask_tile(j, slot, carry):
        run_eq, m8 = carry
        key = keys_ref[j]
        eq = key == thr
        sel = (key > thr) | (eq & (r_refs[slot][...] + run_eq < need))
        sel = sel & jnp.logical_not(inadmissible(j))
        negm = jnp.where(sel, 0.0, NEG)
        m8_new = []
        for h in range(A_HEADS):
            cols = slice(h * QB, (h + 1) * QB)
            blk = s_ref[j, :, cols] + negm
            s_ref[j, :, cols] = blk
            m8_new.append(jnp.maximum(m8[:, cols], blk.reshape(KT // 8, 8, QB).max(axis=0)))
        run_eq = run_eq + jnp.where(eq, 1.0, 0.0).sum(axis=0, keepdims=True)
        return run_eq, jnp.concatenate(m8_new, axis=1)

    _, m8 = pipelined(tie_ranks, mask_tile,
                      (jnp.zeros((1, QB), F32), jnp.full((8, hq), NEG, F32)))
    m = m8.max(axis=0, keepdims=True)

    def weights(j, slot):
        p_refs[slot][...] = jnp.exp2(s_ref[j] - m).astype(BF16)

    def weighted(j, slot, carry):
        acc_ref[...] += jnp.dot(ct_ref[j], p_refs[slot][...], preferred_element_type=F32)
        return carry

    acc_ref[...] = jnp.zeros(acc_ref.shape, F32)
    pipelined(weights, weighted, 0)

    inv_l = 1.0 / acc_ref[A_LATENT:A_LATENT + 1, :]
    ot = (acc_ref[:A_LATENT, :] * inv_l).astype(BF16)
    for h in range(A_HEADS):
        o_ref[0, h * A_HEAD_DIM:(h + 1) * A_HEAD_DIM, :] = jnp.dot(
            wuvt_ref[h], ot[:, h * QB:(h + 1) * QB], preferred_element_type=F32
        ).astype(o_ref.dtype)


def dsa_attention(proj_h, proj_f, lat_g, w_uk, w_uv, bias, *, topk):
    b, s, _ = proj_h.shape
    ntiles = s // KT
    kern = functools.partial(_dsa_kernel, topk=topk, ntiles=ntiles)
    wuk_t = jnp.swapaxes(w_uk, 1, 2)
    zero = jnp.zeros_like(wuk_t)
    odd = (jnp.arange(A_HEADS) % 2 == 1)[:, None, None]
    wuk_pad = jnp.concatenate([jnp.where(odd, zero, wuk_t), jnp.where(odd, wuk_t, zero)], axis=1)
    col = lambda blk: (lambda bi, i: (bi, 0, blk))
    const = lambda shape: pl.BlockSpec(shape, lambda bi, i: (0,) * len(shape))
    return pl.pallas_call(
        kern,
        out_shape=jax.ShapeDtypeStruct((b, A_HEADS * A_HEAD_DIM, s), BF16),
        grid=(b, s // QB),
        in_specs=[
            pl.BlockSpec((1, QB, 512), lambda bi, i: (bi, i, PROJ_H_QA)),
            pl.BlockSpec((1, QB, 512), lambda bi, i: (bi, i, PROJ_H_QI)),
            pl.BlockSpec((1, QB, LANE), lambda bi, i: (bi, i, PROJ_F_WI)),
            pl.BlockSpec((1, s, LANE), col(PROJ_F_C)),
            pl.BlockSpec((1, s, LANE), col(PROJ_H_KI_LO)),
            pl.BlockSpec((1, s, LANE), col(PROJ_H_KI_HI)),
            const((1, A_LATENT)),
            const((A_HEADS, LANE, A_LATENT)),
            const((A_HEADS, A_HEAD_DIM, A_LATENT)),
            pl.BlockSpec((3, KT, A_HEADS * QB), lambda bi, i: (0, 0, 0),
                         pipeline_mode=pl.Buffered(1)),
        ],
        out_specs=pl.BlockSpec((1, A_HEADS * A_HEAD_DIM, QB), lambda bi, i: (bi, 0, i)),
        scratch_shapes=[
            pltpu.VMEM((s, A_LATENT), BF16),
            pltpu.VMEM((ntiles, ACC_ROWS, KT), BF16),
            pltpu.VMEM((ntiles, KT, QB), I32),
            pltpu.VMEM((ntiles, KT, QB), jnp.int16),
            pltpu.VMEM((ntiles, KT, A_HEADS * QB), F32),
            pltpu.VMEM((ACC_ROWS, A_HEADS * QB), F32),
            pltpu.VMEM((KT, IDX_HEADS * QB), F32),
            pltpu.VMEM((KT, IDX_HEADS * QB), F32),
            pltpu.VMEM((KT, QB), F32),
            pltpu.VMEM((KT, QB), F32),
            pltpu.VMEM((KT, A_HEADS * QB), BF16),
            pltpu.VMEM((KT, A_HEADS * QB), BF16),
        ],
        compiler_params=_params(("parallel", "arbitrary")),
        name="dsa_attention",
    )(proj_h, proj_h, proj_f, proj_f, proj_h, proj_h, lat_g.reshape(1, A_LATENT),
      wuk_pad.astype(BF16), jnp.swapaxes(w_uv, 1, 2).astype(BF16), bias)


def _hgrn_kernel(f_ref, v_ref, q_ref, g_ref, lb_ref, og_ref, o_ref, st_ref, *, nchunk):
    @pl.when(pl.program_id(1) == 0)
    def _():
        st_ref[...] = jnp.zeros(st_ref.shape, F32)

    lb = lb_ref[...]
    og = og_ref[...]
    row = lax.broadcasted_iota(I32, (CHUNK, CHUNK), 0)
    col = lax.broadcasted_iota(I32, (CHUNK, CHUNK), 1)
    causal = col <= row
    tri = jnp.where(causal, 1.0, 0.0).astype(BF16)
    hc = CHUNK // 2
    top = lax.broadcasted_iota(I32, (CHUNK, B_HEADS * B_DK), 0) < hc
    zeros_h = jnp.zeros((hc, B_DK), BF16)

    for c in range(nchunk):
        rows = slice(c * CHUNK, (c + 1) * CHUNK)
        f = lb + (1.0 - lb) * jax.nn.sigmoid(f_ref[0, rows, :])
        g = jnp.log(f)
        k = 1.0 - f
        q = q_ref[0, rows, :]
        qf = q * jax.nn.sigmoid(q)
        vb = v_ref[0, rows, :]
        g_hi = g.astype(BF16)
        g_lo = (g - g_hi.astype(F32)).astype(BF16)
        bc = (jnp.dot(tri, g_hi, preferred_element_type=F32)
              + jnp.dot(tri, g_lo, preferred_element_type=F32))
        b_end = bc[CHUNK - 1:CHUNK, :]
        b_half = bc[hc - 1:hc, :]
        e_in = bc - jnp.where(top, bc[hc // 2 - 1:hc // 2, :], bc[hc + hc // 2 - 1:hc + hc // 2, :])
        q_in = (qf * jnp.exp(e_in)).astype(BF16)
        k_in = (k * jnp.exp(-e_in)).astype(BF16)
        q_x = (qf[hc:, :] * jnp.exp(bc[hc:, :] - b_half)).astype(BF16)
        k_x = (k[:hc, :] * jnp.exp(b_half - bc[:hc, :])).astype(BF16)
        qb = (qf * jnp.exp(bc)).astype(BF16)
        kd = (k * jnp.exp(b_end - bc)).astype(BF16)
        dec = jnp.exp(b_end)
        gate = g_ref[0, rows, :]
        gate = gate * jax.nn.sigmoid(gate)
        for h in range(B_HEADS):
            sl = slice(h * B_DK, (h + 1) * B_DK)
            q3 = jnp.concatenate([
                jnp.concatenate([q_in[:hc, sl], zeros_h, zeros_h], axis=1),
                jnp.concatenate([zeros_h, q_x[:, sl], q_in[hc:, sl]], axis=1)], axis=0)
            k3 = jnp.concatenate([
                jnp.concatenate([k_in[:hc, sl], k_x[:, sl], zeros_h], axis=1),
                jnp.concatenate([zeros_h, zeros_h, k_in[hc:, sl]], axis=1)], axis=0)
            sc = lax.dot_general(q3, k3, NT, preferred_element_type=F32)
            sc = jnp.where(causal, sc, 0.0).astype(BF16)
            st = st_ref[h]
            o = (jnp.dot(sc, vb[:, sl], preferred_element_type=F32)
                 + lax.dot_general(qb[:, sl], st.astype(BF16), NT, preferred_element_type=F32))
            st_ref[h] = st * dec[:, sl] + lax.dot_general(vb[:, sl], kd[:, sl], TN,
                                                          preferred_element_type=F32)
            o_ref[0, rows, sl] = (_rms(o, og) * gate[:, sl]).astype(o_ref.dtype)


def hgrn2(proj_h, proj_f, lb, o_g, *, rows):
    b, s, _ = proj_h.shape
    width = B_HEADS * B_DK
    kern = functools.partial(_hgrn_kernel, nchunk=rows // CHUNK)
    spec = lambda blk: pl.BlockSpec((1, rows, width), lambda bi, r: (bi, r, blk))
    return pl.pallas_call(
        kern,
        out_shape=jax.ShapeDtypeStruct((b, s, width), BF16),
        grid=(b, s // rows),
        in_specs=[spec(PROJ_F_F), spec(PROJ_H_V), spec(PROJ_F_Q), spec(PROJ_F_GATE),
                  pl.BlockSpec((1, width), lambda bi, r: (0, 0)),
                  pl.BlockSpec((1, B_DK), lambda bi, r: (0, 0))],
        out_specs=pl.BlockSpec((1, rows, width), lambda bi, r: (bi, r, 0)),
        scratch_shapes=[pltpu.VMEM((B_HEADS, B_DK, B_DK), F32)],
        compiler_params=_params(("parallel", "arbitrary")),
        name="hgrn2",
    )(proj_f, proj_h, proj_f, proj_f, lb.reshape(1, width), o_g.reshape(1, B_DK))


def _gmlp_kernel(h_ref, g_ref, win_ref, lng_ref, lnb_ref, wsp_ref, bsp_ref, wout_ref, o_ref,
                 gated_ref, *, nchunk):
    x = h_ref[0]
    width = x.shape[1]
    xn = _rms(x, g_ref[...]).astype(BF16)
    uv = jnp.dot(xn, win_ref[...], preferred_element_type=F32)
    uv = 0.5 * uv * (1.0 + lax.erf(uv * (0.5 ** 0.5)))
    u = uv[:, :width]
    v = uv[:, width:]
    mu = jnp.mean(v, axis=-1, keepdims=True)
    vc = v - mu
    var = jnp.mean(vc * vc, axis=-1, keepdims=True)
    vn = (vc * lax.rsqrt(var + EPS) * lng_ref[...] + lnb_ref[...]).astype(BF16)
    row = lax.broadcasted_iota(I32, (GM_CHUNK, GM_CHUNK), 0)
    col = lax.broadcasted_iota(I32, (GM_CHUNK, GM_CHUNK), 1)
    gw = width // GM_GROUPS
    for gi in range(GM_GROUPS):
        wg = jnp.where(col <= row, wsp_ref[gi], 0.0).astype(BF16)
        cs = slice(gi * gw, (gi + 1) * gw)
        for c in range(nchunk):
            rs = slice(c * GM_CHUNK, (c + 1) * GM_CHUNK)
            mixed = jnp.dot(wg, vn[rs, cs], preferred_element_type=F32) + bsp_ref[:, cs]
            gated_ref[rs, cs] = (u[rs, cs] * mixed).astype(BF16)
    o_ref[0] = x + jnp.dot(gated_ref[...], wout_ref[...], preferred_element_type=F32)


def gmlp(h, g, w_in, ln_g, ln_b, w_sp, b_sp, w_out, *, rows):
    b, s, d = h.shape
    width = w_out.shape[0]
    gw = width // GM_GROUPS
    bsp_full = jnp.repeat(jnp.transpose(b_sp), gw, axis=1)
    kern = functools.partial(_gmlp_kernel, nchunk=rows // GM_CHUNK)
    const = lambda shape: pl.BlockSpec(shape, lambda bi, r: (0,) * len(shape))
    return pl.pallas_call(
        kern,
        out_shape=jax.ShapeDtypeStruct((b, s, d), F32),
        grid=(b, s // rows),
        in_specs=[pl.BlockSpec((1, rows, d), lambda bi, r: (bi, r, 0)),
                  const((1, d)), const((d, 2 * width)), const((1, width)), const((1, width)),
                  const((GM_GROUPS, GM_CHUNK, GM_CHUNK)), const((GM_CHUNK, width)),
                  const((width, d))],
        out_specs=pl.BlockSpec((1, rows, d), lambda bi, r: (bi, r, 0)),
        scratch_shapes=[pltpu.VMEM((rows, width), BF16)],
        compiler_params=_params(("parallel", "parallel")),
        name="gmlp",
    )(h, g.reshape(1, d), w_in.astype(BF16), ln_g.reshape(1, width), ln_b.reshape(1, width),
      w_sp, bsp_full, w_out.astype(BF16))


def _xattn_kernel(*refs, mixed):
    if mixed:
        h_ref, at_ref, b_ref, wa_ref, wb_ref, g_ref, wq_ref, k_ref, v_ref, wo_ref, o_ref = refs
        x = (h_ref[0] + lax.dot_general(at_ref[0], wa_ref[...], TN, preferred_element_type=F32)
             + jnp.dot(b_ref[0], wb_ref[...], preferred_element_type=F32))
    else:
        h_ref, g_ref, wq_ref, k_ref, v_ref, wo_ref, o_ref = refs
        x = h_ref[0]
    xn = _rms(x, g_ref[...]).astype(BF16)
    q = jnp.dot(xn, wq_ref[...], preferred_element_type=F32).astype(BF16)
    outs = []
    for h in range(X_HEADS):
        sl = slice(h * X_HEAD_DIM, (h + 1) * X_HEAD_DIM)
        s = lax.dot_general(q[:, sl], k_ref[0, :, sl], NT, preferred_element_type=F32)
        s = s * (X_HEAD_DIM ** -0.5)
        p = jnp.exp(s - s.max(axis=-1, keepdims=True))
        l = p.sum(axis=-1, keepdims=True)
        o = jnp.dot(p.astype(BF16), v_ref[0, :, sl], preferred_element_type=F32)
        outs.append((o * (1.0 / l)).astype(BF16))
    att = jnp.concatenate(outs, axis=1)
    o_ref[0] = x + jnp.dot(att, wo_ref[...], preferred_element_type=F32)


def xattn(h, g, wq, kv, wo, *, rows, mix=None):
    b, s, d = h.shape
    m = kv.shape[1]
    xw = wq.shape[1]
    const = lambda shape: pl.BlockSpec(shape, lambda bi, r: (0,) * len(shape))
    rows_spec = lambda width: pl.BlockSpec((1, rows, width), lambda bi, r: (bi, r, 0))
    mix_specs, mix_args = [], []
    if mix is not None:
        a_t, b_out, wa, wb = mix
        mix_specs = [pl.BlockSpec((1, a_t.shape[1], rows), lambda bi, r: (bi, 0, r)),
                     rows_spec(b_out.shape[2]), const(wa.shape), const(wb.shape)]
        mix_args = [a_t, b_out, wa, wb]
    return pl.pallas_call(
        functools.partial(_xattn_kernel, mixed=mix is not None),
        out_shape=jax.ShapeDtypeStruct((b, s, d), F32),
        grid=(b, s // rows),
        in_specs=[rows_spec(d), *mix_specs,
                  const((1, d)), const((d, xw)),
                  pl.BlockSpec((1, m, xw), lambda bi, r: (bi, 0, 0)),
                  pl.BlockSpec((1, m, xw), lambda bi, r: (bi, 0, 1)),
                  const((xw, d))],
        out_specs=rows_spec(d),
        compiler_params=_params(("parallel", "parallel")),
        name="xattn",
    )(h, *mix_args, g.reshape(1, d), wq.astype(BF16), kv, kv, wo.astype(BF16))


def _swiglu_kernel(x_ref, g_ref, wgu_ref, wd_ref, fg_ref, o_ref, a_ref, *, ff, fchunk, final):
    x = x_ref[...]
    xn = _rms(x, g_ref[...]).astype(BF16)
    for c in range(ff // fchunk):
        gate = jnp.dot(xn, wgu_ref[:, c * fchunk:(c + 1) * fchunk], preferred_element_type=F32)
        up = jnp.dot(xn, wgu_ref[:, ff + c * fchunk:ff + (c + 1) * fchunk],
                     preferred_element_type=F32)
        a_ref[:, c * fchunk:(c + 1) * fchunk] = (gate * jax.nn.sigmoid(gate) * up).astype(BF16)
    y = x + jnp.dot(a_ref[...], wd_ref[...], preferred_element_type=F32)
    if final:
        y = _rms(y, fg_ref[...])
    o_ref[...] = y


def swiglu(x, g, w_gu, w_down, final_g, *, tm, final):
    n, d = x.shape
    ff = w_down.shape[0]
    kern = functools.partial(_swiglu_kernel, ff=ff, fchunk=256, final=final)
    once = dict(pipeline_mode=pl.Buffered(1))
    return pl.pallas_call(
        kern,
        out_shape=jax.ShapeDtypeStruct((n, d), F32),
        grid=(n // tm,),
        in_specs=[pl.BlockSpec((tm, d), lambda i: (i, 0)),
                  pl.BlockSpec((1, d), lambda i: (0, 0)),
                  pl.BlockSpec((d, 2 * ff), lambda i: (0, 0), **once),
                  pl.BlockSpec((ff, d), lambda i: (0, 0), **once),
                  pl.BlockSpec((1, d), lambda i: (0, 0))],
        out_specs=pl.BlockSpec((tm, d), lambda i: (i, 0)),
        scratch_shapes=[pltpu.VMEM((tm, ff), BF16)],
        compiler_params=_params(("parallel",)),
        name="swiglu",
    )(x, g.reshape(1, d), w_gu.astype(BF16), w_down.astype(BF16), final_g.reshape(1, d))


def _pack_in_proj(w):
    sizes = [A_HEADS * A_HEAD_DIM, A_LATENT, IDX_HEADS * IDX_DIM, IDX_DIM, IDX_HEADS,
             B_HEADS * B_DK, B_HEADS * B_DK, B_HEADS * B_DK, B_HEADS * B_DK]
    offs = [0]
    for sz in sizes:
        offs.append(offs[-1] + sz)
    q_a, c, qi, ki, wi, f_b, i_b, q_b, g_b = [w[:, offs[n]:offs[n + 1]] for n in range(9)]
    z = lambda n: jnp.zeros((w.shape[0], n), w.dtype)
    return jnp.concatenate([q_a, qi, i_b, ki, z(LANE - IDX_DIM), z(LANE - IDX_DIM), ki,
                            f_b, q_b, g_b, c, wi, z(LANE - IDX_HEADS)], axis=1)


def kernel(x, mem, rel_bias, hgrn_lb, mix_norm, e_w_in, e_lat_norm, e_w_uk, e_w_uv, e_o_norm, e_w_out, o_w_in, o_ln_g, o_ln_b, o_w_sp, o_b_sp, o_w_out, x_norm, mem_norm, x_wq, x_wkv, x_wo, f_norm, f_w_gu, f_w_down, final_norm):
    b, s, d = x.shape
    m = mem.shape[1]
    n = b * s
    depth = mix_norm.shape[0]
    topk = min(TOPK_MAX, s // 4)
    lb_all = jnp.cumsum(jax.nn.softmax(hgrn_lb.astype(F32), axis=0), axis=0)
    bias = bias_tiles(rel_bias)
    mem2 = mem.reshape(b * m, d)

    h = x.reshape(n, d)
    for l in range(depth):
        j = l // 2
        mix = None
        if l % 2 == 0:
            proj_h, proj_f = norm_matmul(h, mix_norm[l], _pack_in_proj(e_w_in[j]).astype(BF16),
                                         tm=512, tn=256,
                                         groups=((PROJ_H_COLS, BF16), (PROJ_F_COLS, F32)))
            proj_h = proj_h.reshape(b, s, -1)
            proj_f = proj_f.reshape(b, s, -1)
            a_out = dsa_attention(proj_h, proj_f, e_lat_norm[j], e_w_uk[j], e_w_uv[j], bias,
                                  topk=topk)
            b_out = hgrn2(proj_h, proj_f, lb_all[l], e_o_norm[j], rows=256)
            wa = e_w_out[j][:A_HEADS * A_HEAD_DIM].astype(BF16)
            wb = e_w_out[j][A_HEADS * A_HEAD_DIM:].astype(BF16)
            mix = (a_out, b_out, wa, wb)
        else:
            h = gmlp(h.reshape(b, s, d), mix_norm[l], o_w_in[j], o_ln_g[j], o_ln_b[j], o_w_sp[j],
                     o_b_sp[j], o_w_out[j], rows=512).reshape(n, d)
        (kv,) = norm_matmul(mem2, mem_norm[l], x_wkv[l].astype(BF16), tm=512, tn=512,
                            groups=((x_wkv.shape[2], BF16),))
        kv = kv.reshape(b, m, -1)
        h = xattn(h.reshape(b, s, d), x_norm[l], x_wq[l], kv, x_wo[l], rows=512,
                  mix=mix).reshape(n, d)
        h = swiglu(h, f_norm[l], f_w_gu[l], f_w_down[l], final_norm, tm=512,
                   final=(l == depth - 1))
    return h.reshape(b, s, d)
```

```python
import functools

import jax
import jax.numpy as jnp
from jax import lax
from jax.experimental import pallas as pl
from jax.experimental.pallas import tpu as pltpu

F32 = jnp.float32
BF16 = jnp.bfloat16
I32 = jnp.int32

EPS = 1e-6
CHUNK = 64
LANE = 128
KT = 256
QB = 256
A_HEADS = 8
A_HEAD_DIM = 64
A_LATENT = 128
IDX_HEADS = 8
IDX_DIM = 64
TOPK_MAX = 256
REL_BUCKETS = 32
B_HEADS = 4
B_DK = 128
GM_CHUNK = 128
GM_GROUPS = 8
X_HEADS = 4
X_HEAD_DIM = 128
NEG = -1e30
INT_MIN = -2**31
LOG2E = 1.4426950408889634
ACC_ROWS = A_LATENT + 16
PROJ_H_COLS = PROJ_F_COLS = 1792
PROJ_H_QA, PROJ_H_QI, PROJ_H_V = 0, 1, 2
PROJ_H_KI_LO, PROJ_H_KI_HI = 12, 13
PROJ_F_F, PROJ_F_Q, PROJ_F_GATE = 0, 1, 2
PROJ_F_C, PROJ_F_WI = 12, 13

VMEM_LIMIT = 56 * 1024 * 1024

NT = (((1,), (1,)), ((), ()))
TN = (((0,), (0,)), ((), ()))


def _rms(x, g):
    ms = jnp.mean(x * x, axis=-1, keepdims=True)
    return x * lax.rsqrt(ms + EPS) * g


def _params(sem, vmem=VMEM_LIMIT):
    return pltpu.CompilerParams(dimension_semantics=sem, vmem_limit_bytes=vmem)


def _norm_matmul_kernel(x_ref, g_ref, w_ref, *o_refs, tn):
    xn = _rms(x_ref[...], g_ref[...]).astype(BF16)
    base = 0
    for o_ref in o_refs:
        for c in range(o_ref.shape[1] // tn):
            y = jnp.dot(xn, w_ref[:, base + c * tn:base + (c + 1) * tn], preferred_element_type=F32)
            o_ref[:, c * tn:(c + 1) * tn] = y.astype(o_ref.dtype)
        base += o_ref.shape[1]


def norm_matmul(x, g, w, *, tm, tn, groups):
    n, k = x.shape
    nout = w.shape[1]
    assert sum(nc for nc, _ in groups) == nout and all(nc % tn == 0 for nc, _ in groups)
    return pl.pallas_call(
        functools.partial(_norm_matmul_kernel, tn=tn),
        out_shape=tuple(jax.ShapeDtypeStruct((n, nc), dt) for nc, dt in groups),
        grid=(n // tm,),
        in_specs=[
            pl.BlockSpec((tm, k), lambda i: (i, 0)),
            pl.BlockSpec((1, k), lambda i: (0, 0)),
            pl.BlockSpec((k, nout), lambda i: (0, 0), pipeline_mode=pl.Buffered(1)),
        ],
        out_specs=tuple(pl.BlockSpec((tm, nc), lambda i: (i, 0)) for nc, _ in groups),
        compiler_params=_params(("parallel",)),
        name="norm_matmul",
    )(x, g.reshape(1, k), w)


def _bias_tiles_kernel(rb_ref, o_ref):
    s = lax.broadcasted_iota(I32, (KT, QB), 0)
    t = lax.broadcasted_iota(I32, (KT, QB), 1)
    nb = REL_BUCKETS // 2
    max_exact = nb // 2
    for kind in range(3):
        rel = s - t - KT * kind
        n = jnp.abs(rel)
        n2 = n * n
        large = jnp.full((KT, QB), max_exact, I32)
        for j in range(1, nb - max_exact):
            large = large + jnp.where(n2 >= (max_exact * max_exact) * (2 ** j), 1, 0)
        bucket = jnp.where(rel > 0, nb, 0) + jnp.where(n < max_exact, n, large)
        for h in range(A_HEADS):
            acc = jnp.zeros((KT, QB), F32)
            for b in range(REL_BUCKETS):
                acc = jnp.where(bucket == b, rb_ref[b, h] * LOG2E, acc)
            o_ref[kind, :, h * QB:(h + 1) * QB] = acc


def bias_tiles(rel_bias):
    return pl.pallas_call(
        _bias_tiles_kernel,
        out_shape=jax.ShapeDtypeStruct((3, KT, A_HEADS * QB), F32),
        in_specs=[pl.BlockSpec(memory_space=pltpu.SMEM)],
        out_specs=pl.BlockSpec(memory_space=pltpu.VMEM),
        name="bias_tiles",
    )(rel_bias)


def _dsa_kernel(qa_ref, qi_ref, wi_ref, c_ref, ka_ref, kb_ref, latg_ref, wuk_ref, wuvt_ref, bias_ref,
                o_ref, cn_ref, ct_ref, keys_ref, hi_ref, s_ref, acc_ref,
                d0_ref, d1_ref, r0_ref, r1_ref, p0_ref, p1_ref, *, topk, ntiles):
    i = pl.program_id(1)
    hq = A_HEADS * QB
    nt = i + 1
    last = i

    @pl.when(i == 0)
    def _prep():
        cn = _rms(c_ref[0], latg_ref[...])
        cn_ref[...] = cn.astype(BF16)
        ones = jnp.ones((ACC_ROWS - A_LATENT, KT), BF16)
        for j in range(ntiles):
            ct_ref[j, :A_LATENT, :] = cn[j * KT:(j + 1) * KT, :].T.astype(BF16)
            ct_ref[j, A_LATENT:, :] = ones

    qa = qa_ref[0]
    ql = jnp.concatenate(
        [jnp.dot(qa[:, (h // 2) * LANE:(h // 2 + 1) * LANE], wuk_ref[h],
                 preferred_element_type=F32) for h in range(A_HEADS)], axis=0)
    qlb = (ql * (A_HEAD_DIM ** -0.5 * LOG2E)).astype(BF16)
    qi = qi_ref[0]
    qp = jnp.concatenate([qi[:, p * LANE:(p + 1) * LANE] for p in range(IDX_HEADS // 2)], axis=0)
    w = wi_ref[0] * (IDX_HEADS ** -0.5) * (IDX_DIM ** -0.5)
    wt = w.T

    row = lax.broadcasted_iota(I32, (KT, QB), 0)
    col = lax.broadcasted_iota(I32, (KT, QB), 1)
    chunk_bits = CHUNK.bit_length() - 1
    inadm_last = jnp.where(lax.shift_right_logical(row, chunk_bits)
                           > lax.shift_right_logical(col, chunk_bits), 1, 0)

    def inadmissible(j):
        return (inadm_last * jnp.where(j == last, 1, 0)) != 0

    def rows(j):
        return pl.ds(pl.multiple_of(j * KT, KT), KT)

    def pipelined(prepare, consume, carry):
        prepare(0, 0)

        def pair(t, carry):
            j = 2 * t
            prepare(j + 1, 1)
            carry = consume(j, 0, carry)
            prepare(jnp.minimum(j + 2, last), 0)
            return consume(j + 1, 1, carry)

        carry = lax.fori_loop(0, lax.shift_right_logical(nt, 1), pair, carry)
        return lax.cond((nt & 1) == 1, lambda c: consume(last, 0, c), lambda c: c, carry)

    half_w = (IDX_HEADS // 2) * QB
    d_refs = (d0_ref, d1_ref)

    def dots(j, slot):
        ks = rows(j)
        d_ref = d_refs[slot]
        d_ref[:, :half_w] = lax.dot_general(ka_ref[0, ks, :], qp, NT, preferred_element_type=F32)
        d_ref[:, half_w:] = lax.dot_general(kb_ref[0, ks, :], qp, NT, preferred_element_type=F32)
        s = lax.dot_general(cn_ref[ks, :], qlb, NT, preferred_element_type=F32)
        s_ref[j] = s + bias_ref[jnp.minimum(i - j, 2)]

    def score_keys(j, slot, carry):
        d_ref = d_refs[slot]
        sc = jnp.zeros((KT, QB), F32)
        for h in range(IDX_HEADS):
            c0 = (h % 2) * half_w + (h // 2) * QB
            sc = sc + jnp.maximum(d_ref[:, c0:c0 + QB], 0.0) * wt[h:h + 1, :]
        sc = jnp.where(inadmissible(j), -jnp.inf, sc)
        bits = pltpu.bitcast(sc, I32)
        key = jnp.where(bits < 0, bits ^ 0x7FFFFFFF, bits)
        key = jnp.where(bits == INT_MIN, 0, key)
        keys_ref[j] = key
        hi_ref[j] = lax.shift_right_arithmetic(key, 16).astype(jnp.int16)
        return carry

    pipelined(dots, score_keys, 0)

    half_min = -2 ** 15

    @pl.when(nt < ntiles)
    def _pad():
        hi_ref[nt] = jnp.full((KT, QB), half_min, jnp.int16)

    def select(cap):
        def count16(pred):
            acc = jnp.zeros((16, QB), jnp.int16)
            for j in range(cap):
                hit = jnp.where(pred(hi_ref[j]), jnp.int16(1), jnp.int16(0))
                parts = [hit[r * 16:(r + 1) * 16, :] for r in range(KT // 16)]
                while len(parts) > 1:
                    parts = [a + b for a, b in zip(parts[::2], parts[1::2])]
                acc = acc + parts[0]
            return acc.astype(I32).sum(axis=0, keepdims=True)

        def bisect16(target):
            def step_fn(step, lo):
                trial = lo + lax.shift_left(jnp.int32(1), 15 - step)
                t16 = trial.astype(jnp.int16)
                return jnp.where(count16(lambda k: k >= t16) >= target, trial, lo)
            return lax.fori_loop(0, 16, step_fn, jnp.full((1, QB), half_min, I32))

        thr_hi = bisect16(topk)

        def low_halves(j, acc):
            key = keys_ref[j]
            hi = lax.shift_right_arithmetic(key, 16)
            lo = (key & 0xFFFF) + half_min
            hi_ref[j] = jnp.where(hi == thr_hi, lo, half_min).astype(jnp.int16)
            above = jnp.where(hi > thr_hi, 1, 0)
            return acc + above.reshape(KT // 8, 8, QB).sum(axis=0)

        above_hi = lax.fori_loop(0, nt, low_halves, jnp.zeros((8, QB), I32))
        above_hi = above_hi.sum(axis=0, keepdims=True)
        thr_lo = bisect16(topk - above_hi)
        thr_lo16 = thr_lo.astype(jnp.int16)
        above = above_hi + count16(lambda k: k > thr_lo16)
        return lax.shift_left(thr_hi, 16) + (thr_lo - half_min), (topk - above).astype(F32)

    caps = sorted(set(min(c, ntiles) for c in range(2, ntiles + 2, 2)))
    thr, need = lax.switch(lax.shift_right_logical(nt - 1, 1),
                           [functools.partial(select, c) for c in caps])

    trow = lax.broadcasted_iota(I32, (KT, KT), 0)
    tcol = lax.broadcasted_iota(I32, (KT, KT), 1)
    tri = jnp.where(tcol < trow, 1.0, 0.0).astype(BF16)

    r_refs = (r0_ref, r1_ref)
    p_refs = (p0_ref, p1_ref)

    def tie_ranks(j, slot):
        eqf = jnp.where(keys_ref[j] == thr, 1.0, 0.0).astype(BF16)
        r_refs[slot][...] = jnp.dot(tri, eqf, preferred_element_type=F32)

    def mask_tile(j, slot, carry):
        run_eq, m8 = carry
        key = keys_ref[j]
        eq = key == thr
        sel = (key > thr) | (eq & (r_refs[slot][...] + run_eq < need))
        sel = sel & jnp.logical_not(inadmissible(j))
        negm = jnp.where(sel, 0.0, NEG)
        m8_new = []
        for h in range(A_HEADS):
            cols = slice(h * QB, (h + 1) * QB)
            blk = s_ref[j, :, cols] + negm
            s_ref[j, :, cols] = blk
            m8_new.append(jnp.maximum(m8[:, cols], blk.reshape(KT // 8, 8, QB).max(axis=0)))
        run_eq = run_eq + jnp.where(eq, 1.0, 0.0).sum(axis=0, keepdims=True)
        return run_eq, jnp.concatenate(m8_new, axis=1)

    _, m8 = pipelined(tie_ranks, mask_tile,
                      (jnp.zeros((1, QB), F32), jnp.full((8, hq), NEG, F32)))
    m = m8.max(axis=0, keepdims=True)

    def weights(j, slot):
        p_refs[slot][...] = jnp.exp2(s_ref[j] - m).astype(BF16)

    def weighted(j, slot, carry):
        acc_ref[...] += jnp.dot(ct_ref[j], p_refs[slot][...], preferred_element_type=F32)
        return carry

    acc_ref[...] = jnp.zeros(acc_ref.shape, F32)
    pipelined(weights, weighted, 0)

    inv_l = 1.0 / acc_ref[A_LATENT:A_LATENT + 1, :]
    ot = (acc_ref[:A_LATENT, :] * inv_l).astype(BF16)
    for h in range(A_HEADS):
        o_ref[0, h * A_HEAD_DIM:(h + 1) * A_HEAD_DIM, :] = jnp.dot(
            wuvt_ref[h], ot[:, h * QB:(h + 1) * QB], preferred_element_type=F32
        ).astype(o_ref.dtype)


def dsa_attention(proj_h, proj_f, lat_g, w_uk, w_uv, bias, *, topk):
    b, s, _ = proj_h.shape
    ntiles = s // KT
    kern = functools.partial(_dsa_kernel, topk=topk, ntiles=ntiles)
    wuk_t = jnp.swapaxes(w_uk, 1, 2)
    zero = jnp.zeros_like(wuk_t)
    odd = (jnp.arange(A_HEADS) % 2 == 1)[:, None, None]
    wuk_pad = jnp.concatenate([jnp.where(odd, zero, wuk_t), jnp.where(odd, wuk_t, zero)], axis=1)
    col = lambda blk: (lambda bi, i: (bi, 0, blk))
    const = lambda shape: pl.BlockSpec(shape, lambda bi, i: (0,) * len(shape))
    return pl.pallas_call(
        kern,
        out_shape=jax.ShapeDtypeStruct((b, A_HEADS * A_HEAD_DIM, s), BF16),
        grid=(b, s // QB),
        in_specs=[
            pl.BlockSpec((1, QB, 512), lambda bi, i: (bi, i, PROJ_H_QA)),
            pl.BlockSpec((1, QB, 512), lambda bi, i: (bi, i, PROJ_H_QI)),
            pl.BlockSpec((1, QB, LANE), lambda bi, i: (bi, i, PROJ_F_WI)),
            pl.BlockSpec((1, s, LANE), col(PROJ_F_C)),
            pl.BlockSpec((1, s, LANE), col(PROJ_H_KI_LO)),
            pl.BlockSpec((1, s, LANE), col(PROJ_H_KI_HI)),
            const((1, A_LATENT)),
            const((A_HEADS, LANE, A_LATENT)),
            const((A_HEADS, A_HEAD_DIM, A_LATENT)),
            pl.BlockSpec((3, KT, A_HEADS * QB), lambda bi, i: (0, 0, 0),
                         pipeline_mode=pl.Buffered(1)),
        ],
        out_specs=pl.BlockSpec((1, A_HEADS * A_HEAD_DIM, QB), lambda bi, i: (bi, 0, i)),
        scratch_shapes=[
            pltpu.VMEM((s, A_LATENT), BF16),
            pltpu.VMEM((ntiles, ACC_ROWS, KT), BF16),
            pltpu.VMEM((ntiles, KT, QB), I32),
            pltpu.VMEM((ntiles, KT, QB), jnp.int16),
            pltpu.VMEM((ntiles, KT, A_HEADS * QB), F32),
            pltpu.VMEM((ACC_ROWS, A_HEADS * QB), F32),
            pltpu.VMEM((KT, IDX_HEADS * QB), F32),
            pltpu.VMEM((KT, IDX_HEADS * QB), F32),
            pltpu.VMEM((KT, QB), F32),
            pltpu.VMEM((KT, QB), F32),
            pltpu.VMEM((KT, A_HEADS * QB), BF16),
            pltpu.VMEM((KT, A_HEADS * QB), BF16),
        ],
        compiler_params=_params(("parallel", "arbitrary")),
        name="dsa_attention",
    )(proj_h, proj_h, proj_f, proj_f, proj_h, proj_h, lat_g.reshape(1, A_LATENT),
      wuk_pad.astype(BF16), jnp.swapaxes(w_uv, 1, 2).astype(BF16), bias)


def _hgrn_kernel(f_ref, v_ref, q_ref, g_ref, lb_ref, og_ref, o_ref, st_ref, *, nchunk):
    @pl.when(pl.program_id(1) == 0)
    def _():
        st_ref[...] = jnp.zeros(st_ref.shape, F32)

    lb = lb_ref[...]
    og = og_ref[...]
    row = lax.broadcasted_iota(I32, (CHUNK, CHUNK), 0)
    col = lax.broadcasted_iota(I32, (CHUNK, CHUNK), 1)
    causal = col <= row
    tri = jnp.where(causal, 1.0, 0.0).astype(BF16)
    hc = CHUNK // 2
    top = lax.broadcasted_iota(I32, (CHUNK, B_HEADS * B_DK), 0) < hc
    zeros_h = jnp.zeros((hc, B_DK), BF16)

    for c in range(nchunk):
        rows = slice(c * CHUNK, (c + 1) * CHUNK)
        f = lb + (1.0 - lb) * jax.nn.sigmoid(f_ref[0, rows, :])
        g = jnp.log(f)
        k = 1.0 - f
        q = q_ref[0, rows, :]
        qf = q * jax.nn.sigmoid(q)
        vb = v_ref[0, rows, :]
        g_hi = g.astype(BF16)
        g_lo = (g - g_hi.astype(F32)).astype(BF16)
        bc = (jnp.dot(tri, g_hi, preferred_element_type=F32)
              + jnp.dot(tri, g_lo, preferred_element_type=F32))
        b_end = bc[CHUNK - 1:CHUNK, :]
        b_half = bc[hc - 1:hc, :]
        e_in = bc - jnp.where(top, bc[hc // 2 - 1:hc // 2, :], bc[hc + hc // 2 - 1:hc + hc // 2, :])
        q_in = (qf * jnp.exp(e_in)).astype(BF16)
        k_in = (k * jnp.exp(-e_in)).astype(BF16)
        q_x = (qf[hc:, :] * jnp.exp(bc[hc:, :] - b_half)).astype(BF16)
        k_x = (k[:hc, :] * jnp.exp(b_half - bc[:hc, :])).astype(BF16)
        qb = (qf * jnp.exp(bc)).astype(BF16)
        kd = (k * jnp.exp(b_end - bc)).astype(BF16)
        dec = jnp.exp(b_end)
        gate = g_ref[0, rows, :]
        gate = gate * jax.nn.sigmoid(gate)
        for h in range(B_HEADS):
            sl = slice(h * B_DK, (h + 1) * B_DK)
            q3 = jnp.concatenate([
                jnp.concatenate([q_in[:hc, sl], zeros_h, zeros_h], axis=1),
                jnp.concatenate([zeros_h, q_x[:, sl], q_in[hc:, sl]], axis=1)], axis=0)
            k3 = jnp.concatenate([
                jnp.concatenate([k_in[:hc, sl], k_x[:, sl], zeros_h], axis=1),
                jnp.concatenate([zeros_h, zeros_h, k_in[hc:, sl]], axis=1)], axis=0)
            sc = lax.dot_general(q3, k3, NT, preferred_element_type=F32)
            sc = jnp.where(causal, sc, 0.0).astype(BF16)
            st = st_ref[h]
            o = (jnp.dot(sc, vb[:, sl], preferred_element_type=F32)
                 + lax.dot_general(qb[:, sl], st.astype(BF16), NT, preferred_element_type=F32))
            st_ref[h] = st * dec[:, sl] + lax.dot_general(vb[:, sl], kd[:, sl], TN,
                                                          preferred_element_type=F32)
            o_ref[0, rows, sl] = (_rms(o, og) * gate[:, sl]).astype(o_ref.dtype)


def hgrn2(proj_h, proj_f, lb, o_g, *, rows):
    b, s, _ = proj_h.shape
    width = B_HEADS * B_DK
    kern = functools.partial(_hgrn_kernel, nchunk=rows // CHUNK)
    spec = lambda blk: pl.BlockSpec((1, rows, width), lambda bi, r: (bi, r, blk))
    return pl.pallas_call(
        kern,
        out_shape=jax.ShapeDtypeStruct((b, s, width), BF16),
        grid=(b, s // rows),
        in_specs=[spec(PROJ_F_F), spec(PROJ_H_V), spec(PROJ_F_Q), spec(PROJ_F_GATE),
                  pl.BlockSpec((1, width), lambda bi, r: (0, 0)),
                  pl.BlockSpec((1, B_DK), lambda bi, r: (0, 0))],
        out_specs=pl.BlockSpec((1, rows, width), lambda bi, r: (bi, r, 0)),
        scratch_shapes=[pltpu.VMEM((B_HEADS, B_DK, B_DK), F32)],
        compiler_params=_params(("parallel", "arbitrary")),
        name="hgrn2",
    )(proj_f, proj_h, proj_f, proj_f, lb.reshape(1, width), o_g.reshape(1, B_DK))


def _gmlp_kernel(h_ref, g_ref, win_ref, lng_ref, lnb_ref, wsp_ref, bsp_ref, wout_ref, o_ref,
                 gated_ref, *, nchunk):
    x = h_ref[0]
    width = x.shape[1]
    xn = _rms(x, g_ref[...]).astype(BF16)
    uv = jnp.dot(xn, win_ref[...], preferred_element_type=F32)
    uv = 0.5 * uv * (1.0 + lax.erf(uv * (0.5 ** 0.5)))
    u = uv[:, :width]
    v = uv[:, width:]
    mu = jnp.mean(v, axis=-1, keepdims=True)
    vc = v - mu
    var = jnp.mean(vc * vc, axis=-1, keepdims=True)
    vn = (vc * lax.rsqrt(var + EPS) * lng_ref[...] + lnb_ref[...]).astype(BF16)
    row = lax.broadcasted_iota(I32, (GM_CHUNK, GM_CHUNK), 0)
    col = lax.broadcasted_iota(I32, (GM_CHUNK, GM_CHUNK), 1)
    gw = width // GM_GROUPS
    for gi in range(GM_GROUPS):
        wg = jnp.where(col <= row, wsp_ref[gi], 0.0).astype(BF16)
        cs = slice(gi * gw, (gi + 1) * gw)
        for c in range(nchunk):
            rs = slice(c * GM_CHUNK, (c + 1) * GM_CHUNK)
            mixed = jnp.dot(wg, vn[rs, cs], preferred_element_type=F32) + bsp_ref[:, cs]
            gated_ref[rs, cs] = (u[rs, cs] * mixed).astype(BF16)
    o_ref[0] = x + jnp.dot(gated_ref[...], wout_ref[...], preferred_element_type=F32)


def gmlp(h, g, w_in, ln_g, ln_b, w_sp, b_sp, w_out, *, rows):
    b, s, d = h.shape
    width = w_out.shape[0]
    gw = width // GM_GROUPS
    bsp_full = jnp.repeat(jnp.transpose(b_sp), gw, axis=1)
    kern = functools.partial(_gmlp_kernel, nchunk=rows // GM_CHUNK)
    const = lambda shape: pl.BlockSpec(shape, lambda bi, r: (0,) * len(shape),
                                       pipeline_mode=pl.Buffered(1))
    return pl.pallas_call(
        kern,
        out_shape=jax.ShapeDtypeStruct((b, s, d), F32),
        grid=(b, s // rows),
        in_specs=[pl.BlockSpec((1, rows, d), lambda bi, r: (bi, r, 0)),
                  const((1, d)), const((d, 2 * width)), const((1, width)), const((1, width)),
                  const((GM_GROUPS, GM_CHUNK, GM_CHUNK)), const((GM_CHUNK, width)),
                  const((width, d))],
        out_specs=pl.BlockSpec((1, rows, d), lambda bi, r: (bi, r, 0)),
        scratch_shapes=[pltpu.VMEM((rows, width), BF16)],
        compiler_params=_params(("parallel", "parallel")),
        name="gmlp",
    )(h, g.reshape(1, d), w_in.astype(BF16), ln_g.reshape(1, width), ln_b.reshape(1, width),
      w_sp, bsp_full, w_out.astype(BF16))


def _xattn_kernel(*refs, mixed):
    if mixed:
        h_ref, at_ref, b_ref, wa_ref, wb_ref, g_ref, wq_ref, k_ref, v_ref, wo_ref, o_ref = refs
        x = (h_ref[0] + lax.dot_general(at_ref[0], wa_ref[...], TN, preferred_element_type=F32)
             + jnp.dot(b_ref[0], wb_ref[...], preferred_element_type=F32))
    else:
        h_ref, g_ref, wq_ref, k_ref, v_ref, wo_ref, o_ref = refs
        x = h_ref[0]
    xn = _rms(x, g_ref[...]).astype(BF16)
    q = jnp.dot(xn, wq_ref[...], preferred_element_type=F32).astype(BF16)
    outs = []
    for h in range(X_HEADS):
        sl = slice(h * X_HEAD_DIM, (h + 1) * X_HEAD_DIM)
        s = lax.dot_general(q[:, sl], k_ref[0, :, sl], NT, preferred_element_type=F32)
        s = s * (X_HEAD_DIM ** -0.5)
        p = jnp.exp(s - s.max(axis=-1, keepdims=True))
        l = p.sum(axis=-1, keepdims=True)
        o = jnp.dot(p.astype(BF16), v_ref[0, :, sl], preferred_element_type=F32)
        outs.append((o * (1.0 / l)).astype(BF16))
    att = jnp.concatenate(outs, axis=1)
    o_ref[0] = x + jnp.dot(att, wo_ref[...], preferred_element_type=F32)


def xattn(h, g, wq, kv, wo, *, rows, mix=None):
    b, s, d = h.shape
    m = kv.shape[1]
    xw = wq.shape[1]
    const = lambda shape: pl.BlockSpec(shape, lambda bi, r: (0,) * len(shape))
    rows_spec = lambda width: pl.BlockSpec((1, rows, width), lambda bi, r: (bi, r, 0))
    mix_specs, mix_args = [], []
    if mix is not None:
        a_t, b_out, wa, wb = mix
        mix_specs = [pl.BlockSpec((1, a_t.shape[1], rows), lambda bi, r: (bi, 0, r)),
                     rows_spec(b_out.shape[2]), const(wa.shape), const(wb.shape)]
        mix_args = [a_t, b_out, wa, wb]
    return pl.pallas_call(
        functools.partial(_xattn_kernel, mixed=mix is not None),
        out_shape=jax.ShapeDtypeStruct((b, s, d), F32),
        grid=(b, s // rows),
        in_specs=[rows_spec(d), *mix_specs,
                  const((1, d)), const((d, xw)),
                  pl.BlockSpec((1, m, xw), lambda bi, r: (bi, 0, 0)),
                  pl.BlockSpec((1, m, xw), lambda bi, r: (bi, 0, 1)),
                  const((xw, d))],
        out_specs=rows_spec(d),
        compiler_params=_params(("parallel", "parallel")),
        name="xattn",
    )(h, *mix_args, g.reshape(1, d), wq.astype(BF16), kv, kv, wo.astype(BF16))


def _swiglu_kernel(x_ref, g_ref, wgu_ref, wd_ref, fg_ref, o_ref, a_ref, *, ff, fchunk, final):
    x = x_ref[...]
    xn = _rms(x, g_ref[...]).astype(BF16)
    for c in range(ff // fchunk):
        gate = jnp.dot(xn, wgu_ref[:, c * fchunk:(c + 1) * fchunk], preferred_element_type=F32)
        up = jnp.dot(xn, wgu_ref[:, ff + c * fchunk:ff + (c + 1) * fchunk],
                     preferred_element_type=F32)
        a_ref[:, c * fchunk:(c + 1) * fchunk] = (gate * jax.nn.sigmoid(gate) * up).astype(BF16)
    y = x + jnp.dot(a_ref[...], wd_ref[...], preferred_element_type=F32)
    if final:
        y = _rms(y, fg_ref[...])
    o_ref[...] = y


def swiglu(x, g, w_gu, w_down, final_g, *, tm, final):
    n, d = x.shape
    ff = w_down.shape[0]
    kern = functools.partial(_swiglu_kernel, ff=ff, fchunk=256, final=final)
    once = dict(pipeline_mode=pl.Buffered(1))
    return pl.pallas_call(
        kern,
        out_shape=jax.ShapeDtypeStruct((n, d), F32),
        grid=(n // tm,),
        in_specs=[pl.BlockSpec((tm, d), lambda i: (i, 0)),
                  pl.BlockSpec((1, d), lambda i: (0, 0)),
                  pl.BlockSpec((d, 2 * ff), lambda i: (0, 0), **once),
                  pl.BlockSpec((ff, d), lambda i: (0, 0), **once),
                  pl.BlockSpec((1, d), lambda i: (0, 0))],
        out_specs=pl.BlockSpec((tm, d), lambda i: (i, 0)),
        scratch_shapes=[pltpu.VMEM((tm, ff), BF16)],
        compiler_params=_params(("parallel",)),
        name="swiglu",
    )(x, g.reshape(1, d), w_gu.astype(BF16), w_down.astype(BF16), final_g.reshape(1, d))


def _pack_in_proj(w):
    sizes = [A_HEADS * A_HEAD_DIM, A_LATENT, IDX_HEADS * IDX_DIM, IDX_DIM, IDX_HEADS,
             B_HEADS * B_DK, B_HEADS * B_DK, B_HEADS * B_DK, B_HEADS * B_DK]
    offs = [0]
    for sz in sizes:
        offs.append(offs[-1] + sz)
    q_a, c, qi, ki, wi, f_b, i_b, q_b, g_b = [w[:, offs[n]:offs[n + 1]] for n in range(9)]
    z = lambda n: jnp.zeros((w.shape[0], n), w.dtype)
    return jnp.concatenate([q_a, qi, i_b, ki, z(LANE - IDX_DIM), z(LANE - IDX_DIM), ki,
                            f_b, q_b, g_b, c, wi, z(LANE - IDX_HEADS)], axis=1)


def kernel(x, mem, rel_bias, hgrn_lb, mix_norm, e_w_in, e_lat_norm, e_w_uk, e_w_uv, e_o_norm, e_w_out, o_w_in, o_ln_g, o_ln_b, o_w_sp, o_b_sp, o_w_out, x_norm, mem_norm, x_wq, x_wkv, x_wo, f_norm, f_w_gu, f_w_down, final_norm):
    b, s, d = x.shape
    m = mem.shape[1]
    n = b * s
    depth = mix_norm.shape[0]
    topk = min(TOPK_MAX, s // 4)
    lb_all = jnp.cumsum(jax.nn.softmax(hgrn_lb.astype(F32), axis=0), axis=0)
    bias = bias_tiles(rel_bias)
    mem2 = mem.reshape(b * m, d)

    h = x.reshape(n, d)
    for l in range(depth):
        j = l // 2
        mix = None
        if l % 2 == 0:
            proj_h, proj_f = norm_matmul(h, mix_norm[l], _pack_in_proj(e_w_in[j]).astype(BF16),
                                         tm=1024, tn=256,
                                         groups=((PROJ_H_COLS, BF16), (PROJ_F_COLS, F32)))
            proj_h = proj_h.reshape(b, s, -1)
            proj_f = proj_f.reshape(b, s, -1)
            a_out = dsa_attention(proj_h, proj_f, e_lat_norm[j], e_w_uk[j], e_w_uv[j], bias,
                                  topk=topk)
            b_out = hgrn2(proj_h, proj_f, lb_all[l], e_o_norm[j], rows=512)
            wa = e_w_out[j][:A_HEADS * A_HEAD_DIM].astype(BF16)
            wb = e_w_out[j][A_HEADS * A_HEAD_DIM:].astype(BF16)
            mix = (a_out, b_out, wa, wb)
        else:
            h = gmlp(h.reshape(b, s, d), mix_norm[l], o_w_in[j], o_ln_g[j], o_ln_b[j], o_w_sp[j],
                     o_b_sp[j], o_w_out[j], rows=1024).reshape(n, d)
        (kv,) = norm_matmul(mem2, mem_norm[l], x_wkv[l].astype(BF16), tm=512, tn=512,
                            groups=((x_wkv.shape[2], BF16),))
        kv = kv.reshape(b, m, -1)
        h = xattn(h.reshape(b, s, d), x_norm[l], x_wq[l], kv, x_wo[l], rows=1024,
                  mix=mix).reshape(n, d)
        h = swiglu(h, f_norm[l], f_w_gu[l], f_w_down[l], final_norm, tm=1024,
                   final=(l == depth - 1))
    return h.reshape(b, s, d)
```

```python
import functools

import jax
import jax.numpy as jnp
from jax import lax
from jax.experimental import pallas as pl
from jax.experimental.pallas import tpu as pltpu

F32 = jnp.float32
BF16 = jnp.bfloat16
I32 = jnp.int32

EPS = 1e-6
CHUNK = 64
LANE = 128
KT = 256
QB = 256
A_HEADS = 8
A_HEAD_DIM = 64
A_LATENT = 128
IDX_HEADS = 8
IDX_DIM = 64
TOPK_MAX = 256
REL_BUCKETS = 32
B_HEADS = 4
B_DK = 128
GM_CHUNK = 128
GM_GROUPS = 8
X_HEADS = 4
X_HEAD_DIM = 128
NEG = -1e30
INT_MIN = -2**31
LOG2E = 1.4426950408889634
ACC_ROWS = A_LATENT + 16
WEIGHT_SUM_FLOOR = 2.0 ** -60
PROJ_H_COLS = PROJ_F_COLS = 1792
PROJ_H_QA, PROJ_H_QI, PROJ_H_V = 0, 1, 2
PROJ_H_KI_LO, PROJ_H_KI_HI = 12, 13
PROJ_F_F, PROJ_F_Q, PROJ_F_GATE = 0, 1, 2
PROJ_F_C, PROJ_F_WI = 12, 13

VMEM_LIMIT = 56 * 1024 * 1024

NT = (((1,), (1,)), ((), ()))
TN = (((0,), (0,)), ((), ()))


def _rms(x, g):
    ms = jnp.mean(x * x, axis=-1, keepdims=True)
    return x * lax.rsqrt(ms + EPS) * g


def _params(sem, vmem=VMEM_LIMIT):
    return pltpu.CompilerParams(dimension_semantics=sem, vmem_limit_bytes=vmem)


def _norm_matmul_kernel(x_ref, g_ref, w_ref, *o_refs, tn):
    xn = _rms(x_ref[...], g_ref[...]).astype(BF16)
    base = 0
    for o_ref in o_refs:
        for c in range(o_ref.shape[1] // tn):
            y = jnp.dot(xn, w_ref[:, base + c * tn:base + (c + 1) * tn], preferred_element_type=F32)
            o_ref[:, c * tn:(c + 1) * tn] = y.astype(o_ref.dtype)
        base += o_ref.shape[1]


def norm_matmul(x, g, w, *, tm, tn, groups):
    n, k = x.shape
    nout = w.shape[1]
    assert sum(nc for nc, _ in groups) == nout and all(nc % tn == 0 for nc, _ in groups)
    return pl.pallas_call(
        functools.partial(_norm_matmul_kernel, tn=tn),
        out_shape=tuple(jax.ShapeDtypeStruct((n, nc), dt) for nc, dt in groups),
        grid=(n // tm,),
        in_specs=[
            pl.BlockSpec((tm, k), lambda i: (i, 0)),
            pl.BlockSpec((1, k), lambda i: (0, 0)),
            pl.BlockSpec((k, nout), lambda i: (0, 0), pipeline_mode=pl.Buffered(1)),
        ],
        out_specs=tuple(pl.BlockSpec((tm, nc), lambda i: (i, 0)) for nc, _ in groups),
        compiler_params=_params(("parallel",)),
        name="norm_matmul",
    )(x, g.reshape(1, k), w)


def _bias_tiles_kernel(rb_ref, o_ref):
    s = lax.broadcasted_iota(I32, (KT, QB), 0)
    t = lax.broadcasted_iota(I32, (KT, QB), 1)
    nb = REL_BUCKETS // 2
    max_exact = nb // 2
    for kind in range(3):
        rel = s - t - KT * kind
        n = jnp.abs(rel)
        n2 = n * n
        large = jnp.full((KT, QB), max_exact, I32)
        for j in range(1, nb - max_exact):
            large = large + jnp.where(n2 >= (max_exact * max_exact) * (2 ** j), 1, 0)
        bucket = jnp.where(rel > 0, nb, 0) + jnp.where(n < max_exact, n, large)
        for h in range(A_HEADS):
            acc = jnp.zeros((KT, QB), F32)
            for b in range(REL_BUCKETS):
                acc = jnp.where(bucket == b, rb_ref[b, h] * LOG2E, acc)
            o_ref[kind, :, h * QB:(h + 1) * QB] = acc


def bias_tiles(rel_bias):
    return pl.pallas_call(
        _bias_tiles_kernel,
        out_shape=jax.ShapeDtypeStruct((3, KT, A_HEADS * QB), F32),
        in_specs=[pl.BlockSpec(memory_space=pltpu.SMEM)],
        out_specs=pl.BlockSpec(memory_space=pltpu.VMEM),
        name="bias_tiles",
    )(rel_bias)


def _dsa_kernel(qa_ref, qi_ref, wi_ref, c_ref, ka_ref, kb_ref, latg_ref, wuk_ref, wuvt_ref, bias_ref,
                o_ref, cn_ref, ct_ref, keys_ref, hi_ref, s_ref, acc_ref,
                d0_ref, d1_ref, r0_ref, r1_ref, p0_ref, p1_ref, mx_ref, re_ref, *, topk, ntiles):
    i = pl.program_id(1)
    hq = A_HEADS * QB
    nt = i + 1
    last = i

    @pl.when(i == 0)
    def _prep():
        cn = _rms(c_ref[0], latg_ref[...])
        cn_ref[...] = cn.astype(BF16)
        ones = jnp.ones((ACC_ROWS - A_LATENT, KT), BF16)
        for j in range(ntiles):
            ct_ref[j, :A_LATENT, :] = cn[j * KT:(j + 1) * KT, :].T.astype(BF16)
            ct_ref[j, A_LATENT:, :] = ones

    qa = qa_ref[0]
    ql = jnp.concatenate(
        [jnp.dot(qa[:, (h // 2) * LANE:(h // 2 + 1) * LANE], wuk_ref[h],
                 preferred_element_type=F32) for h in range(A_HEADS)], axis=0)
    qlb = (ql * (A_HEAD_DIM ** -0.5 * LOG2E)).astype(BF16)
    qi = qi_ref[0]
    qp = jnp.concatenate([qi[:, p * LANE:(p + 1) * LANE] for p in range(IDX_HEADS // 2)], axis=0)
    w = wi_ref[0] * (IDX_HEADS ** -0.5) * (IDX_DIM ** -0.5)
    wt = w.T

    row = lax.broadcasted_iota(I32, (KT, QB), 0)
    col = lax.broadcasted_iota(I32, (KT, QB), 1)
    chunk_bits = CHUNK.bit_length() - 1
    inadm_last = jnp.where(lax.shift_right_logical(row, chunk_bits)
                           > lax.shift_right_logical(col, chunk_bits), 1, 0)

    def inadmissible(j):
        return (inadm_last * jnp.where(j == last, 1, 0)) != 0

    def rows(j):
        return pl.ds(pl.multiple_of(j * KT, KT), KT)

    def pipelined(prepare, consume, carry):
        prepare(0, 0)

        def pair(t, carry):
            j = 2 * t
            prepare(j + 1, 1)
            carry = consume(j, 0, carry)
            prepare(jnp.minimum(j + 2, last), 0)
            return consume(j + 1, 1, carry)

        carry = lax.fori_loop(0, lax.shift_right_logical(nt, 1), pair, carry)
        return lax.cond((nt & 1) == 1, lambda c: consume(last, 0, c), lambda c: c, carry)

    half_w = (IDX_HEADS // 2) * QB
    d_refs = (d0_ref, d1_ref)

    def dots(j, slot):
        ks = rows(j)
        d_ref = d_refs[slot]
        d_ref[:, :half_w] = lax.dot_general(ka_ref[0, ks, :], qp, NT, preferred_element_type=F32)
        d_ref[:, half_w:] = lax.dot_general(kb_ref[0, ks, :], qp, NT, preferred_element_type=F32)
        s = lax.dot_general(cn_ref[ks, :], qlb, NT, preferred_element_type=F32)
        s = s + bias_ref[jnp.minimum(i - j, 2)]
        s_ref[j] = s
        mx_ref[j] = s.reshape(KT // 8, 8, hq).max(axis=0)

    def score_keys(j, slot, carry):
        d_ref = d_refs[slot]
        sc = jnp.zeros((KT, QB), F32)
        for h in range(IDX_HEADS):
            c0 = (h % 2) * half_w + (h // 2) * QB
            sc = sc + jnp.maximum(d_ref[:, c0:c0 + QB], 0.0) * wt[h:h + 1, :]
        sc = jnp.where(inadmissible(j), -jnp.inf, sc)
        bits = pltpu.bitcast(sc, I32)
        key = jnp.where(bits < 0, bits ^ 0x7FFFFFFF, bits)
        key = jnp.where(bits == INT_MIN, 0, key)
        keys_ref[j] = key
        hi_ref[j] = lax.shift_right_arithmetic(key, 16).astype(jnp.int16)
        return carry

    pipelined(dots, score_keys, 0)

    half_min = -2 ** 15

    @pl.when(nt < ntiles)
    def _pad():
        hi_ref[nt] = jnp.full((KT, QB), half_min, jnp.int16)

    def select(cap):
        def count16(pred):
            acc = jnp.zeros((16, QB), jnp.int16)
            for j in range(cap):
                hit = jnp.where(pred(hi_ref[j]), jnp.int16(1), jnp.int16(0))
                parts = [hit[r * 16:(r + 1) * 16, :] for r in range(KT // 16)]
                while len(parts) > 1:
                    parts = [a + b for a, b in zip(parts[::2], parts[1::2])]
                acc = acc + parts[0]
            return acc.astype(I32).sum(axis=0, keepdims=True)

        def bisect16(target):
            def step_fn(step, lo):
                trial = lo + lax.shift_left(jnp.int32(1), 15 - step)
                t16 = trial.astype(jnp.int16)
                return jnp.where(count16(lambda k: k >= t16) >= target, trial, lo)
            return lax.fori_loop(0, 16, step_fn, jnp.full((1, QB), half_min, I32))

        thr_hi = bisect16(topk)

        def low_halves(j, acc):
            key = keys_ref[j]
            hi = lax.shift_right_arithmetic(key, 16)
            lo = (key & 0xFFFF) + half_min
            hi_ref[j] = jnp.where(hi == thr_hi, lo, half_min).astype(jnp.int16)
            above = jnp.where(hi > thr_hi, 1, 0)
            return acc + above.reshape(KT // 8, 8, QB).sum(axis=0)

        above_hi = lax.fori_loop(0, nt, low_halves, jnp.zeros((8, QB), I32))
        above_hi = above_hi.sum(axis=0, keepdims=True)
        thr_lo = bisect16(topk - above_hi)
        thr_lo16 = thr_lo.astype(jnp.int16)
        above = above_hi + count16(lambda k: k > thr_lo16)
        return lax.shift_left(thr_hi, 16) + (thr_lo - half_min), (topk - above).astype(F32)

    caps = sorted(set(min(c, ntiles) for c in range(2, ntiles + 2, 2)))
    thr, need = lax.switch(lax.shift_right_logical(nt - 1, 1),
                           [functools.partial(select, c) for c in caps])

    trow = lax.broadcasted_iota(I32, (KT, KT), 0)
    tcol = lax.broadcasted_iota(I32, (KT, KT), 1)
    tri = jnp.where(tcol < trow, 1.0, 0.0).astype(BF16)

    r_refs = (r0_ref, r1_ref)
    p_refs = (p0_ref, p1_ref)

    def selected(j, key, rank):
        sel = (key > thr) | ((key == thr) & (rank < need))
        return sel & jnp.logical_not(inadmissible(j))

    def weighted(j, slot, carry):
        acc_ref[...] += jnp.dot(ct_ref[j], p_refs[slot][...], preferred_element_type=F32)
        return carry

    def ties_before(j, run_eq):
        re_ref[j] = jnp.broadcast_to(run_eq, (8, QB))
        return run_eq + jnp.where(keys_ref[j] == thr, 1.0, 0.0).sum(axis=0, keepdims=True)

    lax.fori_loop(0, nt, ties_before, jnp.zeros((1, QB), F32))
    m_all = lax.fori_loop(0, nt, lambda j, mx: jnp.maximum(mx, mx_ref[j]),
                          jnp.full((8, hq), NEG, F32)).max(axis=0, keepdims=True)

    def fast_weights(j, slot):
        key = keys_ref[j]
        eqf = jnp.where(key == thr, 1.0, 0.0).astype(BF16)
        rank = jnp.dot(tri, eqf, preferred_element_type=F32) + re_ref[j, 0:1, :]
        keep = jnp.where(selected(j, key, rank), 1.0, 0.0).astype(BF16)
        for h in range(A_HEADS):
            cols = slice(h * QB, (h + 1) * QB)
            p = jnp.exp2(s_ref[j, :, cols] - m_all[:, cols]).astype(BF16)
            p_refs[slot][:, cols] = p * keep

    acc_ref[...] = jnp.zeros(acc_ref.shape, F32)
    pipelined(fast_weights, weighted, 0)

    @pl.when(jnp.min(acc_ref[A_LATENT:A_LATENT + 1, :]) < WEIGHT_SUM_FLOOR)
    def _exact_shift():
        def tie_ranks(j, slot):
            eqf = jnp.where(keys_ref[j] == thr, 1.0, 0.0).astype(BF16)
            r_refs[slot][...] = jnp.dot(tri, eqf, preferred_element_type=F32)

        def mask_tile(j, slot, m8):
            rank = r_refs[slot][...] + re_ref[j, 0:1, :]
            negm = jnp.where(selected(j, keys_ref[j], rank), 0.0, NEG)
            m8_new = []
            for h in range(A_HEADS):
                cols = slice(h * QB, (h + 1) * QB)
                blk = s_ref[j, :, cols] + negm
                s_ref[j, :, cols] = blk
                m8_new.append(jnp.maximum(m8[:, cols], blk.reshape(KT // 8, 8, QB).max(axis=0)))
            return jnp.concatenate(m8_new, axis=1)

        m8 = pipelined(tie_ranks, mask_tile, jnp.full((8, hq), NEG, F32))
        m = m8.max(axis=0, keepdims=True)

        def weights(j, slot):
            p_refs[slot][...] = jnp.exp2(s_ref[j] - m).astype(BF16)

        acc_ref[...] = jnp.zeros(acc_ref.shape, F32)
        pipelined(weights, weighted, 0)

    inv_l = 1.0 / acc_ref[A_LATENT:A_LATENT + 1, :]
    ot = (acc_ref[:A_LATENT, :] * inv_l).astype(BF16)
    for h in range(A_HEADS):
        o_ref[0, h * A_HEAD_DIM:(h + 1) * A_HEAD_DIM, :] = jnp.dot(
            wuvt_ref[h], ot[:, h * QB:(h + 1) * QB], preferred_element_type=F32
        ).astype(o_ref.dtype)


def dsa_attention(proj_h, proj_f, lat_g, w_uk, w_uv, bias, *, topk):
    b, s, _ = proj_h.shape
    ntiles = s // KT
    kern = functools.partial(_dsa_kernel, topk=topk, ntiles=ntiles)
    wuk_t = jnp.swapaxes(w_uk, 1, 2)
    zero = jnp.zeros_like(wuk_t)
    odd = (jnp.arange(A_HEADS) % 2 == 1)[:, None, None]
    wuk_pad = jnp.concatenate([jnp.where(odd, zero, wuk_t), jnp.where(odd, wuk_t, zero)], axis=1)
    col = lambda blk: (lambda bi, i: (bi, 0, blk))
    const = lambda shape: pl.BlockSpec(shape, lambda bi, i: (0,) * len(shape))
    return pl.pallas_call(
        kern,
        out_shape=jax.ShapeDtypeStruct((b, A_HEADS * A_HEAD_DIM, s), BF16),
        grid=(b, s // QB),
        in_specs=[
            pl.BlockSpec((1, QB, 512), lambda bi, i: (bi, i, PROJ_H_QA)),
            pl.BlockSpec((1, QB, 512), lambda bi, i: (bi, i, PROJ_H_QI)),
            pl.BlockSpec((1, QB, LANE), lambda bi, i: (bi, i, PROJ_F_WI)),
            pl.BlockSpec((1, s, LANE), col(PROJ_F_C)),
            pl.BlockSpec((1, s, LANE), col(PROJ_H_KI_LO)),
            pl.BlockSpec((1, s, LANE), col(PROJ_H_KI_HI)),
            const((1, A_LATENT)),
            const((A_HEADS, LANE, A_LATENT)),
            const((A_HEADS, A_HEAD_DIM, A_LATENT)),
            pl.BlockSpec((3, KT, A_HEADS * QB), lambda bi, i: (0, 0, 0),
                         pipeline_mode=pl.Buffered(1)),
        ],
        out_specs=pl.BlockSpec((1, A_HEADS * A_HEAD_DIM, QB), lambda bi, i: (bi, 0, i)),
        scratch_shapes=[
            pltpu.VMEM((s, A_LATENT), BF16),
            pltpu.VMEM((ntiles, ACC_ROWS, KT), BF16),
            pltpu.VMEM((ntiles, KT, QB), I32),
            pltpu.VMEM((ntiles, KT, QB), jnp.int16),
            pltpu.VMEM((ntiles, KT, A_HEADS * QB), F32),
            pltpu.VMEM((ACC_ROWS, A_HEADS * QB), F32),
            pltpu.VMEM((KT, IDX_HEADS * QB), F32),
            pltpu.VMEM((KT, IDX_HEADS * QB), F32),
            pltpu.VMEM((KT, QB), F32),
            pltpu.VMEM((KT, QB), F32),
            pltpu.VMEM((KT, A_HEADS * QB), BF16),
            pltpu.VMEM((KT, A_HEADS * QB), BF16),
            pltpu.VMEM((ntiles, 8, A_HEADS * QB), F32),
            pltpu.VMEM((ntiles, 8, QB), F32),
        ],
        compiler_params=_params(("parallel", "arbitrary")),
        name="dsa_attention",
    )(proj_h, proj_h, proj_f, proj_f, proj_h, proj_h, lat_g.reshape(1, A_LATENT),
      wuk_pad.astype(BF16), jnp.swapaxes(w_uv, 1, 2).astype(BF16), bias)


def _hgrn_kernel(f_ref, v_ref, q_ref, g_ref, lb_ref, og_ref, o_ref, st_ref, *, nchunk):
    @pl.when(pl.program_id(1) == 0)
    def _():
        st_ref[...] = jnp.zeros(st_ref.shape, F32)

    lb = lb_ref[...]
    og = og_ref[...]
    row = lax.broadcasted_iota(I32, (CHUNK, CHUNK), 0)
    col = lax.broadcasted_iota(I32, (CHUNK, CHUNK), 1)
    causal = col <= row
    tri = jnp.where(causal, 1.0, 0.0).astype(BF16)
    hc = CHUNK // 2
    top = lax.broadcasted_iota(I32, (CHUNK, B_HEADS * B_DK), 0) < hc
    zeros_h = jnp.zeros((hc, B_DK), BF16)

    for c in range(nchunk):
        rows = slice(c * CHUNK, (c + 1) * CHUNK)
        f = lb + (1.0 - lb) * jax.nn.sigmoid(f_ref[0, rows, :])
        g = jnp.log(f)
        k = 1.0 - f
        q = q_ref[0, rows, :]
        qf = q * jax.nn.sigmoid(q)
        vb = v_ref[0, rows, :]
        g_hi = g.astype(BF16)
        g_lo = (g - g_hi.astype(F32)).astype(BF16)
        bc = (jnp.dot(tri, g_hi, preferred_element_type=F32)
              + jnp.dot(tri, g_lo, preferred_element_type=F32))
        b_end = bc[CHUNK - 1:CHUNK, :]
        b_half = bc[hc - 1:hc, :]
        e_in = bc - jnp.where(top, bc[hc // 2 - 1:hc // 2, :], bc[hc + hc // 2 - 1:hc + hc // 2, :])
        q_in = (qf * jnp.exp(e_in)).astype(BF16)
        k_in = (k * jnp.exp(-e_in)).astype(BF16)
        q_x = (qf[hc:, :] * jnp.exp(bc[hc:, :] - b_half)).astype(BF16)
        k_x = (k[:hc, :] * jnp.exp(b_half - bc[:hc, :])).astype(BF16)
        qb = (qf * jnp.exp(bc)).astype(BF16)
        kd = (k * jnp.exp(b_end - bc)).astype(BF16)
        dec = jnp.exp(b_end)
        gate = g_ref[0, rows, :]
        gate = gate * jax.nn.sigmoid(gate)
        for h in range(B_HEADS):
            sl = slice(h * B_DK, (h + 1) * B_DK)
            q3 = jnp.concatenate([
                jnp.concatenate([q_in[:hc, sl], zeros_h, zeros_h], axis=1),
                jnp.concatenate([zeros_h, q_x[:, sl], q_in[hc:, sl]], axis=1)], axis=0)
            k3 = jnp.concatenate([
                jnp.concatenate([k_in[:hc, sl], k_x[:, sl], zeros_h], axis=1),
                jnp.concatenate([zeros_h, zeros_h, k_in[hc:, sl]], axis=1)], axis=0)
            sc = lax.dot_general(q3, k3, NT, preferred_element_type=F32)
            sc = jnp.where(causal, sc, 0.0).astype(BF16)
            st = st_ref[h]
            o = (jnp.dot(sc, vb[:, sl], preferred_element_type=F32)
                 + lax.dot_general(qb[:, sl], st.astype(BF16), NT, preferred_element_type=F32))
            st_ref[h] = st * dec[:, sl] + lax.dot_general(vb[:, sl], kd[:, sl], TN,
                                                          preferred_element_type=F32)
            o_ref[0, rows, sl] = (_rms(o, og) * gate[:, sl]).astype(o_ref.dtype)


def hgrn2(proj_h, proj_f, lb, o_g, *, rows):
    b, s, _ = proj_h.shape
    width = B_HEADS * B_DK
    kern = functools.partial(_hgrn_kernel, nchunk=rows // CHUNK)
    spec = lambda blk: pl.BlockSpec((1, rows, width), lambda bi, r: (bi, r, blk))
    return pl.pallas_call(
        kern,
        out_shape=jax.ShapeDtypeStruct((b, s, width), BF16),
        grid=(b, s // rows),
        in_specs=[spec(PROJ_F_F), spec(PROJ_H_V), spec(PROJ_F_Q), spec(PROJ_F_GATE),
                  pl.BlockSpec((1, width), lambda bi, r: (0, 0)),
                  pl.BlockSpec((1, B_DK), lambda bi, r: (0, 0))],
        out_specs=pl.BlockSpec((1, rows, width), lambda bi, r: (bi, r, 0)),
        scratch_shapes=[pltpu.VMEM((B_HEADS, B_DK, B_DK), F32)],
        compiler_params=_params(("parallel", "arbitrary")),
        name="hgrn2",
    )(proj_f, proj_h, proj_f, proj_f, lb.reshape(1, width), o_g.reshape(1, B_DK))


def _gmlp_kernel(h_ref, g_ref, win_ref, lng_ref, lnb_ref, wsp_ref, bsp_ref, wout_ref, o_ref,
                 gated_ref, *, nchunk):
    x = h_ref[0]
    width = x.shape[1]
    xn = _rms(x, g_ref[...]).astype(BF16)
    uv = jnp.dot(xn, win_ref[...], preferred_element_type=F32)
    uv = 0.5 * uv * (1.0 + lax.erf(uv * (0.5 ** 0.5)))
    u = uv[:, :width]
    v = uv[:, width:]
    mu = jnp.mean(v, axis=-1, keepdims=True)
    vc = v - mu
    var = jnp.mean(vc * vc, axis=-1, keepdims=True)
    vn = (vc * lax.rsqrt(var + EPS) * lng_ref[...] + lnb_ref[...]).astype(BF16)
    row = lax.broadcasted_iota(I32, (GM_CHUNK, GM_CHUNK), 0)
    col = lax.broadcasted_iota(I32, (GM_CHUNK, GM_CHUNK), 1)
    gw = width // GM_GROUPS
    for gi in range(GM_GROUPS):
        wg = jnp.where(col <= row, wsp_ref[gi], 0.0).astype(BF16)
        cs = slice(gi * gw, (gi + 1) * gw)
        for c in range(nchunk):
            rs = slice(c * GM_CHUNK, (c + 1) * GM_CHUNK)
            mixed = jnp.dot(wg, vn[rs, cs], preferred_element_type=F32) + bsp_ref[:, cs]
            gated_ref[rs, cs] = (u[rs, cs] * mixed).astype(BF16)
    o_ref[0] = x + jnp.dot(gated_ref[...], wout_ref[...], preferred_element_type=F32)


def gmlp(h, g, w_in, ln_g, ln_b, w_sp, b_sp, w_out, *, rows):
    b, s, d = h.shape
    width = w_out.shape[0]
    gw = width // GM_GROUPS
    bsp_full = jnp.repeat(jnp.transpose(b_sp), gw, axis=1)
    kern = functools.partial(_gmlp_kernel, nchunk=rows // GM_CHUNK)
    const = lambda shape: pl.BlockSpec(shape, lambda bi, r: (0,) * len(shape),
                                       pipeline_mode=pl.Buffered(1))
    return pl.pallas_call(
        kern,
        out_shape=jax.ShapeDtypeStruct((b, s, d), F32),
        grid=(b, s // rows),
        in_specs=[pl.BlockSpec((1, rows, d), lambda bi, r: (bi, r, 0)),
                  const((1, d)), const((d, 2 * width)), const((1, width)), const((1, width)),
                  const((GM_GROUPS, GM_CHUNK, GM_CHUNK)), const((GM_CHUNK, width)),
                  const((width, d))],
        out_specs=pl.BlockSpec((1, rows, d), lambda bi, r: (bi, r, 0)),
        scratch_shapes=[pltpu.VMEM((rows, width), BF16)],
        compiler_params=_params(("parallel", "parallel")),
        name="gmlp",
    )(h, g.reshape(1, d), w_in.astype(BF16), ln_g.reshape(1, width), ln_b.reshape(1, width),
      w_sp, bsp_full, w_out.astype(BF16))


def _xattn_kernel(*refs, mixed):
    if mixed:
        h_ref, at_ref, b_ref, wa_ref, wb_ref, g_ref, wq_ref, k_ref, v_ref, wo_ref, o_ref = refs
        x = (h_ref[0] + lax.dot_general(at_ref[0], wa_ref[...], TN, preferred_element_type=F32)
             + jnp.dot(b_ref[0], wb_ref[...], preferred_element_type=F32))
    else:
        h_ref, g_ref, wq_ref, k_ref, v_ref, wo_ref, o_ref = refs
        x = h_ref[0]
    xn = _rms(x, g_ref[...]).astype(BF16)
    q = jnp.dot(xn, wq_ref[...], preferred_element_type=F32).astype(BF16)
    outs = []
    for h in range(X_HEADS):
        sl = slice(h * X_HEAD_DIM, (h + 1) * X_HEAD_DIM)
        s = lax.dot_general(q[:, sl], k_ref[0, :, sl], NT, preferred_element_type=F32)
        s = s * (X_HEAD_DIM ** -0.5)
        p = jnp.exp(s - s.max(axis=-1, keepdims=True))
        l = p.sum(axis=-1, keepdims=True)
        o = jnp.dot(p.astype(BF16), v_ref[0, :, sl], preferred_element_type=F32)
        outs.append((o * (1.0 / l)).astype(BF16))
    att = jnp.concatenate(outs, axis=1)
    o_ref[0] = x + jnp.dot(att, wo_ref[...], preferred_element_type=F32)


def xattn(h, g, wq, kv, wo, *, rows, mix=None):
    b, s, d = h.shape
    m = kv.shape[1]
    xw = wq.shape[1]
    const = lambda shape: pl.BlockSpec(shape, lambda bi, r: (0,) * len(shape))
    rows_spec = lambda width: pl.BlockSpec((1, rows, width), lambda bi, r: (bi, r, 0))
    mix_specs, mix_args = [], []
    if mix is not None:
        a_t, b_out, wa, wb = mix
        mix_specs = [pl.BlockSpec((1, a_t.shape[1], rows), lambda bi, r: (bi, 0, r)),
                     rows_spec(b_out.shape[2]), const(wa.shape), const(wb.shape)]
        mix_args = [a_t, b_out, wa, wb]
    return pl.pallas_call(
        functools.partial(_xattn_kernel, mixed=mix is not None),
        out_shape=jax.ShapeDtypeStruct((b, s, d), F32),
        grid=(b, s // rows),
        in_specs=[rows_spec(d), *mix_specs,
                  const((1, d)), const((d, xw)),
                  pl.BlockSpec((1, m, xw), lambda bi, r: (bi, 0, 0)),
                  pl.BlockSpec((1, m, xw), lambda bi, r: (bi, 0, 1)),
                  const((xw, d))],
        out_specs=rows_spec(d),
        compiler_params=_params(("parallel", "parallel")),
        name="xattn",
    )(h, *mix_args, g.reshape(1, d), wq.astype(BF16), kv, kv, wo.astype(BF16))


def _swiglu_kernel(x_ref, g_ref, wgu_ref, wd_ref, fg_ref, o_ref, a_ref, *, ff, fchunk, final):
    x = x_ref[...]
    xn = _rms(x, g_ref[...]).astype(BF16)
    for c in range(ff // fchunk):
        gate = jnp.dot(xn, wgu_ref[:, c * fchunk:(c + 1) * fchunk], preferred_element_type=F32)
        up = jnp.dot(xn, wgu_ref[:, ff + c * fchunk:ff + (c + 1) * fchunk],
                     preferred_element_type=F32)
        a_ref[:, c * fchunk:(c + 1) * fchunk] = (gate * jax.nn.sigmoid(gate) * up).astype(BF16)
    y = x + jnp.dot(a_ref[...], wd_ref[...], preferred_element_type=F32)
    if final:
        y = _rms(y, fg_ref[...])
    o_ref[...] = y


def swiglu(x, g, w_gu, w_down, final_g, *, tm, final):
    n, d = x.shape
    ff = w_down.shape[0]
    kern = functools.partial(_swiglu_kernel, ff=ff, fchunk=256, final=final)
    once = dict(pipeline_mode=pl.Buffered(1))
    return pl.pallas_call(
        kern,
        out_shape=jax.ShapeDtypeStruct((n, d), F32),
        grid=(n // tm,),
        in_specs=[pl.BlockSpec((tm, d), lambda i: (i, 0)),
                  pl.BlockSpec((1, d), lambda i: (0, 0)),
                  pl.BlockSpec((d, 2 * ff), lambda i: (0, 0), **once),
                  pl.BlockSpec((ff, d), lambda i: (0, 0), **once),
                  pl.BlockSpec((1, d), lambda i: (0, 0))],
        out_specs=pl.BlockSpec((tm, d), lambda i: (i, 0)),
        scratch_shapes=[pltpu.VMEM((tm, ff), BF16)],
        compiler_params=_params(("parallel",)),
        name="swiglu",
    )(x, g.reshape(1, d), w_gu.astype(BF16), w_down.astype(BF16), final_g.reshape(1, d))


def _pack_in_proj(w):
    sizes = [A_HEADS * A_HEAD_DIM, A_LATENT, IDX_HEADS * IDX_DIM, IDX_DIM, IDX_HEADS,
             B_HEADS * B_DK, B_HEADS * B_DK, B_HEADS * B_DK, B_HEADS * B_DK]
    offs = [0]
    for sz in sizes:
        offs.append(offs[-1] + sz)
    q_a, c, qi, ki, wi, f_b, i_b, q_b, g_b = [w[:, offs[n]:offs[n + 1]] for n in range(9)]
    z = lambda n: jnp.zeros((w.shape[0], n), w.dtype)
    return jnp.concatenate([q_a, qi, i_b, ki, z(LANE - IDX_DIM), z(LANE - IDX_DIM), ki,
                            f_b, q_b, g_b, c, wi, z(LANE - IDX_HEADS)], axis=1)


def kernel(x, mem, rel_bias, hgrn_lb, mix_norm, e_w_in, e_lat_norm, e_w_uk, e_w_uv, e_o_norm, e_w_out, o_w_in, o_ln_g, o_ln_b, o_w_sp, o_b_sp, o_w_out, x_norm, mem_norm, x_wq, x_wkv, x_wo, f_norm, f_w_gu, f_w_down, final_norm):
    b, s, d = x.shape
    m = mem.shape[1]
    n = b * s
    depth = mix_norm.shape[0]
    topk = min(TOPK_MAX, s // 4)
    lb_all = jnp.cumsum(jax.nn.softmax(hgrn_lb.astype(F32), axis=0), axis=0)
    bias = bias_tiles(rel_bias)
    mem2 = mem.reshape(b * m, d)

    h = x.reshape(n, d)
    for l in range(depth):
        j = l // 2
        mix = None
        if l % 2 == 0:
            proj_h, proj_f = norm_matmul(h, mix_norm[l], _pack_in_proj(e_w_in[j]).astype(BF16),
                                         tm=1024, tn=256,
                                         groups=((PROJ_H_COLS, BF16), (PROJ_F_COLS, F32)))
            proj_h = proj_h.reshape(b, s, -1)
            proj_f = proj_f.reshape(b, s, -1)
            a_out = dsa_attention(proj_h, proj_f, e_lat_norm[j], e_w_uk[j], e_w_uv[j], bias,
                                  topk=topk)
            b_out = hgrn2(proj_h, proj_f, lb_all[l], e_o_norm[j], rows=512)
            wa = e_w_out[j][:A_HEADS * A_HEAD_DIM].astype(BF16)
            wb = e_w_out[j][A_HEADS * A_HEAD_DIM:].astype(BF16)
            mix = (a_out, b_out, wa, wb)
        else:
            h = gmlp(h.reshape(b, s, d), mix_norm[l], o_w_in[j], o_ln_g[j], o_ln_b[j], o_w_sp[j],
                     o_b_sp[j], o_w_out[j], rows=1024).reshape(n, d)
        (kv,) = norm_matmul(mem2, mem_norm[l], x_wkv[l].astype(BF16), tm=512, tn=512,
                            groups=((x_wkv.shape[2], BF16),))
        kv = kv.reshape(b, m, -1)
        h = xattn(h.reshape(b, s, d), x_norm[l], x_wq[l], kv, x_wo[l], rows=1024,
                  mix=mix).reshape(n, d)
        h = swiglu(h, f_norm[l], f_w_gu[l], f_w_down[l], final_norm, tm=1024,
                   final=(l == depth - 1))
    return h.reshape(b, s, d)
```

```python
import functools

import jax
import jax.numpy as jnp
from jax import lax
from jax.experimental import pallas as pl
from jax.experimental.pallas import tpu as pltpu

F32 = jnp.float32
BF16 = jnp.bfloat16
I32 = jnp.int32

EPS = 1e-6
CHUNK = 64
LANE = 128
KT = 256
QB = 256
A_HEADS = 8
A_HEAD_DIM = 64
A_LATENT = 128
IDX_HEADS = 8
IDX_DIM = 64
TOPK_MAX = 256
REL_BUCKETS = 32
B_HEADS = 4
B_DK = 128
GM_CHUNK = 128
GM_GROUPS = 8
X_HEADS = 4
X_HEAD_DIM = 128
NEG = -1e30
INT_MIN = -2**31
LOG2E = 1.4426950408889634
ACC_ROWS = A_LATENT + 16
WEIGHT_SUM_FLOOR = 2.0 ** -60
PROJ_H_COLS = PROJ_F_COLS = 1792
PROJ_H_QA, PROJ_H_QI, PROJ_H_V = 0, 1, 2
PROJ_H_KI_LO, PROJ_H_KI_HI = 12, 13
PROJ_F_F, PROJ_F_Q, PROJ_F_GATE = 0, 1, 2
PROJ_F_C, PROJ_F_WI = 12, 13

VMEM_LIMIT = 56 * 1024 * 1024

NT = (((1,), (1,)), ((), ()))
TN = (((0,), (0,)), ((), ()))


def _rms(x, g):
    ms = jnp.mean(x * x, axis=-1, keepdims=True)
    return x * lax.rsqrt(ms + EPS) * g


def _params(sem, vmem=VMEM_LIMIT):
    return pltpu.CompilerParams(dimension_semantics=sem, vmem_limit_bytes=vmem)


def _norm_matmul_kernel(x_ref, g_ref, w_ref, *o_refs, tn):
    xn = _rms(x_ref[...], g_ref[...]).astype(BF16)
    base = 0
    for o_ref in o_refs:
        for c in range(o_ref.shape[1] // tn):
            y = jnp.dot(xn, w_ref[:, base + c * tn:base + (c + 1) * tn], preferred_element_type=F32)
            o_ref[:, c * tn:(c + 1) * tn] = y.astype(o_ref.dtype)
        base += o_ref.shape[1]


def norm_matmul(x, g, w, *, tm, tn, groups):
    n, k = x.shape
    nout = w.shape[1]
    assert sum(nc for nc, _ in groups) == nout and all(nc % tn == 0 for nc, _ in groups)
    return pl.pallas_call(
        functools.partial(_norm_matmul_kernel, tn=tn),
        out_shape=tuple(jax.ShapeDtypeStruct((n, nc), dt) for nc, dt in groups),
        grid=(n // tm,),
        in_specs=[
            pl.BlockSpec((tm, k), lambda i: (i, 0)),
            pl.BlockSpec((1, k), lambda i: (0, 0)),
            pl.BlockSpec((k, nout), lambda i: (0, 0), pipeline_mode=pl.Buffered(1)),
        ],
        out_specs=tuple(pl.BlockSpec((tm, nc), lambda i: (i, 0)) for nc, _ in groups),
        compiler_params=_params(("parallel",)),
        name="norm_matmul",
    )(x, g.reshape(1, k), w)


def _bias_tiles_kernel(rb_ref, o_ref):
    s = lax.broadcasted_iota(I32, (KT, QB), 0)
    t = lax.broadcasted_iota(I32, (KT, QB), 1)
    nb = REL_BUCKETS // 2
    max_exact = nb // 2
    for kind in range(3):
        rel = s - t - KT * kind
        n = jnp.abs(rel)
        n2 = n * n
        large = jnp.full((KT, QB), max_exact, I32)
        for j in range(1, nb - max_exact):
            large = large + jnp.where(n2 >= (max_exact * max_exact) * (2 ** j), 1, 0)
        bucket = jnp.where(rel > 0, nb, 0) + jnp.where(n < max_exact, n, large)
        buckets = (range(REL_BUCKETS), range(nb), (nb - 1,))[kind]
        hits = [(b, bucket == b) for b in buckets]
        for h in range(A_HEADS):
            acc = jnp.zeros((KT, QB), F32)
            for b, hit in hits:
                acc = jnp.where(hit, rb_ref[b, h] * LOG2E, acc)
            o_ref[kind, :, h * QB:(h + 1) * QB] = acc


def bias_tiles(rel_bias):
    return pl.pallas_call(
        _bias_tiles_kernel,
        out_shape=jax.ShapeDtypeStruct((3, KT, A_HEADS * QB), F32),
        in_specs=[pl.BlockSpec(memory_space=pltpu.SMEM)],
        out_specs=pl.BlockSpec(memory_space=pltpu.VMEM),
        name="bias_tiles",
    )(rel_bias)


def _dsa_kernel(qa_ref, qi_ref, wi_ref, c_ref, ka_ref, kb_ref, latg_ref, wuk_ref, wuvt_ref, bias_ref,
                o_ref, cn_ref, ct_ref, keys_ref, hi_ref, s_ref, acc_ref,
                d0_ref, d1_ref, r0_ref, r1_ref, p0_ref, p1_ref, mx_ref, re_ref, *, topk, ntiles):
    i = pl.program_id(1)
    hq = A_HEADS * QB
    nt = i + 1
    last = i

    @pl.when(i == 0)
    def _prep():
        cn = _rms(c_ref[0], latg_ref[...])
        cn_ref[...] = cn.astype(BF16)
        ones = jnp.ones((ACC_ROWS - A_LATENT, KT), BF16)
        for j in range(ntiles):
            ct_ref[j, :A_LATENT, :] = cn[j * KT:(j + 1) * KT, :].T.astype(BF16)
            ct_ref[j, A_LATENT:, :] = ones

    qa = qa_ref[0]
    ql = jnp.concatenate(
        [jnp.dot(qa[:, (h // 2) * LANE:(h // 2 + 1) * LANE], wuk_ref[h],
                 preferred_element_type=F32) for h in range(A_HEADS)], axis=0)
    qlb = (ql * (A_HEAD_DIM ** -0.5 * LOG2E)).astype(BF16)
    qi = qi_ref[0]
    qp = jnp.concatenate([qi[:, p * LANE:(p + 1) * LANE] for p in range(IDX_HEADS // 2)], axis=0)
    w = wi_ref[0] * (IDX_HEADS ** -0.5) * (IDX_DIM ** -0.5)
    wt = w.T

    row = lax.broadcasted_iota(I32, (KT, QB), 0)
    col = lax.broadcasted_iota(I32, (KT, QB), 1)
    chunk_bits = CHUNK.bit_length() - 1
    inadm_last = jnp.where(lax.shift_right_logical(row, chunk_bits)
                           > lax.shift_right_logical(col, chunk_bits), 1, 0)

    def inadmissible(j):
        return (inadm_last * jnp.where(j == last, 1, 0)) != 0

    def rows(j):
        return pl.ds(pl.multiple_of(j * KT, KT), KT)

    def pipelined(prepare, consume, carry):
        prepare(0, 0)

        def pair(t, carry):
            j = 2 * t
            prepare(j + 1, 1)
            carry = consume(j, 0, carry)
            prepare(jnp.minimum(j + 2, last), 0)
            return consume(j + 1, 1, carry)

        carry = lax.fori_loop(0, lax.shift_right_logical(nt, 1), pair, carry)
        return lax.cond((nt & 1) == 1, lambda c: consume(last, 0, c), lambda c: c, carry)

    d_refs = (d0_ref, d1_ref)

    def dots(j, slot):
        ks = rows(j)
        d_ref = d_refs[slot]
        kk = jnp.concatenate([ka_ref[0, ks, :], kb_ref[0, ks, :]], axis=0)
        d_ref[...] = lax.dot_general(kk, qp, NT, preferred_element_type=F32)
        s = lax.dot_general(cn_ref[ks, :], qlb, NT, preferred_element_type=F32)
        s = s + bias_ref[jnp.minimum(i - j, 2)]
        s_ref[j] = s
        mx_ref[j] = s.reshape(KT // 8, 8, hq).max(axis=0)

    def score_keys(j, slot, carry):
        d_ref = d_refs[slot]
        sc = jnp.zeros((KT, QB), F32)
        for h in range(IDX_HEADS):
            d = d_ref[(h % 2) * KT:(h % 2 + 1) * KT, (h // 2) * QB:(h // 2 + 1) * QB]
            sc = sc + jnp.maximum(d, 0.0) * wt[h:h + 1, :]
        sc = jnp.where(inadmissible(j), -jnp.inf, sc)
        bits = pltpu.bitcast(sc, I32)
        key = jnp.where(bits < 0, bits ^ 0x7FFFFFFF, bits)
        key = jnp.where(bits == INT_MIN, 0, key)
        keys_ref[j] = key
        hi_ref[j] = lax.shift_right_arithmetic(key, 16).astype(jnp.int16)
        return carry

    pipelined(dots, score_keys, 0)

    half_min = -2 ** 15

    @pl.when(nt < ntiles)
    def _pad():
        hi_ref[nt] = jnp.full((KT, QB), half_min, jnp.int16)

    def select(cap):
        def count16(pred):
            acc = jnp.zeros((16, QB), jnp.int16)
            for j in range(cap):
                hit = jnp.where(pred(hi_ref[j]), jnp.int16(1), jnp.int16(0))
                parts = [hit[r * 16:(r + 1) * 16, :] for r in range(KT // 16)]
                while len(parts) > 1:
                    parts = [a + b for a, b in zip(parts[::2], parts[1::2])]
                acc = acc + parts[0]
            return acc.astype(I32).sum(axis=0, keepdims=True)

        def bisect16(target):
            def step_fn(step, lo):
                trial = lo + lax.shift_left(jnp.int32(1), 15 - step)
                t16 = trial.astype(jnp.int16)
                return jnp.where(count16(lambda k: k >= t16) >= target, trial, lo)
            return lax.fori_loop(0, 16, step_fn, jnp.full((1, QB), half_min, I32))

        thr_hi = bisect16(topk)

        def low_halves(j, acc):
            key = keys_ref[j]
            hi = lax.shift_right_arithmetic(key, 16)
            lo = (key & 0xFFFF) + half_min
            hi_ref[j] = jnp.where(hi == thr_hi, lo, half_min).astype(jnp.int16)
            above = jnp.where(hi > thr_hi, 1, 0)
            return acc + above.reshape(KT // 8, 8, QB).sum(axis=0)

        above_hi = lax.fori_loop(0, nt, low_halves, jnp.zeros((8, QB), I32))
        above_hi = above_hi.sum(axis=0, keepdims=True)
        thr_lo = bisect16(topk - above_hi)
        thr_lo16 = thr_lo.astype(jnp.int16)
        above = above_hi + count16(lambda k: k > thr_lo16)
        return lax.shift_left(thr_hi, 16) + (thr_lo - half_min), (topk - above).astype(F32)

    caps = sorted(set(min(c, ntiles) for c in range(2, ntiles + 2, 2)))
    thr, need = lax.switch(lax.shift_right_logical(nt - 1, 1),
                           [functools.partial(select, c) for c in caps])

    trow = lax.broadcasted_iota(I32, (KT, KT), 0)
    tcol = lax.broadcasted_iota(I32, (KT, KT), 1)
    tri = jnp.where(tcol < trow, 1.0, 0.0).astype(BF16)

    r_refs = (r0_ref, r1_ref)
    p_refs = (p0_ref, p1_ref)

    def selected(j, key, rank):
        sel = (key > thr) | ((key == thr) & (rank < need))
        return sel & jnp.logical_not(inadmissible(j))

    def weighted(j, slot, carry):
        acc_ref[...] += jnp.dot(ct_ref[j], p_refs[slot][...], preferred_element_type=F32)
        return carry

    def ties_before(j, run_eq):
        re_ref[j] = jnp.broadcast_to(run_eq, (8, QB))
        return run_eq + jnp.where(keys_ref[j] == thr, 1.0, 0.0).sum(axis=0, keepdims=True)

    lax.fori_loop(0, nt, ties_before, jnp.zeros((1, QB), F32))
    m_all = lax.fori_loop(0, nt, lambda j, mx: jnp.maximum(mx, mx_ref[j]),
                          jnp.full((8, hq), NEG, F32)).max(axis=0, keepdims=True)

    def fast_weights(j, slot):
        key = keys_ref[j]
        eqf = jnp.where(key == thr, 1.0, 0.0).astype(BF16)
        rank = jnp.dot(tri, eqf, preferred_element_type=F32) + re_ref[j, 0:1, :]
        keep = jnp.where(selected(j, key, rank), 1.0, 0.0).astype(BF16)
        for h in range(A_HEADS):
            cols = slice(h * QB, (h + 1) * QB)
            p = jnp.exp2(s_ref[j, :, cols] - m_all[:, cols]).astype(BF16)
            p_refs[slot][:, cols] = p * keep

    acc_ref[...] = jnp.zeros(acc_ref.shape, F32)
    pipelined(fast_weights, weighted, 0)

    @pl.when(jnp.min(acc_ref[A_LATENT:A_LATENT + 1, :]) < WEIGHT_SUM_FLOOR)
    def _exact_shift():
        def tie_ranks(j, slot):
            eqf = jnp.where(keys_ref[j] == thr, 1.0, 0.0).astype(BF16)
            r_refs[slot][...] = jnp.dot(tri, eqf, preferred_element_type=F32)

        def mask_tile(j, slot, m8):
            rank = r_refs[slot][...] + re_ref[j, 0:1, :]
            negm = jnp.where(selected(j, keys_ref[j], rank), 0.0, NEG)
            m8_new = []
            for h in range(A_HEADS):
                cols = slice(h * QB, (h + 1) * QB)
                blk = s_ref[j, :, cols] + negm
                s_ref[j, :, cols] = blk
                m8_new.append(jnp.maximum(m8[:, cols], blk.reshape(KT // 8, 8, QB).max(axis=0)))
            return jnp.concatenate(m8_new, axis=1)

        m8 = pipelined(tie_ranks, mask_tile, jnp.full((8, hq), NEG, F32))
        m = m8.max(axis=0, keepdims=True)

        def weights(j, slot):
            p_refs[slot][...] = jnp.exp2(s_ref[j] - m).astype(BF16)

        acc_ref[...] = jnp.zeros(acc_ref.shape, F32)
        pipelined(weights, weighted, 0)

    inv_l = 1.0 / acc_ref[A_LATENT:A_LATENT + 1, :]
    ot = (acc_ref[:A_LATENT, :] * inv_l).astype(BF16)
    for h in range(A_HEADS):
        o_ref[0, h * A_HEAD_DIM:(h + 1) * A_HEAD_DIM, :] = jnp.dot(
            wuvt_ref[h], ot[:, h * QB:(h + 1) * QB], preferred_element_type=F32
        ).astype(o_ref.dtype)


def dsa_attention(proj_h, proj_f, lat_g, w_uk, w_uv, bias, *, topk):
    b, s, _ = proj_h.shape
    ntiles = s // KT
    kern = functools.partial(_dsa_kernel, topk=topk, ntiles=ntiles)
    wuk_t = jnp.swapaxes(w_uk, 1, 2)
    zero = jnp.zeros_like(wuk_t)
    odd = (jnp.arange(A_HEADS) % 2 == 1)[:, None, None]
    wuk_pad = jnp.concatenate([jnp.where(odd, zero, wuk_t), jnp.where(odd, wuk_t, zero)], axis=1)
    col = lambda blk: (lambda bi, i: (bi, 0, blk))
    const = lambda shape: pl.BlockSpec(shape, lambda bi, i: (0,) * len(shape))
    return pl.pallas_call(
        kern,
        out_shape=jax.ShapeDtypeStruct((b, A_HEADS * A_HEAD_DIM, s), BF16),
        grid=(b, s // QB),
        in_specs=[
            pl.BlockSpec((1, QB, 512), lambda bi, i: (bi, i, PROJ_H_QA)),
            pl.BlockSpec((1, QB, 512), lambda bi, i: (bi, i, PROJ_H_QI)),
            pl.BlockSpec((1, QB, LANE), lambda bi, i: (bi, i, PROJ_F_WI)),
            pl.BlockSpec((1, s, LANE), col(PROJ_F_C)),
            pl.BlockSpec((1, s, LANE), col(PROJ_H_KI_LO)),
            pl.BlockSpec((1, s, LANE), col(PROJ_H_KI_HI)),
            const((1, A_LATENT)),
            const((A_HEADS, LANE, A_LATENT)),
            const((A_HEADS, A_HEAD_DIM, A_LATENT)),
            pl.BlockSpec((3, KT, A_HEADS * QB), lambda bi, i: (0, 0, 0),
                         pipeline_mode=pl.Buffered(1)),
        ],
        out_specs=pl.BlockSpec((1, A_HEADS * A_HEAD_DIM, QB), lambda bi, i: (bi, 0, i)),
        scratch_shapes=[
            pltpu.VMEM((s, A_LATENT), BF16),
            pltpu.VMEM((ntiles, ACC_ROWS, KT), BF16),
            pltpu.VMEM((ntiles, KT, QB), I32),
            pltpu.VMEM((ntiles, KT, QB), jnp.int16),
            pltpu.VMEM((ntiles, KT, A_HEADS * QB), F32),
            pltpu.VMEM((ACC_ROWS, A_HEADS * QB), F32),
            pltpu.VMEM((2 * KT, (IDX_HEADS // 2) * QB), F32),
            pltpu.VMEM((2 * KT, (IDX_HEADS // 2) * QB), F32),
            pltpu.VMEM((KT, QB), F32),
            pltpu.VMEM((KT, QB), F32),
            pltpu.VMEM((KT, A_HEADS * QB), BF16),
            pltpu.VMEM((KT, A_HEADS * QB), BF16),
            pltpu.VMEM((ntiles, 8, A_HEADS * QB), F32),
            pltpu.VMEM((ntiles, 8, QB), F32),
        ],
        compiler_params=_params(("parallel", "arbitrary")),
        name="dsa_attention",
    )(proj_h, proj_h, proj_f, proj_f, proj_h, proj_h, lat_g.reshape(1, A_LATENT),
      wuk_pad.astype(BF16), jnp.swapaxes(w_uv, 1, 2).astype(BF16), bias)


def _hgrn_kernel(f_ref, v_ref, q_ref, g_ref, lb_ref, og_ref, o_ref, st_ref, *, nchunk):
    @pl.when(pl.program_id(1) == 0)
    def _():
        st_ref[...] = jnp.zeros(st_ref.shape, F32)

    lb = lb_ref[...]
    og = og_ref[...]
    row = lax.broadcasted_iota(I32, (CHUNK, CHUNK), 0)
    col = lax.broadcasted_iota(I32, (CHUNK, CHUNK), 1)
    causal = col <= row
    tri = jnp.where(causal, 1.0, 0.0).astype(BF16)
    hc = CHUNK // 2
    top = lax.broadcasted_iota(I32, (CHUNK, B_HEADS * B_DK), 0) < hc
    zeros_h = jnp.zeros((hc, B_DK), BF16)

    for c in range(nchunk):
        rows = slice(c * CHUNK, (c + 1) * CHUNK)
        f = lb + (1.0 - lb) * jax.nn.sigmoid(f_ref[0, rows, :])
        g = jnp.log(f)
        k = 1.0 - f
        q = q_ref[0, rows, :]
        qf = q * jax.nn.sigmoid(q)
        vb = v_ref[0, rows, :]
        g_hi = g.astype(BF16)
        g_lo = (g - g_hi.astype(F32)).astype(BF16)
        bc = (jnp.dot(tri, g_hi, preferred_element_type=F32)
              + jnp.dot(tri, g_lo, preferred_element_type=F32))
        b_end = bc[CHUNK - 1:CHUNK, :]
        b_half = bc[hc - 1:hc, :]
        e_in = bc - jnp.where(top, bc[hc // 2 - 1:hc // 2, :], bc[hc + hc // 2 - 1:hc + hc // 2, :])
        q_in = (qf * jnp.exp(e_in)).astype(BF16)
        k_in = (k * jnp.exp(-e_in)).astype(BF16)
        q_x = (qf[hc:, :] * jnp.exp(bc[hc:, :] - b_half)).astype(BF16)
        k_x = (k[:hc, :] * jnp.exp(b_half - bc[:hc, :])).astype(BF16)
        qb = (qf * jnp.exp(bc)).astype(BF16)
        kd = (k * jnp.exp(b_end - bc)).astype(BF16)
        dec = jnp.exp(b_end)
        gate = g_ref[0, rows, :]
        gate = gate * jax.nn.sigmoid(gate)
        for h in range(B_HEADS):
            sl = slice(h * B_DK, (h + 1) * B_DK)
            q3 = jnp.concatenate([
                jnp.concatenate([q_in[:hc, sl], zeros_h, zeros_h], axis=1),
                jnp.concatenate([zeros_h, q_x[:, sl], q_in[hc:, sl]], axis=1)], axis=0)
            k3 = jnp.concatenate([
                jnp.concatenate([k_in[:hc, sl], k_x[:, sl], zeros_h], axis=1),
                jnp.concatenate([zeros_h, zeros_h, k_in[hc:, sl]], axis=1)], axis=0)
            sc = lax.dot_general(q3, k3, NT, preferred_element_type=F32)
            sc = jnp.where(causal, sc, 0.0).astype(BF16)
            st = st_ref[h]
            o = (jnp.dot(sc, vb[:, sl], preferred_element_type=F32)
                 + lax.dot_general(qb[:, sl], st.astype(BF16), NT, preferred_element_type=F32))
            st_ref[h] = st * dec[:, sl] + lax.dot_general(vb[:, sl], kd[:, sl], TN,
                                                          preferred_element_type=F32)
            o_ref[0, rows, sl] = (_rms(o, og) * gate[:, sl]).astype(o_ref.dtype)


def hgrn2(proj_h, proj_f, lb, o_g, *, rows):
    b, s, _ = proj_h.shape
    width = B_HEADS * B_DK
    kern = functools.partial(_hgrn_kernel, nchunk=rows // CHUNK)
    spec = lambda blk: pl.BlockSpec((1, rows, width), lambda bi, r: (bi, r, blk))
    return pl.pallas_call(
        kern,
        out_shape=jax.ShapeDtypeStruct((b, s, width), BF16),
        grid=(b, s // rows),
        in_specs=[spec(PROJ_F_F), spec(PROJ_H_V), spec(PROJ_F_Q), spec(PROJ_F_GATE),
                  pl.BlockSpec((1, width), lambda bi, r: (0, 0)),
                  pl.BlockSpec((1, B_DK), lambda bi, r: (0, 0))],
        out_specs=pl.BlockSpec((1, rows, width), lambda bi, r: (bi, r, 0)),
        scratch_shapes=[pltpu.VMEM((B_HEADS, B_DK, B_DK), F32)],
        compiler_params=_params(("parallel", "arbitrary")),
        name="hgrn2",
    )(proj_f, proj_h, proj_f, proj_f, lb.reshape(1, width), o_g.reshape(1, B_DK))


def _gmlp_kernel(h_ref, g_ref, win_ref, lng_ref, lnb_ref, wsp_ref, bsp_ref, wout_ref, o_ref,
                 gated_ref, *, nchunk):
    x = h_ref[0]
    width = x.shape[1]
    xn = _rms(x, g_ref[...]).astype(BF16)
    uv = jnp.dot(xn, win_ref[...], preferred_element_type=F32)
    uv = 0.5 * uv * (1.0 + lax.erf(uv * (0.5 ** 0.5)))
    u = uv[:, :width]
    v = uv[:, width:]
    mu = jnp.mean(v, axis=-1, keepdims=True)
    vc = v - mu
    var = jnp.mean(vc * vc, axis=-1, keepdims=True)
    vn = (vc * lax.rsqrt(var + EPS) * lng_ref[...] + lnb_ref[...]).astype(BF16)
    row = lax.broadcasted_iota(I32, (GM_CHUNK, GM_CHUNK), 0)
    col = lax.broadcasted_iota(I32, (GM_CHUNK, GM_CHUNK), 1)
    gw = width // GM_GROUPS
    for gi in range(GM_GROUPS):
        wg = jnp.where(col <= row, wsp_ref[gi], 0.0).astype(BF16)
        cs = slice(gi * gw, (gi + 1) * gw)
        vg = jnp.concatenate([vn[c * GM_CHUNK:(c + 1) * GM_CHUNK, cs] for c in range(nchunk)], axis=1)
        mixed = jnp.dot(wg, vg, preferred_element_type=F32)
        for c in range(nchunk):
            rs = slice(c * GM_CHUNK, (c + 1) * GM_CHUNK)
            mc = mixed[:, c * gw:(c + 1) * gw] + bsp_ref[:, cs]
            gated_ref[rs, cs] = (u[rs, cs] * mc).astype(BF16)
    o_ref[0] = x + jnp.dot(gated_ref[...], wout_ref[...], preferred_element_type=F32)


def gmlp(h, g, w_in, ln_g, ln_b, w_sp, b_sp, w_out, *, rows):
    b, s, d = h.shape
    width = w_out.shape[0]
    gw = width // GM_GROUPS
    bsp_full = jnp.repeat(jnp.transpose(b_sp), gw, axis=1)
    kern = functools.partial(_gmlp_kernel, nchunk=rows // GM_CHUNK)
    const = lambda shape: pl.BlockSpec(shape, lambda bi, r: (0,) * len(shape),
                                       pipeline_mode=pl.Buffered(1))
    return pl.pallas_call(
        kern,
        out_shape=jax.ShapeDtypeStruct((b, s, d), F32),
        grid=(b, s // rows),
        in_specs=[pl.BlockSpec((1, rows, d), lambda bi, r: (bi, r, 0)),
                  const((1, d)), const((d, 2 * width)), const((1, width)), const((1, width)),
                  const((GM_GROUPS, GM_CHUNK, GM_CHUNK)), const((GM_CHUNK, width)),
                  const((width, d))],
        out_specs=pl.BlockSpec((1, rows, d), lambda bi, r: (bi, r, 0)),
        scratch_shapes=[pltpu.VMEM((rows, width), BF16)],
        compiler_params=_params(("parallel", "parallel")),
        name="gmlp",
    )(h, g.reshape(1, d), w_in.astype(BF16), ln_g.reshape(1, width), ln_b.reshape(1, width),
      w_sp, bsp_full, w_out.astype(BF16))


def _xattn_kernel(*refs, mixed):
    if mixed:
        h_ref, at_ref, b_ref, wa_ref, wb_ref, g_ref, wq_ref, k_ref, v_ref, wo_ref, o_ref = refs
        x = (h_ref[0] + lax.dot_general(at_ref[0], wa_ref[...], TN, preferred_element_type=F32)
             + jnp.dot(b_ref[0], wb_ref[...], preferred_element_type=F32))
    else:
        h_ref, g_ref, wq_ref, k_ref, v_ref, wo_ref, o_ref = refs
        x = h_ref[0]
    xn = _rms(x, g_ref[...]).astype(BF16)
    q = jnp.dot(xn, wq_ref[...], preferred_element_type=F32)
    q = (q * (X_HEAD_DIM ** -0.5 * LOG2E)).astype(BF16)
    ones = jnp.ones((k_ref.shape[1], X_HEAD_DIM), BF16)
    outs = []
    for h in range(X_HEADS):
        sl = slice(h * X_HEAD_DIM, (h + 1) * X_HEAD_DIM)
        s = lax.dot_general(q[:, sl], k_ref[0, :, sl], NT, preferred_element_type=F32)
        p = jnp.exp2(s - s.max(axis=-1, keepdims=True)).astype(BF16)
        ol = jnp.dot(p, jnp.concatenate([v_ref[0, :, sl], ones], axis=1),
                     preferred_element_type=F32)
        outs.append((ol[:, :X_HEAD_DIM] * (1.0 / ol[:, X_HEAD_DIM:X_HEAD_DIM + 1])).astype(BF16))
    att = jnp.concatenate(outs, axis=1)
    o_ref[0] = x + jnp.dot(att, wo_ref[...], preferred_element_type=F32)


def xattn(h, g, wq, kv, wo, *, rows, mix=None):
    b, s, d = h.shape
    m = kv.shape[1]
    xw = wq.shape[1]
    const = lambda shape: pl.BlockSpec(shape, lambda bi, r: (0,) * len(shape))
    rows_spec = lambda width: pl.BlockSpec((1, rows, width), lambda bi, r: (bi, r, 0))
    mix_specs, mix_args = [], []
    if mix is not None:
        a_t, b_out, wa, wb = mix
        mix_specs = [pl.BlockSpec((1, a_t.shape[1], rows), lambda bi, r: (bi, 0, r)),
                     rows_spec(b_out.shape[2]), const(wa.shape), const(wb.shape)]
        mix_args = [a_t, b_out, wa, wb]
    return pl.pallas_call(
        functools.partial(_xattn_kernel, mixed=mix is not None),
        out_shape=jax.ShapeDtypeStruct((b, s, d), F32),
        grid=(b, s // rows),
        in_specs=[rows_spec(d), *mix_specs,
                  const((1, d)), const((d, xw)),
                  pl.BlockSpec((1, m, xw), lambda bi, r: (bi, 0, 0)),
                  pl.BlockSpec((1, m, xw), lambda bi, r: (bi, 0, 1)),
                  const((xw, d))],
        out_specs=rows_spec(d),
        compiler_params=_params(("parallel", "parallel")),
        name="xattn",
    )(h, *mix_args, g.reshape(1, d), wq.astype(BF16), kv, kv, wo.astype(BF16))


def _swiglu_kernel(x_ref, g_ref, wgu_ref, wd_ref, fg_ref, o_ref, a_ref, *, ff, fchunk, final):
    x = x_ref[...]
    xn = _rms(x, g_ref[...]).astype(BF16)
    for c in range(ff // fchunk):
        gate = jnp.dot(xn, wgu_ref[:, c * fchunk:(c + 1) * fchunk], preferred_element_type=F32)
        up = jnp.dot(xn, wgu_ref[:, ff + c * fchunk:ff + (c + 1) * fchunk],
                     preferred_element_type=F32)
        a_ref[:, c * fchunk:(c + 1) * fchunk] = (gate * jax.nn.sigmoid(gate) * up).astype(BF16)
    y = x + jnp.dot(a_ref[...], wd_ref[...], preferred_element_type=F32)
    if final:
        y = _rms(y, fg_ref[...])
    o_ref[...] = y


def swiglu(x, g, w_gu, w_down, final_g, *, tm, final):
    n, d = x.shape
    ff = w_down.shape[0]
    kern = functools.partial(_swiglu_kernel, ff=ff, fchunk=256, final=final)
    once = dict(pipeline_mode=pl.Buffered(1))
    return pl.pallas_call(
        kern,
        out_shape=jax.ShapeDtypeStruct((n, d), F32),
        grid=(n // tm,),
        in_specs=[pl.BlockSpec((tm, d), lambda i: (i, 0)),
                  pl.BlockSpec((1, d), lambda i: (0, 0)),
                  pl.BlockSpec((d, 2 * ff), lambda i: (0, 0), **once),
                  pl.BlockSpec((ff, d), lambda i: (0, 0), **once),
                  pl.BlockSpec((1, d), lambda i: (0, 0))],
        out_specs=pl.BlockSpec((tm, d), lambda i: (i, 0)),
        scratch_shapes=[pltpu.VMEM((tm, ff), BF16)],
        compiler_params=_params(("parallel",)),
        name="swiglu",
    )(x, g.reshape(1, d), w_gu.astype(BF16), w_down.astype(BF16), final_g.reshape(1, d))


def _pack_in_proj(w):
    sizes = [A_HEADS * A_HEAD_DIM, A_LATENT, IDX_HEADS * IDX_DIM, IDX_DIM, IDX_HEADS,
             B_HEADS * B_DK, B_HEADS * B_DK, B_HEADS * B_DK, B_HEADS * B_DK]
    offs = [0]
    for sz in sizes:
        offs.append(offs[-1] + sz)
    q_a, c, qi, ki, wi, f_b, i_b, q_b, g_b = [w[:, offs[n]:offs[n + 1]] for n in range(9)]
    z = lambda n: jnp.zeros((w.shape[0], n), w.dtype)
    return jnp.concatenate([q_a, qi, i_b, ki, z(LANE - IDX_DIM), z(LANE - IDX_DIM), ki,
                            f_b, q_b, g_b, c, wi, z(LANE - IDX_HEADS)], axis=1)


def kernel(x, mem, rel_bias, hgrn_lb, mix_norm, e_w_in, e_lat_norm, e_w_uk, e_w_uv, e_o_norm, e_w_out, o_w_in, o_ln_g, o_ln_b, o_w_sp, o_b_sp, o_w_out, x_norm, mem_norm, x_wq, x_wkv, x_wo, f_norm, f_w_gu, f_w_down, final_norm):
    b, s, d = x.shape
    m = mem.shape[1]
    n = b * s
    depth = mix_norm.shape[0]
    topk = min(TOPK_MAX, s // 4)
    lb_all = jnp.cumsum(jax.nn.softmax(hgrn_lb.astype(F32), axis=0), axis=0)
    bias = bias_tiles(rel_bias)
    mem2 = mem.reshape(b * m, d)

    h = x.reshape(n, d)
    for l in range(depth):
        j = l // 2
        mix = None
        if l % 2 == 0:
            proj_h, proj_f = norm_matmul(h, mix_norm[l], _pack_in_proj(e_w_in[j]).astype(BF16),
                                         tm=1024, tn=256,
                                         groups=((PROJ_H_COLS, BF16), (PROJ_F_COLS, F32)))
            proj_h = proj_h.reshape(b, s, -1)
            proj_f = proj_f.reshape(b, s, -1)
            a_out = dsa_attention(proj_h, proj_f, e_lat_norm[j], e_w_uk[j], e_w_uv[j], bias,
                                  topk=topk)
            b_out = hgrn2(proj_h, proj_f, lb_all[l], e_o_norm[j], rows=512)
            wa = e_w_out[j][:A_HEADS * A_HEAD_DIM].astype(BF16)
            wb = e_w_out[j][A_HEADS * A_HEAD_DIM:].astype(BF16)
            mix = (a_out, b_out, wa, wb)
        else:
            h = gmlp(h.reshape(b, s, d), mix_norm[l], o_w_in[j], o_ln_g[j], o_ln_b[j], o_w_sp[j],
                     o_b_sp[j], o_w_out[j], rows=1024).reshape(n, d)
        (kv,) = norm_matmul(mem2, mem_norm[l], x_wkv[l].astype(BF16), tm=512, tn=512,
                            groups=((x_wkv.shape[2], BF16),))
        kv = kv.reshape(b, m, -1)
        h = xattn(h.reshape(b, s, d), x_norm[l], x_wq[l], kv, x_wo[l], rows=1024,
                  mix=mix).reshape(n, d)
        h = swiglu(h, f_norm[l], f_w_gu[l], f_w_down[l], final_norm, tm=1024,
                   final=(l == depth - 1))
    return h.reshape(b, s, d)
```

```python
import functools

import jax
import jax.numpy as jnp
from jax import lax
from jax.experimental import pallas as pl
from jax.experimental.pallas import tpu as pltpu

F32 = jnp.float32
BF16 = jnp.bfloat16
I32 = jnp.int32

EPS = 1e-6
CHUNK = 64
LANE = 128
KT = 256
QB = 256
A_HEADS = 8
A_HEAD_DIM = 64
A_LATENT = 128
IDX_HEADS = 8
IDX_DIM = 64
TOPK_MAX = 256
REL_BUCKETS = 32
B_HEADS = 4
B_DK = 128
GM_CHUNK = 128
GM_GROUPS = 8
X_HEADS = 4
X_HEAD_DIM = 128
NEG = -1e30
INT_MIN = -2**31
LOG2E = 1.4426950408889634
ACC_ROWS = A_LATENT + 16
WEIGHT_SUM_FLOOR = 2.0 ** -60
PROJ_H_COLS = PROJ_F_COLS = 1792
PROJ_H_QA, PROJ_H_QI, PROJ_H_V = 0, 1, 2
PROJ_H_KI_LO, PROJ_H_KI_HI = 12, 13
PROJ_F_F, PROJ_F_Q, PROJ_F_GATE = 0, 1, 2
PROJ_F_C, PROJ_F_WI = 12, 13

VMEM_LIMIT = 56 * 1024 * 1024

NT = (((1,), (1,)), ((), ()))
TN = (((0,), (0,)), ((), ()))


def _rms(x, g):
    ms = jnp.mean(x * x, axis=-1, keepdims=True)
    return x * lax.rsqrt(ms + EPS) * g


def _params(sem, vmem=VMEM_LIMIT):
    return pltpu.CompilerParams(dimension_semantics=sem, vmem_limit_bytes=vmem)


def _norm_matmul_kernel(x_ref, g_ref, w_ref, *o_refs, tn):
    xn = _rms(x_ref[...], g_ref[...]).astype(BF16)
    base = 0
    for o_ref in o_refs:
        for c in range(o_ref.shape[1] // tn):
            y = jnp.dot(xn, w_ref[:, base + c * tn:base + (c + 1) * tn], preferred_element_type=F32)
            o_ref[:, c * tn:(c + 1) * tn] = y.astype(o_ref.dtype)
        base += o_ref.shape[1]


def norm_matmul(x, g, w, *, tm, tn, groups):
    n, k = x.shape
    nout = w.shape[1]
    assert sum(nc for nc, _ in groups) == nout and all(nc % tn == 0 for nc, _ in groups)
    return pl.pallas_call(
        functools.partial(_norm_matmul_kernel, tn=tn),
        out_shape=tuple(jax.ShapeDtypeStruct((n, nc), dt) for nc, dt in groups),
        grid=(n // tm,),
        in_specs=[
            pl.BlockSpec((tm, k), lambda i: (i, 0)),
            pl.BlockSpec((1, k), lambda i: (0, 0)),
            pl.BlockSpec((k, nout), lambda i: (0, 0), pipeline_mode=pl.Buffered(1)),
        ],
        out_specs=tuple(pl.BlockSpec((tm, nc), lambda i: (i, 0)) for nc, _ in groups),
        compiler_params=_params(("parallel",)),
        name="norm_matmul",
    )(x, g.reshape(1, k), w)


def _bias_tiles_kernel(rb_ref, o_ref):
    s = lax.broadcasted_iota(I32, (KT, QB), 0)
    t = lax.broadcasted_iota(I32, (KT, QB), 1)
    nb = REL_BUCKETS // 2
    max_exact = nb // 2
    for kind in range(3):
        rel = s - t - KT * kind
        n = jnp.abs(rel)
        n2 = n * n
        large = jnp.full((KT, QB), max_exact, I32)
        for j in range(1, nb - max_exact):
            large = large + jnp.where(n2 >= (max_exact * max_exact) * (2 ** j), 1, 0)
        bucket = jnp.where(rel > 0, nb, 0) + jnp.where(n < max_exact, n, large)
        buckets = (range(REL_BUCKETS), range(nb), (nb - 1,))[kind]
        hits = [(b, bucket == b) for b in buckets]
        for h in range(A_HEADS):
            acc = jnp.zeros((KT, QB), F32)
            for b, hit in hits:
                acc = jnp.where(hit, rb_ref[b, h] * LOG2E, acc)
            o_ref[kind, :, h * QB:(h + 1) * QB] = acc


def bias_tiles(rel_bias):
    return pl.pallas_call(
        _bias_tiles_kernel,
        out_shape=jax.ShapeDtypeStruct((3, KT, A_HEADS * QB), F32),
        in_specs=[pl.BlockSpec(memory_space=pltpu.SMEM)],
        out_specs=pl.BlockSpec(memory_space=pltpu.VMEM),
        name="bias_tiles",
    )(rel_bias)


def _dsa_kernel(qa_ref, qi_ref, wi_ref, c_ref, ka_ref, kb_ref, latg_ref, wuk_ref, wuvt_ref, bias_ref,
                o_ref, cn_ref, ct_ref, keys_ref, hi_ref, s_ref, acc_ref,
                d0_ref, d1_ref, r0_ref, r1_ref, p0_ref, p1_ref, mx_ref, re_ref, *, topk, ntiles):
    i = pl.program_id(1)
    hq = A_HEADS * QB
    nt = i + 1
    last = i

    @pl.when(i == 0)
    def _prep():
        cn = _rms(c_ref[0], latg_ref[...])
        cn_ref[...] = cn.astype(BF16)
        ones = jnp.ones((ACC_ROWS - A_LATENT, KT), BF16)
        for j in range(ntiles):
            ct_ref[j, :A_LATENT, :] = cn[j * KT:(j + 1) * KT, :].T.astype(BF16)
            ct_ref[j, A_LATENT:, :] = ones

    qa = qa_ref[0]
    ql = jnp.concatenate(
        [jnp.dot(qa[:, (h // 2) * LANE:(h // 2 + 1) * LANE], wuk_ref[h],
                 preferred_element_type=F32) for h in range(A_HEADS)], axis=0)
    qlb = (ql * (A_HEAD_DIM ** -0.5 * LOG2E)).astype(BF16)
    qi = qi_ref[0]
    qp = jnp.concatenate([qi[:, p * LANE:(p + 1) * LANE] for p in range(IDX_HEADS // 2)], axis=0)
    w = wi_ref[0] * (IDX_HEADS ** -0.5) * (IDX_DIM ** -0.5)
    wt = w.T

    row = lax.broadcasted_iota(I32, (KT, QB), 0)
    col = lax.broadcasted_iota(I32, (KT, QB), 1)
    chunk_bits = CHUNK.bit_length() - 1
    inadm_last = jnp.where(lax.shift_right_logical(row, chunk_bits)
                           > lax.shift_right_logical(col, chunk_bits), 1, 0)

    def inadmissible(j):
        return (inadm_last * jnp.where(j == last, 1, 0)) != 0

    def rows(j):
        return pl.ds(pl.multiple_of(j * KT, KT), KT)

    def pipelined(prepare, consume, carry):
        prepare(0, 0)

        def pair(t, carry):
            j = 2 * t
            prepare(j + 1, 1)
            carry = consume(j, 0, carry)
            prepare(jnp.minimum(j + 2, last), 0)
            return consume(j + 1, 1, carry)

        carry = lax.fori_loop(0, lax.shift_right_logical(nt, 1), pair, carry)
        return lax.cond((nt & 1) == 1, lambda c: consume(last, 0, c), lambda c: c, carry)

    d_refs = (d0_ref, d1_ref)

    def dots(j, slot):
        ks = rows(j)
        d_ref = d_refs[slot]
        kk = jnp.concatenate([ka_ref[0, ks, :], kb_ref[0, ks, :]], axis=0)
        d_ref[...] = lax.dot_general(kk, qp, NT, preferred_element_type=F32)
        s = lax.dot_general(cn_ref[ks, :], qlb, NT, preferred_element_type=F32)
        s = s + bias_ref[jnp.minimum(i - j, 2)]
        s_ref[j] = s
        mx_ref[j] = s.reshape(KT // 8, 8, hq).max(axis=0)

    def score_keys(j, slot, carry):
        d_ref = d_refs[slot]
        sc = jnp.zeros((KT, QB), F32)
        for h in range(IDX_HEADS):
            d = d_ref[(h % 2) * KT:(h % 2 + 1) * KT, (h // 2) * QB:(h // 2 + 1) * QB]
            sc = sc + jnp.maximum(d, 0.0) * wt[h:h + 1, :]
        sc = jnp.where(inadmissible(j), -jnp.inf, sc)
        bits = pltpu.bitcast(sc, I32)
        key = jnp.where(bits < 0, bits ^ 0x7FFFFFFF, bits)
        key = jnp.where(bits == INT_MIN, 0, key)
        keys_ref[j] = key
        hi_ref[j] = lax.shift_right_arithmetic(key, 16).astype(jnp.int16)
        return carry

    pipelined(dots, score_keys, 0)

    half_min = -2 ** 15

    @pl.when(nt < ntiles)
    def _pad():
        hi_ref[nt] = jnp.full((KT, QB), half_min, jnp.int16)

    def select(cap):
        def count16(pred):
            acc = jnp.zeros((16, QB), jnp.int16)
            for j in range(cap):
                hit = jnp.where(pred(hi_ref[j]), jnp.int16(1), jnp.int16(0))
                parts = [hit[r * 16:(r + 1) * 16, :] for r in range(KT // 16)]
                while len(parts) > 1:
                    parts = [a + b for a, b in zip(parts[::2], parts[1::2])]
                acc = acc + parts[0]
            return acc.astype(I32).sum(axis=0, keepdims=True)

        def bisect16(target):
            def step_fn(step, lo):
                trial = lo + lax.shift_left(jnp.int32(1), 15 - step)
                t16 = trial.astype(jnp.int16)
                return jnp.where(count16(lambda k: k >= t16) >= target, trial, lo)
            return lax.fori_loop(0, 16, step_fn, jnp.full((1, QB), half_min, I32))

        thr_hi = bisect16(topk)

        def low_halves(j, acc):
            key = keys_ref[j]
            hi = lax.shift_right_arithmetic(key, 16)
            lo = (key & 0xFFFF) + half_min
            hi_ref[j] = jnp.where(hi == thr_hi, lo, half_min).astype(jnp.int16)
            above = jnp.where(hi > thr_hi, 1, 0)
            return acc + above.reshape(KT // 8, 8, QB).sum(axis=0)

        above_hi = lax.fori_loop(0, nt, low_halves, jnp.zeros((8, QB), I32))
        above_hi = above_hi.sum(axis=0, keepdims=True)
        thr_lo = bisect16(topk - above_hi)
        thr_lo16 = thr_lo.astype(jnp.int16)
        above = above_hi + count16(lambda k: k > thr_lo16)
        return lax.shift_left(thr_hi, 16) + (thr_lo - half_min), (topk - above).astype(F32)

    caps = sorted(set(min(c, ntiles) for c in range(2, ntiles + 2, 2)))
    thr, need = lax.switch(lax.shift_right_logical(nt - 1, 1),
                           [functools.partial(select, c) for c in caps])

    trow = lax.broadcasted_iota(I32, (KT, KT), 0)
    tcol = lax.broadcasted_iota(I32, (KT, KT), 1)
    tri = jnp.where(tcol < trow, 1.0, 0.0).astype(BF16)

    r_refs = (r0_ref, r1_ref)
    p_refs = (p0_ref, p1_ref)

    def selected(j, key, rank):
        sel = (key > thr) | ((key == thr) & (rank < need))
        return sel & jnp.logical_not(inadmissible(j))

    def weighted(j, slot, carry):
        acc_ref[...] += jnp.dot(ct_ref[j], p_refs[slot][...], preferred_element_type=F32)
        return carry

    def ties_before(j, run_eq):
        re_ref[j] = jnp.broadcast_to(run_eq, (8, QB))
        return run_eq + jnp.where(keys_ref[j] == thr, 1.0, 0.0).sum(axis=0, keepdims=True)

    lax.fori_loop(0, nt, ties_before, jnp.zeros((1, QB), F32))
    m_all = lax.fori_loop(0, nt, lambda j, mx: jnp.maximum(mx, mx_ref[j]),
                          jnp.full((8, hq), NEG, F32)).max(axis=0, keepdims=True)

    def fast_weights(j, slot):
        key = keys_ref[j]
        eqf = jnp.where(key == thr, 1.0, 0.0).astype(BF16)
        rank = jnp.dot(tri, eqf, preferred_element_type=F32) + re_ref[j, 0:1, :]
        keep = jnp.where(selected(j, key, rank), 1.0, 0.0).astype(BF16)
        for h in range(A_HEADS):
            cols = slice(h * QB, (h + 1) * QB)
            p = jnp.exp2(s_ref[j, :, cols] - m_all[:, cols]).astype(BF16)
            p_refs[slot][:, cols] = p * keep

    acc_ref[...] = jnp.zeros(acc_ref.shape, F32)
    pipelined(fast_weights, weighted, 0)

    @pl.when(jnp.min(acc_ref[A_LATENT:A_LATENT + 1, :]) < WEIGHT_SUM_FLOOR)
    def _exact_shift():
        def tie_ranks(j, slot):
            eqf = jnp.where(keys_ref[j] == thr, 1.0, 0.0).astype(BF16)
            r_refs[slot][...] = jnp.dot(tri, eqf, preferred_element_type=F32)

        def mask_tile(j, slot, m8):
            rank = r_refs[slot][...] + re_ref[j, 0:1, :]
            negm = jnp.where(selected(j, keys_ref[j], rank), 0.0, NEG)
            m8_new = []
            for h in range(A_HEADS):
                cols = slice(h * QB, (h + 1) * QB)
                blk = s_ref[j, :, cols] + negm
                s_ref[j, :, cols] = blk
                m8_new.append(jnp.maximum(m8[:, cols], blk.reshape(KT // 8, 8, QB).max(axis=0)))
            return jnp.concatenate(m8_new, axis=1)

        m8 = pipelined(tie_ranks, mask_tile, jnp.full((8, hq), NEG, F32))
        m = m8.max(axis=0, keepdims=True)

        def weights(j, slot):
            p_refs[slot][...] = jnp.exp2(s_ref[j] - m).astype(BF16)

        acc_ref[...] = jnp.zeros(acc_ref.shape, F32)
        pipelined(weights, weighted, 0)

    inv_l = 1.0 / acc_ref[A_LATENT:A_LATENT + 1, :]
    ot = (acc_ref[:A_LATENT, :] * inv_l).astype(BF16)
    for h in range(A_HEADS):
        o_ref[0, h * A_HEAD_DIM:(h + 1) * A_HEAD_DIM, :] = jnp.dot(
            wuvt_ref[h], ot[:, h * QB:(h + 1) * QB], preferred_element_type=F32
        ).astype(o_ref.dtype)


def dsa_attention(proj_h, proj_f, lat_g, w_uk, w_uv, bias, *, topk):
    b, s, _ = proj_h.shape
    ntiles = s // KT
    kern = functools.partial(_dsa_kernel, topk=topk, ntiles=ntiles)
    wuk_t = jnp.swapaxes(w_uk, 1, 2)
    zero = jnp.zeros_like(wuk_t)
    odd = (jnp.arange(A_HEADS) % 2 == 1)[:, None, None]
    wuk_pad = jnp.concatenate([jnp.where(odd, zero, wuk_t), jnp.where(odd, wuk_t, zero)], axis=1)
    col = lambda blk: (lambda bi, i: (bi, 0, blk))
    const = lambda shape: pl.BlockSpec(shape, lambda bi, i: (0,) * len(shape))
    return pl.pallas_call(
        kern,
        out_shape=jax.ShapeDtypeStruct((b, A_HEADS * A_HEAD_DIM, s), BF16),
        grid=(b, s // QB),
        in_specs=[
            pl.BlockSpec((1, QB, 512), lambda bi, i: (bi, i, PROJ_H_QA)),
            pl.BlockSpec((1, QB, 512), lambda bi, i: (bi, i, PROJ_H_QI)),
            pl.BlockSpec((1, QB, LANE), lambda bi, i: (bi, i, PROJ_F_WI)),
            pl.BlockSpec((1, s, LANE), col(PROJ_F_C)),
            pl.BlockSpec((1, s, LANE), col(PROJ_H_KI_LO)),
            pl.BlockSpec((1, s, LANE), col(PROJ_H_KI_HI)),
            const((1, A_LATENT)),
            const((A_HEADS, LANE, A_LATENT)),
            const((A_HEADS, A_HEAD_DIM, A_LATENT)),
            pl.BlockSpec((3, KT, A_HEADS * QB), lambda bi, i: (0, 0, 0),
                         pipeline_mode=pl.Buffered(1)),
        ],
        out_specs=pl.BlockSpec((1, A_HEADS * A_HEAD_DIM, QB), lambda bi, i: (bi, 0, i)),
        scratch_shapes=[
            pltpu.VMEM((s, A_LATENT), BF16),
            pltpu.VMEM((ntiles, ACC_ROWS, KT), BF16),
            pltpu.VMEM((ntiles, KT, QB), I32),
            pltpu.VMEM((ntiles, KT, QB), jnp.int16),
            pltpu.VMEM((ntiles, KT, A_HEADS * QB), F32),
            pltpu.VMEM((ACC_ROWS, A_HEADS * QB), F32),
            pltpu.VMEM((2 * KT, (IDX_HEADS // 2) * QB), F32),
            pltpu.VMEM((2 * KT, (IDX_HEADS // 2) * QB), F32),
            pltpu.VMEM((KT, QB), F32),
            pltpu.VMEM((KT, QB), F32),
            pltpu.VMEM((KT, A_HEADS * QB), BF16),
            pltpu.VMEM((KT, A_HEADS * QB), BF16),
            pltpu.VMEM((ntiles, 8, A_HEADS * QB), F32),
            pltpu.VMEM((ntiles, 8, QB), F32),
        ],
        compiler_params=_params(("parallel", "arbitrary")),
        name="dsa_attention",
    )(proj_h, proj_h, proj_f, proj_f, proj_h, proj_h, lat_g.reshape(1, A_LATENT),
      wuk_pad.astype(BF16), jnp.swapaxes(w_uv, 1, 2).astype(BF16), bias)


def _hgrn_kernel(f_ref, v_ref, q_ref, g_ref, lb_ref, og_ref, o_ref, st_ref, *, nchunk):
    @pl.when(pl.program_id(1) == 0)
    def _():
        st_ref[...] = jnp.zeros(st_ref.shape, F32)

    lb = lb_ref[...]
    og = og_ref[...]
    row = lax.broadcasted_iota(I32, (CHUNK, CHUNK), 0)
    col = lax.broadcasted_iota(I32, (CHUNK, CHUNK), 1)
    causal = col <= row
    tri = jnp.where(causal, 1.0, 0.0).astype(BF16)
    hc = CHUNK // 2
    top = lax.broadcasted_iota(I32, (CHUNK, B_HEADS * B_DK), 0) < hc
    zeros_h = jnp.zeros((hc, B_DK), BF16)

    for c in range(nchunk):
        rows = slice(c * CHUNK, (c + 1) * CHUNK)
        f = lb + (1.0 - lb) * jax.nn.sigmoid(f_ref[0, rows, :])
        g = jnp.log(f)
        k = 1.0 - f
        q = q_ref[0, rows, :]
        qf = q * jax.nn.sigmoid(q)
        vb = v_ref[0, rows, :]
        g_hi = g.astype(BF16)
        g_lo = (g - g_hi.astype(F32)).astype(BF16)
        bc = (jnp.dot(tri, g_hi, preferred_element_type=F32)
              + jnp.dot(tri, g_lo, preferred_element_type=F32))
        b_end = bc[CHUNK - 1:CHUNK, :]
        b_half = bc[hc - 1:hc, :]
        e_in = bc - jnp.where(top, bc[hc // 2 - 1:hc // 2, :], bc[hc + hc // 2 - 1:hc + hc // 2, :])
        q_in = (qf * jnp.exp(e_in)).astype(BF16)
        k_in = (k * jnp.exp(-e_in)).astype(BF16)
        q_x = (qf[hc:, :] * jnp.exp(bc[hc:, :] - b_half)).astype(BF16)
        k_x = (k[:hc, :] * jnp.exp(b_half - bc[:hc, :])).astype(BF16)
        qb = (qf * jnp.exp(bc)).astype(BF16)
        kd = (k * jnp.exp(b_end - bc)).astype(BF16)
        dec = jnp.exp(b_end)
        gate = g_ref[0, rows, :]
        gate = gate * jax.nn.sigmoid(gate)
        for h in range(B_HEADS):
            sl = slice(h * B_DK, (h + 1) * B_DK)
            q3 = jnp.concatenate([
                jnp.concatenate([q_in[:hc, sl], zeros_h, zeros_h], axis=1),
                jnp.concatenate([zeros_h, q_x[:, sl], q_in[hc:, sl]], axis=1)], axis=0)
            k3 = jnp.concatenate([
                jnp.concatenate([k_in[:hc, sl], k_x[:, sl], zeros_h], axis=1),
                jnp.concatenate([zeros_h, zeros_h, k_in[hc:, sl]], axis=1)], axis=0)
            sc = lax.dot_general(q3, k3, NT, preferred_element_type=F32)
            sc = jnp.where(causal, sc, 0.0).astype(BF16)
            st = st_ref[h]
            o = (jnp.dot(sc, vb[:, sl], preferred_element_type=F32)
                 + lax.dot_general(qb[:, sl], st.astype(BF16), NT, preferred_element_type=F32))
            st_ref[h] = st * dec[:, sl] + lax.dot_general(vb[:, sl], kd[:, sl], TN,
                                                          preferred_element_type=F32)
            o_ref[0, rows, sl] = (_rms(o, og) * gate[:, sl]).astype(o_ref.dtype)


def hgrn2(proj_h, proj_f, lb, o_g, *, rows):
    b, s, _ = proj_h.shape
    width = B_HEADS * B_DK
    kern = functools.partial(_hgrn_kernel, nchunk=rows // CHUNK)
    spec = lambda blk: pl.BlockSpec((1, rows, width), lambda bi, r: (bi, r, blk))
    return pl.pallas_call(
        kern,
        out_shape=jax.ShapeDtypeStruct((b, s, width), BF16),
        grid=(b, s // rows),
        in_specs=[spec(PROJ_F_F), spec(PROJ_H_V), spec(PROJ_F_Q), spec(PROJ_F_GATE),
                  pl.BlockSpec((1, width), lambda bi, r: (0, 0)),
                  pl.BlockSpec((1, B_DK), lambda bi, r: (0, 0))],
        out_specs=pl.BlockSpec((1, rows, width), lambda bi, r: (bi, r, 0)),
        scratch_shapes=[pltpu.VMEM((B_HEADS, B_DK, B_DK), F32)],
        compiler_params=_params(("parallel", "arbitrary")),
        name="hgrn2",
    )(proj_f, proj_h, proj_f, proj_f, lb.reshape(1, width), o_g.reshape(1, B_DK))


def _gmlp_kernel(h_ref, g_ref, win_ref, lng_ref, lnb_ref, wsp_ref, bsp_ref, wout_ref, o_ref,
                 gated_ref, *, nchunk, nsplit):
    width = h_ref.shape[2]
    gw = width // GM_GROUPS
    row = lax.broadcasted_iota(I32, (GM_CHUNK, GM_CHUNK), 0)
    col = lax.broadcasted_iota(I32, (GM_CHUNK, GM_CHUNK), 1)
    wgs = [jnp.where(col <= row, wsp_ref[gi], 0.0).astype(BF16) for gi in range(GM_GROUPS)]
    nc = nchunk // nsplit
    xs = [h_ref[0, sp * nc * GM_CHUNK:(sp + 1) * nc * GM_CHUNK, :] for sp in range(nsplit)]
    uvs = [jnp.dot(_rms(x, g_ref[...]).astype(BF16), win_ref[...], preferred_element_type=F32)
           for x in xs]
    for sp in range(nsplit):
        r0 = sp * nc * GM_CHUNK
        x, uv = xs[sp], uvs[sp]
        uv = 0.5 * uv * (1.0 + lax.erf(uv * (0.5 ** 0.5)))
        u = uv[:, :width]
        v = uv[:, width:]
        mu = jnp.mean(v, axis=-1, keepdims=True)
        vc = v - mu
        var = jnp.mean(vc * vc, axis=-1, keepdims=True)
        vn = (vc * lax.rsqrt(var + EPS) * lng_ref[...] + lnb_ref[...]).astype(BF16)
        for gi in range(GM_GROUPS):
            cs = slice(gi * gw, (gi + 1) * gw)
            vg = jnp.concatenate([vn[c * GM_CHUNK:(c + 1) * GM_CHUNK, cs] for c in range(nc)], axis=1)
            mixed = jnp.dot(wgs[gi], vg, preferred_element_type=F32)
            for c in range(nc):
                rs = slice(c * GM_CHUNK, (c + 1) * GM_CHUNK)
                mc = mixed[:, c * gw:(c + 1) * gw] + bsp_ref[:, cs]
                gated_ref[r0 + c * GM_CHUNK:r0 + (c + 1) * GM_CHUNK, cs] = (u[rs, cs] * mc).astype(BF16)
        o_ref[0, r0:r0 + nc * GM_CHUNK, :] = x + jnp.dot(
            gated_ref[r0:r0 + nc * GM_CHUNK, :], wout_ref[...], preferred_element_type=F32)


def gmlp(h, g, w_in, ln_g, ln_b, w_sp, b_sp, w_out, *, rows):
    b, s, d = h.shape
    width = w_out.shape[0]
    gw = width // GM_GROUPS
    bsp_full = jnp.repeat(jnp.transpose(b_sp), gw, axis=1)
    kern = functools.partial(_gmlp_kernel, nchunk=rows // GM_CHUNK, nsplit=4)
    const = lambda shape: pl.BlockSpec(shape, lambda bi, r: (0,) * len(shape),
                                       pipeline_mode=pl.Buffered(1))
    return pl.pallas_call(
        kern,
        out_shape=jax.ShapeDtypeStruct((b, s, d), F32),
        grid=(b, s // rows),
        in_specs=[pl.BlockSpec((1, rows, d), lambda bi, r: (bi, r, 0)),
                  const((1, d)), const((d, 2 * width)), const((1, width)), const((1, width)),
                  const((GM_GROUPS, GM_CHUNK, GM_CHUNK)), const((GM_CHUNK, width)),
                  const((width, d))],
        out_specs=pl.BlockSpec((1, rows, d), lambda bi, r: (bi, r, 0)),
        scratch_shapes=[pltpu.VMEM((rows, width), BF16)],
        compiler_params=_params(("parallel", "parallel")),
        name="gmlp",
    )(h, g.reshape(1, d), w_in.astype(BF16), ln_g.reshape(1, width), ln_b.reshape(1, width),
      w_sp, bsp_full, w_out.astype(BF16))


def _xattn_kernel(*refs, mixed):
    if mixed:
        h_ref, at_ref, b_ref, wa_ref, wb_ref, g_ref, wq_ref, k_ref, v_ref, wo_ref, o_ref = refs
        x = (h_ref[0] + lax.dot_general(at_ref[0], wa_ref[...], TN, preferred_element_type=F32)
             + jnp.dot(b_ref[0], wb_ref[...], preferred_element_type=F32))
    else:
        h_ref, g_ref, wq_ref, k_ref, v_ref, wo_ref, o_ref = refs
        x = h_ref[0]
    xn = _rms(x, g_ref[...]).astype(BF16)
    q = jnp.dot(xn, wq_ref[...], preferred_element_type=F32)
    q = (q * (X_HEAD_DIM ** -0.5 * LOG2E)).astype(BF16)
    ones = jnp.ones((k_ref.shape[1], X_HEAD_DIM), BF16)
    outs = []
    for h in range(X_HEADS):
        sl = slice(h * X_HEAD_DIM, (h + 1) * X_HEAD_DIM)
        s = lax.dot_general(q[:, sl], k_ref[0, :, sl], NT, preferred_element_type=F32)
        p = jnp.exp2(s - s.max(axis=-1, keepdims=True)).astype(BF16)
        ol = jnp.dot(p, jnp.concatenate([v_ref[0, :, sl], ones], axis=1),
                     preferred_element_type=F32)
        outs.append((ol[:, :X_HEAD_DIM] * (1.0 / ol[:, X_HEAD_DIM:X_HEAD_DIM + 1])).astype(BF16))
    att = jnp.concatenate(outs, axis=1)
    o_ref[0] = x + jnp.dot(att, wo_ref[...], preferred_element_type=F32)


def xattn(h, g, wq, kv, wo, *, rows, mix=None):
    b, s, d = h.shape
    m = kv.shape[1]
    xw = wq.shape[1]
    const = lambda shape: pl.BlockSpec(shape, lambda bi, r: (0,) * len(shape))
    rows_spec = lambda width: pl.BlockSpec((1, rows, width), lambda bi, r: (bi, r, 0))
    mix_specs, mix_args = [], []
    if mix is not None:
        a_t, b_out, wa, wb = mix
        mix_specs = [pl.BlockSpec((1, a_t.shape[1], rows), lambda bi, r: (bi, 0, r)),
                     rows_spec(b_out.shape[2]), const(wa.shape), const(wb.shape)]
        mix_args = [a_t, b_out, wa, wb]
    return pl.pallas_call(
        functools.partial(_xattn_kernel, mixed=mix is not None),
        out_shape=jax.ShapeDtypeStruct((b, s, d), F32),
        grid=(b, s // rows),
        in_specs=[rows_spec(d), *mix_specs,
                  const((1, d)), const((d, xw)),
                  pl.BlockSpec((1, m, xw), lambda bi, r: (bi, 0, 0)),
                  pl.BlockSpec((1, m, xw), lambda bi, r: (bi, 0, 1)),
                  const((xw, d))],
        out_specs=rows_spec(d),
        compiler_params=_params(("parallel", "parallel")),
        name="xattn",
    )(h, *mix_args, g.reshape(1, d), wq.astype(BF16), kv, kv, wo.astype(BF16))


def _swiglu_kernel(x_ref, g_ref, wgu_ref, wd_ref, fg_ref, o_ref, a_ref, *, ff, fchunk, final):
    x = x_ref[...]
    xn = _rms(x, g_ref[...]).astype(BF16)
    for c in range(ff // fchunk):
        gate = jnp.dot(xn, wgu_ref[:, c * fchunk:(c + 1) * fchunk], preferred_element_type=F32)
        up = jnp.dot(xn, wgu_ref[:, ff + c * fchunk:ff + (c + 1) * fchunk],
                     preferred_element_type=F32)
        a_ref[:, c * fchunk:(c + 1) * fchunk] = (gate * jax.nn.sigmoid(gate) * up).astype(BF16)
    y = x + jnp.dot(a_ref[...], wd_ref[...], preferred_element_type=F32)
    if final:
        y = _rms(y, fg_ref[...])
    o_ref[...] = y


def swiglu(x, g, w_gu, w_down, final_g, *, tm, final):
    n, d = x.shape
    ff = w_down.shape[0]
    kern = functools.partial(_swiglu_kernel, ff=ff, fchunk=256, final=final)
    once = dict(pipeline_mode=pl.Buffered(1))
    return pl.pallas_call(
        kern,
        out_shape=jax.ShapeDtypeStruct((n, d), F32),
        grid=(n // tm,),
        in_specs=[pl.BlockSpec((tm, d), lambda i: (i, 0)),
                  pl.BlockSpec((1, d), lambda i: (0, 0)),
                  pl.BlockSpec((d, 2 * ff), lambda i: (0, 0), **once),
                  pl.BlockSpec((ff, d), lambda i: (0, 0), **once),
                  pl.BlockSpec((1, d), lambda i: (0, 0))],
        out_specs=pl.BlockSpec((tm, d), lambda i: (i, 0)),
        scratch_shapes=[pltpu.VMEM((tm, ff), BF16)],
        compiler_params=_params(("parallel",)),
        name="swiglu",
    )(x, g.reshape(1, d), w_gu.astype(BF16), w_down.astype(BF16), final_g.reshape(1, d))


def _pack_in_proj(w):
    sizes = [A_HEADS * A_HEAD_DIM, A_LATENT, IDX_HEADS * IDX_DIM, IDX_DIM, IDX_HEADS,
             B_HEADS * B_DK, B_HEADS * B_DK, B_HEADS * B_DK, B_HEADS * B_DK]
    offs = [0]
    for sz in sizes:
        offs.append(offs[-1] + sz)
    q_a, c, qi, ki, wi, f_b, i_b, q_b, g_b = [w[:, offs[n]:offs[n + 1]] for n in range(9)]
    z = lambda n: jnp.zeros((w.shape[0], n), w.dtype)
    return jnp.concatenate([q_a, qi, i_b, ki, z(LANE - IDX_DIM), z(LANE - IDX_DIM), ki,
                            f_b, q_b, g_b, c, wi, z(LANE - IDX_HEADS)], axis=1)


def kernel(x, mem, rel_bias, hgrn_lb, mix_norm, e_w_in, e_lat_norm, e_w_uk, e_w_uv, e_o_norm, e_w_out, o_w_in, o_ln_g, o_ln_b, o_w_sp, o_b_sp, o_w_out, x_norm, mem_norm, x_wq, x_wkv, x_wo, f_norm, f_w_gu, f_w_down, final_norm):
    b, s, d = x.shape
    m = mem.shape[1]
    n = b * s
    depth = mix_norm.shape[0]
    topk = min(TOPK_MAX, s // 4)
    lb_all = jnp.cumsum(jax.nn.softmax(hgrn_lb.astype(F32), axis=0), axis=0)
    bias = bias_tiles(rel_bias)
    mem2 = mem.reshape(b * m, d)

    h = x.reshape(n, d)
    for l in range(depth):
        j = l // 2
        mix = None
        if l % 2 == 0:
            proj_h, proj_f = norm_matmul(h, mix_norm[l], _pack_in_proj(e_w_in[j]).astype(BF16),
                                         tm=1024, tn=256,
                                         groups=((PROJ_H_COLS, BF16), (PROJ_F_COLS, F32)))
            proj_h = proj_h.reshape(b, s, -1)
            proj_f = proj_f.reshape(b, s, -1)
            a_out = dsa_attention(proj_h, proj_f, e_lat_norm[j], e_w_uk[j], e_w_uv[j], bias,
                                  topk=topk)
            b_out = hgrn2(proj_h, proj_f, lb_all[l], e_o_norm[j], rows=512)
            wa = e_w_out[j][:A_HEADS * A_HEAD_DIM].astype(BF16)
            wb = e_w_out[j][A_HEADS * A_HEAD_DIM:].astype(BF16)
            mix = (a_out, b_out, wa, wb)
        else:
            h = gmlp(h.reshape(b, s, d), mix_norm[l], o_w_in[j], o_ln_g[j], o_ln_b[j], o_w_sp[j],
                     o_b_sp[j], o_w_out[j], rows=1024).reshape(n, d)
        (kv,) = norm_matmul(mem2, mem_norm[l], x_wkv[l].astype(BF16), tm=512, tn=512,
                            groups=((x_wkv.shape[2], BF16),))
        kv = kv.reshape(b, m, -1)
        h = xattn(h.reshape(b, s, d), x_norm[l], x_wq[l], kv, x_wo[l], rows=1024,
                  mix=mix).reshape(n, d)
        h = swiglu(h, f_norm[l], f_w_gu[l], f_w_down[l], final_norm, tm=1024,
                   final=(l == depth - 1))
    return h.reshape(b, s, d)
```

```python
import functools

import jax
import jax.numpy as jnp
from jax import lax
from jax.experimental import pallas as pl
from jax.experimental.pallas import tpu as pltpu

F32 = jnp.float32
BF16 = jnp.bfloat16
I32 = jnp.int32

EPS = 1e-6
CHUNK = 64
LANE = 128
KT = 256
QB = 256
A_HEADS = 8
A_HEAD_DIM = 64
A_LATENT = 128
IDX_HEADS = 8
IDX_DIM = 64
TOPK_MAX = 256
REL_BUCKETS = 32
B_HEADS = 4
B_DK = 128
GM_CHUNK = 128
GM_GROUPS = 8
X_HEADS = 4
X_HEAD_DIM = 128
NEG = -1e30
INT_MIN = -2**31
LOG2E = 1.4426950408889634
ACC_ROWS = A_LATENT + 16
WEIGHT_SUM_FLOOR = 2.0 ** -60
PROJ_H_COLS = PROJ_F_COLS = 1792
PROJ_H_QA, PROJ_H_QI, PROJ_H_V = 0, 1, 2
PROJ_H_KI_LO, PROJ_H_KI_HI = 12, 13
PROJ_F_F, PROJ_F_Q, PROJ_F_GATE = 0, 1, 2
PROJ_F_C, PROJ_F_WI = 12, 13

VMEM_LIMIT = 56 * 1024 * 1024

NT = (((1,), (1,)), ((), ()))
TN = (((0,), (0,)), ((), ()))


def _rms(x, g):
    ms = jnp.mean(x * x, axis=-1, keepdims=True)
    return x * lax.rsqrt(ms + EPS) * g


def _params(sem, vmem=VMEM_LIMIT):
    return pltpu.CompilerParams(dimension_semantics=sem, vmem_limit_bytes=vmem)


def _norm_matmul_kernel(x_ref, g_ref, w_ref, *o_refs, tn):
    xn = _rms(x_ref[...], g_ref[...]).astype(BF16)
    base = 0
    for o_ref in o_refs:
        for c in range(o_ref.shape[1] // tn):
            y = jnp.dot(xn, w_ref[:, base + c * tn:base + (c + 1) * tn], preferred_element_type=F32)
            o_ref[:, c * tn:(c + 1) * tn] = y.astype(o_ref.dtype)
        base += o_ref.shape[1]


def norm_matmul(x, g, w, *, tm, tn, groups):
    n, k = x.shape
    nout = w.shape[1]
    assert sum(nc for nc, _ in groups) == nout and all(nc % tn == 0 for nc, _ in groups)
    return pl.pallas_call(
        functools.partial(_norm_matmul_kernel, tn=tn),
        out_shape=tuple(jax.ShapeDtypeStruct((n, nc), dt) for nc, dt in groups),
        grid=(n // tm,),
        in_specs=[
            pl.BlockSpec((tm, k), lambda i: (i, 0)),
            pl.BlockSpec((1, k), lambda i: (0, 0)),
            pl.BlockSpec((k, nout), lambda i: (0, 0), pipeline_mode=pl.Buffered(1)),
        ],
        out_specs=tuple(pl.BlockSpec((tm, nc), lambda i: (i, 0)) for nc, _ in groups),
        compiler_params=_params(("parallel",)),
        name="norm_matmul",
    )(x, g.reshape(1, k), w)


def _bias_tiles_kernel(rb_ref, o_ref):
    s = lax.broadcasted_iota(I32, (KT, QB), 0)
    t = lax.broadcasted_iota(I32, (KT, QB), 1)
    nb = REL_BUCKETS // 2
    max_exact = nb // 2
    for kind in range(3):
        rel = s - t - KT * kind
        n = jnp.abs(rel)
        n2 = n * n
        large = jnp.full((KT, QB), max_exact, I32)
        for j in range(1, nb - max_exact):
            large = large + jnp.where(n2 >= (max_exact * max_exact) * (2 ** j), 1, 0)
        bucket = jnp.where(rel > 0, nb, 0) + jnp.where(n < max_exact, n, large)
        buckets = (range(REL_BUCKETS), range(nb), (nb - 1,))[kind]
        hits = [(b, bucket == b) for b in buckets]
        for h in range(A_HEADS):
            acc = jnp.zeros((KT, QB), F32)
            for b, hit in hits:
                acc = jnp.where(hit, rb_ref[b, h] * LOG2E, acc)
            o_ref[kind, :, h * QB:(h + 1) * QB] = acc


def bias_tiles(rel_bias):
    return pl.pallas_call(
        _bias_tiles_kernel,
        out_shape=jax.ShapeDtypeStruct((3, KT, A_HEADS * QB), F32),
        in_specs=[pl.BlockSpec(memory_space=pltpu.SMEM)],
        out_specs=pl.BlockSpec(memory_space=pltpu.VMEM),
        name="bias_tiles",
    )(rel_bias)


def _dsa_kernel(qa_ref, qi_ref, wi_ref, c_ref, ka_ref, kb_ref, latg_ref, wuk_ref, wuvt_ref, bias_ref,
                o_ref, cn_ref, ct_ref, keys_ref, hi_ref, s_ref, acc_ref,
                d0_ref, d1_ref, r0_ref, r1_ref, p0_ref, p1_ref, mx_ref, re_ref, *, topk, ntiles):
    i = pl.program_id(1)
    hq = A_HEADS * QB
    nt = i + 1
    last = i

    @pl.when(i == 0)
    def _prep():
        cn = _rms(c_ref[0], latg_ref[...])
        cn_ref[...] = cn.astype(BF16)
        ones = jnp.ones((ACC_ROWS - A_LATENT, KT), BF16)
        for j in range(ntiles):
            ct_ref[j, :A_LATENT, :] = cn[j * KT:(j + 1) * KT, :].T.astype(BF16)
            ct_ref[j, A_LATENT:, :] = ones

    qa = qa_ref[0]
    ql = jnp.concatenate(
        [jnp.dot(qa[:, (h // 2) * LANE:(h // 2 + 1) * LANE], wuk_ref[h],
                 preferred_element_type=F32) for h in range(A_HEADS)], axis=0)
    qlb = (ql * (A_HEAD_DIM ** -0.5 * LOG2E)).astype(BF16)
    qi = qi_ref[0]
    qp = jnp.concatenate([qi[:, p * LANE:(p + 1) * LANE] for p in range(IDX_HEADS // 2)], axis=0)
    w = wi_ref[0] * (IDX_HEADS ** -0.5) * (IDX_DIM ** -0.5)
    wt = w.T

    row = lax.broadcasted_iota(I32, (KT, QB), 0)
    col = lax.broadcasted_iota(I32, (KT, QB), 1)
    chunk_bits = CHUNK.bit_length() - 1
    inadm_last = jnp.where(lax.shift_right_logical(row, chunk_bits)
                           > lax.shift_right_logical(col, chunk_bits), 1, 0)

    def inadmissible(j):
        return (inadm_last * jnp.where(j == last, 1, 0)) != 0

    def rows(j):
        return pl.ds(pl.multiple_of(j * KT, KT), KT)

    def pipelined(prepare, consume, carry):
        prepare(0, 0)

        def pair(t, carry):
            j = 2 * t
            prepare(j + 1, 1)
            carry = consume(j, 0, carry)
            prepare(jnp.minimum(j + 2, last), 0)
            return consume(j + 1, 1, carry)

        carry = lax.fori_loop(0, lax.shift_right_logical(nt, 1), pair, carry)
        return lax.cond((nt & 1) == 1, lambda c: consume(last, 0, c), lambda c: c, carry)

    d_refs = (d0_ref, d1_ref)

    def dots(j, slot):
        ks = rows(j)
        d_ref = d_refs[slot]
        kk = jnp.concatenate([ka_ref[0, ks, :], kb_ref[0, ks, :]], axis=0)
        d_ref[...] = lax.dot_general(kk, qp, NT, preferred_element_type=F32)
        s = lax.dot_general(cn_ref[ks, :], qlb, NT, preferred_element_type=F32)
        s = s + bias_ref[jnp.minimum(i - j, 2)]
        s_ref[j] = s
        mx_ref[j] = s.reshape(KT // 8, 8, hq).max(axis=0)

    def score_keys(j, slot, carry):
        d_ref = d_refs[slot]
        sc = jnp.zeros((KT, QB), F32)
        for h in range(IDX_HEADS):
            d = d_ref[(h % 2) * KT:(h % 2 + 1) * KT, (h // 2) * QB:(h // 2 + 1) * QB]
            sc = sc + jnp.maximum(d, 0.0) * wt[h:h + 1, :]
        sc = jnp.where(inadmissible(j), -jnp.inf, sc)
        bits = pltpu.bitcast(sc, I32)
        key = jnp.where(bits < 0, bits ^ 0x7FFFFFFF, bits)
        key = jnp.where(bits == INT_MIN, 0, key)
        keys_ref[j] = key
        hi_ref[j] = lax.shift_right_arithmetic(key, 16).astype(jnp.int16)
        return carry

    pipelined(dots, score_keys, 0)

    half_min = -2 ** 15

    @pl.when(nt < ntiles)
    def _pad():
        hi_ref[nt] = jnp.full((KT, QB), half_min, jnp.int16)

    def select(cap):
        def count16(pred):
            acc = jnp.zeros((16, QB), jnp.int16)
            for j in range(cap):
                hit = jnp.where(pred(hi_ref[j]), jnp.int16(1), jnp.int16(0))
                parts = [hit[r * 16:(r + 1) * 16, :] for r in range(KT // 16)]
                while len(parts) > 1:
                    parts = [a + b for a, b in zip(parts[::2], parts[1::2])]
                acc = acc + parts[0]
            return acc.astype(I32).sum(axis=0, keepdims=True)

        def bisect16(target):
            def step_fn(step, lo):
                trial = lo + lax.shift_left(jnp.int32(1), 15 - step)
                t16 = trial.astype(jnp.int16)
                return jnp.where(count16(lambda k: k >= t16) >= target, trial, lo)
            return lax.fori_loop(0, 16, step_fn, jnp.full((1, QB), half_min, I32))

        thr_hi = bisect16(topk)

        def low_halves(j, acc):
            key = keys_ref[j]
            hi = lax.shift_right_arithmetic(key, 16)
            lo = (key & 0xFFFF) + half_min
            hi_ref[j] = jnp.where(hi == thr_hi, lo, half_min).astype(jnp.int16)
            above = jnp.where(hi > thr_hi, 1, 0)
            return acc + above.reshape(KT // 8, 8, QB).sum(axis=0)

        above_hi = lax.fori_loop(0, nt, low_halves, jnp.zeros((8, QB), I32))
        above_hi = above_hi.sum(axis=0, keepdims=True)
        thr_lo = bisect16(topk - above_hi)
        thr_lo16 = thr_lo.astype(jnp.int16)
        above = above_hi + count16(lambda k: k > thr_lo16)
        return lax.shift_left(thr_hi, 16) + (thr_lo - half_min), (topk - above).astype(F32)

    caps = sorted(set(min(c, ntiles) for c in range(2, ntiles + 2, 2)))
    thr, need = lax.switch(lax.shift_right_logical(nt - 1, 1),
                           [functools.partial(select, c) for c in caps])

    trow = lax.broadcasted_iota(I32, (KT, KT), 0)
    tcol = lax.broadcasted_iota(I32, (KT, KT), 1)
    tri = jnp.where(tcol < trow, 1.0, 0.0).astype(BF16)

    r_refs = (r0_ref, r1_ref)
    p_refs = (p0_ref, p1_ref)

    def selected(j, key, rank):
        sel = (key > thr) | ((key == thr) & (rank < need))
        return sel & jnp.logical_not(inadmissible(j))

    def weighted(j, slot, carry):
        acc_ref[...] += jnp.dot(ct_ref[j], p_refs[slot][...], preferred_element_type=F32)
        return carry

    def ties_before(j, run_eq):
        re_ref[j] = jnp.broadcast_to(run_eq, (8, QB))
        return run_eq + jnp.where(keys_ref[j] == thr, 1.0, 0.0).sum(axis=0, keepdims=True)

    lax.fori_loop(0, nt, ties_before, jnp.zeros((1, QB), F32))
    m_all = lax.fori_loop(0, nt, lambda j, mx: jnp.maximum(mx, mx_ref[j]),
                          jnp.full((8, hq), NEG, F32)).max(axis=0, keepdims=True)

    def fast_weights(j, slot):
        key = keys_ref[j]
        eqf = jnp.where(key == thr, 1.0, 0.0).astype(BF16)
        rank = jnp.dot(tri, eqf, preferred_element_type=F32) + re_ref[j, 0:1, :]
        keep = jnp.where(selected(j, key, rank), 1.0, 0.0).astype(BF16)
        for h in range(A_HEADS):
            cols = slice(h * QB, (h + 1) * QB)
            p = jnp.exp2(s_ref[j, :, cols] - m_all[:, cols]).astype(BF16)
            p_refs[slot][:, cols] = p * keep

    acc_ref[...] = jnp.zeros(acc_ref.shape, F32)
    pipelined(fast_weights, weighted, 0)

    @pl.when(jnp.min(acc_ref[A_LATENT:A_LATENT + 1, :]) < WEIGHT_SUM_FLOOR)
    def _exact_shift():
        def tie_ranks(j, slot):
            eqf = jnp.where(keys_ref[j] == thr, 1.0, 0.0).astype(BF16)
            r_refs[slot][...] = jnp.dot(tri, eqf, preferred_element_type=F32)

        def mask_tile(j, slot, m8):
            rank = r_refs[slot][...] + re_ref[j, 0:1, :]
            negm = jnp.where(selected(j, keys_ref[j], rank), 0.0, NEG)
            m8_new = []
            for h in range(A_HEADS):
                cols = slice(h * QB, (h + 1) * QB)
                blk = s_ref[j, :, cols] + negm
                s_ref[j, :, cols] = blk
                m8_new.append(jnp.maximum(m8[:, cols], blk.reshape(KT // 8, 8, QB).max(axis=0)))
            return jnp.concatenate(m8_new, axis=1)

        m8 = pipelined(tie_ranks, mask_tile, jnp.full((8, hq), NEG, F32))
        m = m8.max(axis=0, keepdims=True)

        def weights(j, slot):
            p_refs[slot][...] = jnp.exp2(s_ref[j] - m).astype(BF16)

        acc_ref[...] = jnp.zeros(acc_ref.shape, F32)
        pipelined(weights, weighted, 0)

    inv_l = 1.0 / acc_ref[A_LATENT:A_LATENT + 1, :]
    ot = (acc_ref[:A_LATENT, :] * inv_l).astype(BF16)
    for h in range(A_HEADS):
        o_ref[0, h * A_HEAD_DIM:(h + 1) * A_HEAD_DIM, :] = jnp.dot(
            wuvt_ref[h], ot[:, h * QB:(h + 1) * QB], preferred_element_type=F32
        ).astype(o_ref.dtype)


def dsa_attention(proj_h, proj_f, lat_g, w_uk, w_uv, bias, *, topk):
    b, s, _ = proj_h.shape
    ntiles = s // KT
    kern = functools.partial(_dsa_kernel, topk=topk, ntiles=ntiles)
    wuk_t = jnp.swapaxes(w_uk, 1, 2)
    zero = jnp.zeros_like(wuk_t)
    odd = (jnp.arange(A_HEADS) % 2 == 1)[:, None, None]
    wuk_pad = jnp.concatenate([jnp.where(odd, zero, wuk_t), jnp.where(odd, wuk_t, zero)], axis=1)
    col = lambda blk: (lambda bi, i: (bi, 0, blk))
    const = lambda shape: pl.BlockSpec(shape, lambda bi, i: (0,) * len(shape))
    return pl.pallas_call(
        kern,
        out_shape=jax.ShapeDtypeStruct((b, A_HEADS * A_HEAD_DIM, s), BF16),
        grid=(b, s // QB),
        in_specs=[
            pl.BlockSpec((1, QB, 512), lambda bi, i: (bi, i, PROJ_H_QA)),
            pl.BlockSpec((1, QB, 512), lambda bi, i: (bi, i, PROJ_H_QI)),
            pl.BlockSpec((1, QB, LANE), lambda bi, i: (bi, i, PROJ_F_WI)),
            pl.BlockSpec((1, s, LANE), col(PROJ_F_C)),
            pl.BlockSpec((1, s, LANE), col(PROJ_H_KI_LO)),
            pl.BlockSpec((1, s, LANE), col(PROJ_H_KI_HI)),
            const((1, A_LATENT)),
            const((A_HEADS, LANE, A_LATENT)),
            const((A_HEADS, A_HEAD_DIM, A_LATENT)),
            pl.BlockSpec((3, KT, A_HEADS * QB), lambda bi, i: (0, 0, 0),
                         pipeline_mode=pl.Buffered(1)),
        ],
        out_specs=pl.BlockSpec((1, A_HEADS * A_HEAD_DIM, QB), lambda bi, i: (bi, 0, i)),
        scratch_shapes=[
            pltpu.VMEM((s, A_LATENT), BF16),
            pltpu.VMEM((ntiles, ACC_ROWS, KT), BF16),
            pltpu.VMEM((ntiles, KT, QB), I32),
            pltpu.VMEM((ntiles, KT, QB), jnp.int16),
            pltpu.VMEM((ntiles, KT, A_HEADS * QB), F32),
            pltpu.VMEM((ACC_ROWS, A_HEADS * QB), F32),
            pltpu.VMEM((2 * KT, (IDX_HEADS // 2) * QB), F32),
            pltpu.VMEM((2 * KT, (IDX_HEADS // 2) * QB), F32),
            pltpu.VMEM((KT, QB), F32),
            pltpu.VMEM((KT, QB), F32),
            pltpu.VMEM((KT, A_HEADS * QB), BF16),
            pltpu.VMEM((KT, A_HEADS * QB), BF16),
            pltpu.VMEM((ntiles, 8, A_HEADS * QB), F32),
            pltpu.VMEM((ntiles, 8, QB), F32),
        ],
        compiler_params=_params(("parallel", "arbitrary")),
        name="dsa_attention",
    )(proj_h, proj_h, proj_f, proj_f, proj_h, proj_h, lat_g.reshape(1, A_LATENT),
      wuk_pad.astype(BF16), jnp.swapaxes(w_uv, 1, 2).astype(BF16), bias)


def _hgrn_kernel(f_ref, v_ref, q_ref, g_ref, lb_ref, og_ref, o_ref, st_ref, *, nchunk):
    @pl.when(pl.program_id(1) == 0)
    def _():
        st_ref[...] = jnp.zeros(st_ref.shape, F32)

    lb = lb_ref[...]
    og = og_ref[...]
    row = lax.broadcasted_iota(I32, (CHUNK, CHUNK), 0)
    col = lax.broadcasted_iota(I32, (CHUNK, CHUNK), 1)
    causal = col <= row
    tri = jnp.where(causal, 1.0, 0.0).astype(BF16)
    hc = CHUNK // 2
    top = lax.broadcasted_iota(I32, (CHUNK, B_HEADS * B_DK), 0) < hc
    zeros_h = jnp.zeros((hc, B_DK), BF16)

    for c in range(nchunk):
        rows = slice(c * CHUNK, (c + 1) * CHUNK)
        f = lb + (1.0 - lb) * jax.nn.sigmoid(f_ref[0, rows, :])
        g = jnp.log(f)
        k = 1.0 - f
        q = q_ref[0, rows, :]
        qf = q * jax.nn.sigmoid(q)
        vb = v_ref[0, rows, :]
        g_hi = g.astype(BF16)
        g_lo = (g - g_hi.astype(F32)).astype(BF16)
        bc = (jnp.dot(tri, g_hi, preferred_element_type=F32)
              + jnp.dot(tri, g_lo, preferred_element_type=F32))
        b_end = bc[CHUNK - 1:CHUNK, :]
        b_half = bc[hc - 1:hc, :]
        e_in = bc - jnp.where(top, bc[hc // 2 - 1:hc // 2, :], bc[hc + hc // 2 - 1:hc + hc // 2, :])
        q_in = (qf * jnp.exp(e_in)).astype(BF16)
        k_in = (k * jnp.exp(-e_in)).astype(BF16)
        q_x = (qf[hc:, :] * jnp.exp(bc[hc:, :] - b_half)).astype(BF16)
        k_x = (k[:hc, :] * jnp.exp(b_half - bc[:hc, :])).astype(BF16)
        qb = (qf * jnp.exp(bc)).astype(BF16)
        kd = (k * jnp.exp(b_end - bc)).astype(BF16)
        dec = jnp.exp(b_end)
        gate = g_ref[0, rows, :]
        gate = gate * jax.nn.sigmoid(gate)
        for h in range(B_HEADS):
            sl = slice(h * B_DK, (h + 1) * B_DK)
            q3 = jnp.concatenate([
                jnp.concatenate([q_in[:hc, sl], zeros_h, zeros_h], axis=1),
                jnp.concatenate([zeros_h, q_x[:, sl], q_in[hc:, sl]], axis=1)], axis=0)
            k3 = jnp.concatenate([
                jnp.concatenate([k_in[:hc, sl], k_x[:, sl], zeros_h], axis=1),
                jnp.concatenate([zeros_h, zeros_h, k_in[hc:, sl]], axis=1)], axis=0)
            sc = lax.dot_general(q3, k3, NT, preferred_element_type=F32)
            sc = jnp.where(causal, sc, 0.0).astype(BF16)
            st = st_ref[h]
            o = (jnp.dot(sc, vb[:, sl], preferred_element_type=F32)
                 + lax.dot_general(qb[:, sl], st.astype(BF16), NT, preferred_element_type=F32))
            st_ref[h] = st * dec[:, sl] + lax.dot_general(vb[:, sl], kd[:, sl], TN,
                                                          preferred_element_type=F32)
            o_ref[0, rows, sl] = (_rms(o, og) * gate[:, sl]).astype(o_ref.dtype)


def hgrn2(proj_h, proj_f, lb, o_g, *, rows):
    b, s, _ = proj_h.shape
    width = B_HEADS * B_DK
    kern = functools.partial(_hgrn_kernel, nchunk=rows // CHUNK)
    spec = lambda blk: pl.BlockSpec((1, rows, width), lambda bi, r: (bi, r, blk))
    return pl.pallas_call(
        kern,
        out_shape=jax.ShapeDtypeStruct((b, s, width), BF16),
        grid=(b, s // rows),
        in_specs=[spec(PROJ_F_F), spec(PROJ_H_V), spec(PROJ_F_Q), spec(PROJ_F_GATE),
                  pl.BlockSpec((1, width), lambda bi, r: (0, 0)),
                  pl.BlockSpec((1, B_DK), lambda bi, r: (0, 0))],
        out_specs=pl.BlockSpec((1, rows, width), lambda bi, r: (bi, r, 0)),
        scratch_shapes=[pltpu.VMEM((B_HEADS, B_DK, B_DK), F32)],
        compiler_params=_params(("parallel", "arbitrary")),
        name="hgrn2",
    )(proj_f, proj_h, proj_f, proj_f, lb.reshape(1, width), o_g.reshape(1, B_DK))


def _gmlp_kernel(h_ref, g_ref, win_ref, lng_ref, lnb_ref, wsp_ref, bsp_ref, wout_ref, o_ref,
                 gated_ref, *, nchunk, nsplit):
    width = h_ref.shape[2]
    gw = width // GM_GROUPS
    row = lax.broadcasted_iota(I32, (GM_CHUNK, GM_CHUNK), 0)
    col = lax.broadcasted_iota(I32, (GM_CHUNK, GM_CHUNK), 1)
    wgs = [jnp.where(col <= row, wsp_ref[gi], 0.0).astype(BF16) for gi in range(GM_GROUPS)]
    nc = nchunk // nsplit
    xs = [h_ref[0, sp * nc * GM_CHUNK:(sp + 1) * nc * GM_CHUNK, :] for sp in range(nsplit)]
    uvs = [jnp.dot(_rms(x, g_ref[...]).astype(BF16), win_ref[...], preferred_element_type=F32)
           for x in xs]
    for sp in range(nsplit):
        r0 = sp * nc * GM_CHUNK
        x, uv = xs[sp], uvs[sp]
        uv = 0.5 * uv * (1.0 + lax.erf(uv * (0.5 ** 0.5)))
        u = uv[:, :width]
        v = uv[:, width:]
        mu = jnp.mean(v, axis=-1, keepdims=True)
        vc = v - mu
        var = jnp.mean(vc * vc, axis=-1, keepdims=True)
        vn = (vc * lax.rsqrt(var + EPS) * lng_ref[...] + lnb_ref[...]).astype(BF16)
        for gi in range(GM_GROUPS):
            cs = slice(gi * gw, (gi + 1) * gw)
            vg = jnp.concatenate([vn[c * GM_CHUNK:(c + 1) * GM_CHUNK, cs] for c in range(nc)], axis=1)
            mixed = jnp.dot(wgs[gi], vg, preferred_element_type=F32)
            for c in range(nc):
                rs = slice(c * GM_CHUNK, (c + 1) * GM_CHUNK)
                mc = mixed[:, c * gw:(c + 1) * gw] + bsp_ref[:, cs]
                gated_ref[r0 + c * GM_CHUNK:r0 + (c + 1) * GM_CHUNK, cs] = (u[rs, cs] * mc).astype(BF16)
        o_ref[0, r0:r0 + nc * GM_CHUNK, :] = x + jnp.dot(
            gated_ref[r0:r0 + nc * GM_CHUNK, :], wout_ref[...], preferred_element_type=F32)


def gmlp(h, g, w_in, ln_g, ln_b, w_sp, b_sp, w_out, *, rows):
    b, s, d = h.shape
    width = w_out.shape[0]
    gw = width // GM_GROUPS
    bsp_full = jnp.repeat(jnp.transpose(b_sp), gw, axis=1)
    kern = functools.partial(_gmlp_kernel, nchunk=rows // GM_CHUNK, nsplit=4)
    const = lambda shape: pl.BlockSpec(shape, lambda bi, r: (0,) * len(shape),
                                       pipeline_mode=pl.Buffered(1))
    return pl.pallas_call(
        kern,
        out_shape=jax.ShapeDtypeStruct((b, s, d), F32),
        grid=(b, s // rows),
        in_specs=[pl.BlockSpec((1, rows, d), lambda bi, r: (bi, r, 0)),
                  const((1, d)), const((d, 2 * width)), const((1, width)), const((1, width)),
                  const((GM_GROUPS, GM_CHUNK, GM_CHUNK)), const((GM_CHUNK, width)),
                  const((width, d))],
        out_specs=pl.BlockSpec((1, rows, d), lambda bi, r: (bi, r, 0)),
        scratch_shapes=[pltpu.VMEM((rows, width), BF16)],
        compiler_params=_params(("parallel", "parallel")),
        name="gmlp",
    )(h, g.reshape(1, d), w_in.astype(BF16), ln_g.reshape(1, width), ln_b.reshape(1, width),
      w_sp, bsp_full, w_out.astype(BF16))


def _xattn_kernel(*refs, mixed):
    if mixed:
        h_ref, at_ref, b_ref, wa_ref, wb_ref, g_ref, wq_ref, k_ref, v_ref, wo_ref, o_ref = refs
        x = (h_ref[0] + lax.dot_general(at_ref[0], wa_ref[...], TN, preferred_element_type=F32)
             + jnp.dot(b_ref[0], wb_ref[...], preferred_element_type=F32))
    else:
        h_ref, g_ref, wq_ref, k_ref, v_ref, wo_ref, o_ref = refs
        x = h_ref[0]
    xn = _rms(x, g_ref[...]).astype(BF16)
    q = jnp.dot(xn, wq_ref[...], preferred_element_type=F32)
    q = (q * (X_HEAD_DIM ** -0.5 * LOG2E)).astype(BF16)
    ones = jnp.ones((k_ref.shape[1], X_HEAD_DIM), BF16)
    outs = []
    for h in range(X_HEADS):
        sl = slice(h * X_HEAD_DIM, (h + 1) * X_HEAD_DIM)
        s = lax.dot_general(q[:, sl], k_ref[0, :, sl], NT, preferred_element_type=F32)
        p = jnp.exp2(s - s.max(axis=-1, keepdims=True)).astype(BF16)
        ol = jnp.dot(p, jnp.concatenate([v_ref[0, :, sl], ones], axis=1),
                     preferred_element_type=F32)
        outs.append((ol[:, :X_HEAD_DIM] * (1.0 / ol[:, X_HEAD_DIM:X_HEAD_DIM + 1])).astype(BF16))
    att = jnp.concatenate(outs, axis=1)
    o_ref[0] = x + jnp.dot(att, wo_ref[...], preferred_element_type=F32)


def xattn(h, g, wq, kv, wo, *, rows, mix=None):
    b, s, d = h.shape
    m = kv.shape[1]
    xw = wq.shape[1]
    const = lambda shape: pl.BlockSpec(shape, lambda bi, r: (0,) * len(shape))
    rows_spec = lambda width: pl.BlockSpec((1, rows, width), lambda bi, r: (bi, r, 0))
    mix_specs, mix_args = [], []
    if mix is not None:
        a_t, b_out, w_mix = mix
        ka, kb = a_t.shape[1], b_out.shape[2]
        assert ka == kb and w_mix.shape[0] == ka + kb
        mix_specs = [pl.BlockSpec((1, ka, rows), lambda bi, r: (bi, 0, r)), rows_spec(kb),
                     pl.BlockSpec((ka, d), lambda bi, r: (0, 0)),
                     pl.BlockSpec((kb, d), lambda bi, r: (1, 0))]
        mix_args = [a_t, b_out, w_mix, w_mix]
    return pl.pallas_call(
        functools.partial(_xattn_kernel, mixed=mix is not None),
        out_shape=jax.ShapeDtypeStruct((b, s, d), F32),
        grid=(b, s // rows),
        in_specs=[rows_spec(d), *mix_specs,
                  const((1, d)), const((d, xw)),
                  pl.BlockSpec((1, m, xw), lambda bi, r: (bi, 0, 0)),
                  pl.BlockSpec((1, m, xw), lambda bi, r: (bi, 0, 1)),
                  const((xw, d))],
        out_specs=rows_spec(d),
        compiler_params=_params(("parallel", "parallel")),
        name="xattn",
    )(h, *mix_args, g.reshape(1, d), wq.astype(BF16), kv, kv, wo.astype(BF16))


def _swiglu_kernel(x_ref, g_ref, wgu_ref, wd_ref, fg_ref, o_ref, a_ref, *, ff, fchunk, final):
    x = x_ref[...]
    xn = _rms(x, g_ref[...]).astype(BF16)
    for c in range(ff // fchunk):
        gate = jnp.dot(xn, wgu_ref[:, c * fchunk:(c + 1) * fchunk], preferred_element_type=F32)
        up = jnp.dot(xn, wgu_ref[:, ff + c * fchunk:ff + (c + 1) * fchunk],
                     preferred_element_type=F32)
        a_ref[:, c * fchunk:(c + 1) * fchunk] = (gate * jax.nn.sigmoid(gate) * up).astype(BF16)
    y = x + jnp.dot(a_ref[...], wd_ref[...], preferred_element_type=F32)
    if final:
        y = _rms(y, fg_ref[...])
    o_ref[...] = y


def swiglu(x, g, w_gu, w_down, final_g, *, tm, final):
    n, d = x.shape
    ff = w_down.shape[0]
    kern = functools.partial(_swiglu_kernel, ff=ff, fchunk=256, final=final)
    once = dict(pipeline_mode=pl.Buffered(1))
    return pl.pallas_call(
        kern,
        out_shape=jax.ShapeDtypeStruct((n, d), F32),
        grid=(n // tm,),
        in_specs=[pl.BlockSpec((tm, d), lambda i: (i, 0)),
                  pl.BlockSpec((1, d), lambda i: (0, 0)),
                  pl.BlockSpec((d, 2 * ff), lambda i: (0, 0), **once),
                  pl.BlockSpec((ff, d), lambda i: (0, 0), **once),
                  pl.BlockSpec((1, d), lambda i: (0, 0))],
        out_specs=pl.BlockSpec((tm, d), lambda i: (i, 0)),
        scratch_shapes=[pltpu.VMEM((tm, ff), BF16)],
        compiler_params=_params(("parallel",)),
        name="swiglu",
    )(x, g.reshape(1, d), w_gu.astype(BF16), w_down.astype(BF16), final_g.reshape(1, d))


def _pack_in_proj(w):
    sizes = [A_HEADS * A_HEAD_DIM, A_LATENT, IDX_HEADS * IDX_DIM, IDX_DIM, IDX_HEADS,
             B_HEADS * B_DK, B_HEADS * B_DK, B_HEADS * B_DK, B_HEADS * B_DK]
    offs = [0]
    for sz in sizes:
        offs.append(offs[-1] + sz)
    q_a, c, qi, ki, wi, f_b, i_b, q_b, g_b = [w[:, offs[n]:offs[n + 1]] for n in range(9)]
    z = lambda n: jnp.zeros((w.shape[0], n), w.dtype)
    return jnp.concatenate([q_a, qi, i_b, ki, z(LANE - IDX_DIM), z(LANE - IDX_DIM), ki,
                            f_b, q_b, g_b, c, wi, z(LANE - IDX_HEADS)], axis=1)


def kernel(x, mem, rel_bias, hgrn_lb, mix_norm, e_w_in, e_lat_norm, e_w_uk, e_w_uv, e_o_norm, e_w_out, o_w_in, o_ln_g, o_ln_b, o_w_sp, o_b_sp, o_w_out, x_norm, mem_norm, x_wq, x_wkv, x_wo, f_norm, f_w_gu, f_w_down, final_norm):
    b, s, d = x.shape
    m = mem.shape[1]
    n = b * s
    depth = mix_norm.shape[0]
    topk = min(TOPK_MAX, s // 4)
    lb_all = jnp.cumsum(jax.nn.softmax(hgrn_lb.astype(F32), axis=0), axis=0)
    bias = bias_tiles(rel_bias)
    mem2 = mem.reshape(b * m, d)

    h = x.reshape(n, d)
    for l in range(depth):
        j = l // 2
        mix = None
        if l % 2 == 0:
            proj_h, proj_f = norm_matmul(h, mix_norm[l], _pack_in_proj(e_w_in[j].astype(BF16)),
                                         tm=1024, tn=256,
                                         groups=((PROJ_H_COLS, BF16), (PROJ_F_COLS, F32)))
            proj_h = proj_h.reshape(b, s, -1)
            proj_f = proj_f.reshape(b, s, -1)
            a_out = dsa_attention(proj_h, proj_f, e_lat_norm[j], e_w_uk[j], e_w_uv[j], bias,
                                  topk=topk)
            b_out = hgrn2(proj_h, proj_f, lb_all[l], e_o_norm[j], rows=512)
            mix = (a_out, b_out, e_w_out[j].astype(BF16))
        else:
            h = gmlp(h.reshape(b, s, d), mix_norm[l], o_w_in[j], o_ln_g[j], o_ln_b[j], o_w_sp[j],
                     o_b_sp[j], o_w_out[j], rows=1024).reshape(n, d)
        (kv,) = norm_matmul(mem2, mem_norm[l], x_wkv[l].astype(BF16), tm=512, tn=512,
                            groups=((x_wkv.shape[2], BF16),))
        kv = kv.reshape(b, m, -1)
        h = xattn(h.reshape(b, s, d), x_norm[l], x_wq[l], kv, x_wo[l], rows=1024,
                  mix=mix).reshape(n, d)
        h = swiglu(h, f_norm[l], f_w_gu[l], f_w_down[l], final_norm, tm=1024,
                   final=(l == depth - 1))
    return h.reshape(b, s, d)
```

```python
import functools

import jax
import jax.numpy as jnp
from jax import lax
from jax.experimental import pallas as pl
from jax.experimental.pallas import tpu as pltpu

F32 = jnp.float32
BF16 = jnp.bfloat16
I32 = jnp.int32

EPS = 1e-6
CHUNK = 64
LANE = 128
KT = 256
QB = 256
A_HEADS = 8
A_HEAD_DIM = 64
A_LATENT = 128
IDX_HEADS = 8
IDX_DIM = 64
TOPK_MAX = 256
REL_BUCKETS = 32
B_HEADS = 4
B_DK = 128
GM_CHUNK = 128
GM_GROUPS = 8
X_HEADS = 4
X_HEAD_DIM = 128
NEG = -1e30
INT_MIN = -2**31
LOG2E = 1.4426950408889634
ACC_ROWS = A_LATENT + 16
WEIGHT_SUM_FLOOR = 2.0 ** -60
PROJ_H_COLS = PROJ_F_COLS = 1792
PROJ_H_QA, PROJ_H_QI, PROJ_H_V = 0, 1, 2
PROJ_H_KI_LO, PROJ_H_KI_HI = 12, 13
PROJ_F_F, PROJ_F_Q, PROJ_F_GATE = 0, 1, 2
PROJ_F_C, PROJ_F_WI = 12, 13

VMEM_LIMIT = 56 * 1024 * 1024

NT = (((1,), (1,)), ((), ()))
TN = (((0,), (0,)), ((), ()))


def _rms(x, g):
    ms = jnp.mean(x * x, axis=-1, keepdims=True)
    return x * lax.rsqrt(ms + EPS) * g


def _params(sem, vmem=VMEM_LIMIT):
    return pltpu.CompilerParams(dimension_semantics=sem, vmem_limit_bytes=vmem)


def _norm_matmul_kernel(x_ref, g_ref, w_ref, *o_refs, tn):
    xn = _rms(x_ref[...], g_ref[...]).astype(BF16)
    base = 0
    for o_ref in o_refs:
        for c in range(o_ref.shape[1] // tn):
            y = jnp.dot(xn, w_ref[:, base + c * tn:base + (c + 1) * tn], preferred_element_type=F32)
            o_ref[:, c * tn:(c + 1) * tn] = y.astype(o_ref.dtype)
        base += o_ref.shape[1]


def norm_matmul(x, g, w, *, tm, tn, groups):
    n, k = x.shape
    nout = w.shape[1]
    assert sum(nc for nc, _ in groups) == nout and all(nc % tn == 0 for nc, _ in groups)
    return pl.pallas_call(
        functools.partial(_norm_matmul_kernel, tn=tn),
        out_shape=tuple(jax.ShapeDtypeStruct((n, nc), dt) for nc, dt in groups),
        grid=(n // tm,),
        in_specs=[
            pl.BlockSpec((tm, k), lambda i: (i, 0)),
            pl.BlockSpec((1, k), lambda i: (0, 0)),
            pl.BlockSpec((k, nout), lambda i: (0, 0), pipeline_mode=pl.Buffered(1)),
        ],
        out_specs=tuple(pl.BlockSpec((tm, nc), lambda i: (i, 0)) for nc, _ in groups),
        compiler_params=_params(("parallel",)),
        name="norm_matmul",
    )(x, g.reshape(1, k), w)


def _bias_tiles_kernel(rb_ref, o_ref):
    s = lax.broadcasted_iota(I32, (KT, QB), 0)
    t = lax.broadcasted_iota(I32, (KT, QB), 1)
    nb = REL_BUCKETS // 2
    max_exact = nb // 2
    for kind in range(3):
        rel = s - t - KT * kind
        n = jnp.abs(rel)
        n2 = n * n
        large = jnp.full((KT, QB), max_exact, I32)
        for j in range(1, nb - max_exact):
            large = large + jnp.where(n2 >= (max_exact * max_exact) * (2 ** j), 1, 0)
        bucket = jnp.where(rel > 0, nb, 0) + jnp.where(n < max_exact, n, large)
        buckets = (range(REL_BUCKETS), range(nb), (nb - 1,))[kind]
        hits = [(b, bucket == b) for b in buckets]
        for h in range(A_HEADS):
            acc = jnp.zeros((KT, QB), F32)
            for b, hit in hits:
                acc = jnp.where(hit, rb_ref[b, h] * LOG2E, acc)
            o_ref[kind, :, h * QB:(h + 1) * QB] = acc


def bias_tiles(rel_bias):
    return pl.pallas_call(
        _bias_tiles_kernel,
        out_shape=jax.ShapeDtypeStruct((3, KT, A_HEADS * QB), F32),
        in_specs=[pl.BlockSpec(memory_space=pltpu.SMEM)],
        out_specs=pl.BlockSpec(memory_space=pltpu.VMEM),
        name="bias_tiles",
    )(rel_bias)


def _dsa_kernel(qa_ref, qi_ref, wi_ref, c_ref, ka_ref, kb_ref, latg_ref, wuk_ref, wuvt_ref, bias_ref,
                o_ref, cn_ref, ct_ref, keys_ref, hi_ref, s_ref, acc_ref,
                d0_ref, d1_ref, r0_ref, r1_ref, p0_ref, p1_ref, mx_ref, re_ref, *, topk, ntiles):
    i = pl.program_id(1)
    hq = A_HEADS * QB
    nt = i + 1
    last = i

    @pl.when(i == 0)
    def _prep():
        cn = _rms(c_ref[0], latg_ref[...])
        cn_ref[...] = cn.astype(BF16)
        ones = jnp.ones((ACC_ROWS - A_LATENT, KT), BF16)
        for j in range(ntiles):
            ct_ref[j, :A_LATENT, :] = cn[j * KT:(j + 1) * KT, :].T.astype(BF16)
            ct_ref[j, A_LATENT:, :] = ones

    qa = qa_ref[0]
    ql = jnp.concatenate(
        [jnp.dot(qa[:, (h // 2) * LANE:(h // 2 + 1) * LANE], wuk_ref[h],
                 preferred_element_type=F32) for h in range(A_HEADS)], axis=0)
    qlb = (ql * (A_HEAD_DIM ** -0.5 * LOG2E)).astype(BF16)
    qi = qi_ref[0]
    qp = jnp.concatenate([qi[:, p * LANE:(p + 1) * LANE] for p in range(IDX_HEADS // 2)], axis=0)
    w = wi_ref[0] * (IDX_HEADS ** -0.5) * (IDX_DIM ** -0.5)
    wt = w.T

    row = lax.broadcasted_iota(I32, (KT, QB), 0)
    col = lax.broadcasted_iota(I32, (KT, QB), 1)
    chunk_bits = CHUNK.bit_length() - 1
    inadm_last = jnp.where(lax.shift_right_logical(row, chunk_bits)
                           > lax.shift_right_logical(col, chunk_bits), 1, 0)

    def inadmissible(j):
        return (inadm_last * jnp.where(j == last, 1, 0)) != 0

    def rows(j):
        return pl.ds(pl.multiple_of(j * KT, KT), KT)

    def pipelined(prepare, consume, carry):
        prepare(0, 0)

        def pair(t, carry):
            j = 2 * t
            prepare(j + 1, 1)
            carry = consume(j, 0, carry)
            prepare(jnp.minimum(j + 2, last), 0)
            return consume(j + 1, 1, carry)

        carry = lax.fori_loop(0, lax.shift_right_logical(nt, 1), pair, carry)
        return lax.cond((nt & 1) == 1, lambda c: consume(last, 0, c), lambda c: c, carry)

    d_refs = (d0_ref, d1_ref)

    def dots(j, slot):
        ks = rows(j)
        d_ref = d_refs[slot]
        kk = jnp.concatenate([ka_ref[0, ks, :], kb_ref[0, ks, :]], axis=0)
        d_ref[...] = lax.dot_general(kk, qp, NT, preferred_element_type=F32)
        s = lax.dot_general(cn_ref[ks, :], qlb, NT, preferred_element_type=F32)
        s = s + bias_ref[jnp.minimum(i - j, 2)]
        s_ref[j] = s
        mx_ref[j] = s.reshape(KT // 8, 8, hq).max(axis=0)

    def score_keys(j, slot, carry):
        d_ref = d_refs[slot]
        sc = jnp.zeros((KT, QB), F32)
        for h in range(IDX_HEADS):
            d = d_ref[(h % 2) * KT:(h % 2 + 1) * KT, (h // 2) * QB:(h // 2 + 1) * QB]
            sc = sc + jnp.maximum(d, 0.0) * wt[h:h + 1, :]
        sc = jnp.where(inadmissible(j), -jnp.inf, sc)
        bits = pltpu.bitcast(sc, I32)
        key = jnp.where(bits < 0, bits ^ 0x7FFFFFFF, bits)
        key = jnp.where(bits == INT_MIN, 0, key)
        keys_ref[j] = key
        hi_ref[j] = lax.shift_right_arithmetic(key, 16).astype(jnp.int16)
        return carry

    pipelined(dots, score_keys, 0)

    half_min = -2 ** 15

    @pl.when(nt < ntiles)
    def _pad():
        hi_ref[nt] = jnp.full((KT, QB), half_min, jnp.int16)

    def select(cap):
        def count16(pred):
            acc = jnp.zeros((16, QB), jnp.int16)
            for j in range(cap):
                hit = jnp.where(pred(hi_ref[j]), jnp.int16(1), jnp.int16(0))
                parts = [hit[r * 16:(r + 1) * 16, :] for r in range(KT // 16)]
                while len(parts) > 1:
                    parts = [a + b for a, b in zip(parts[::2], parts[1::2])]
                acc = acc + parts[0]
            return acc.astype(I32).sum(axis=0, keepdims=True)

        def bisect16(target):
            def step_fn(step, lo):
                trial = lo + lax.shift_left(jnp.int32(1), 15 - step)
                t16 = trial.astype(jnp.int16)
                return jnp.where(count16(lambda k: k >= t16) >= target, trial, lo)
            return lax.fori_loop(0, 16, step_fn, jnp.full((1, QB), half_min, I32))

        thr_hi = bisect16(topk)

        def low_halves(j, acc):
            key = keys_ref[j]
            hi = lax.shift_right_arithmetic(key, 16)
            lo = (key & 0xFFFF) + half_min
            hi_ref[j] = jnp.where(hi == thr_hi, lo, half_min).astype(jnp.int16)
            above = jnp.where(hi > thr_hi, 1, 0)
            return acc + above.reshape(KT // 8, 8, QB).sum(axis=0)

        above_hi = lax.fori_loop(0, nt, low_halves, jnp.zeros((8, QB), I32))
        above_hi = above_hi.sum(axis=0, keepdims=True)
        thr_lo = bisect16(topk - above_hi)
        thr_lo16 = thr_lo.astype(jnp.int16)
        above = above_hi + count16(lambda k: k > thr_lo16)
        return lax.shift_left(thr_hi, 16) + (thr_lo - half_min), (topk - above).astype(F32)

    caps = sorted(set(min(c, ntiles) for c in range(2, ntiles + 2, 2)))
    thr, need = lax.switch(lax.shift_right_logical(nt - 1, 1),
                           [functools.partial(select, c) for c in caps])

    trow = lax.broadcasted_iota(I32, (KT, KT), 0)
    tcol = lax.broadcasted_iota(I32, (KT, KT), 1)
    tri = jnp.where(tcol < trow, 1.0, 0.0).astype(BF16)

    r_refs = (r0_ref, r1_ref)
    p_refs = (p0_ref, p1_ref)

    def selected(j, key, rank):
        sel = (key > thr) | ((key == thr) & (rank < need))
        return sel & jnp.logical_not(inadmissible(j))

    def weighted(j, slot, carry):
        acc_ref[...] += jnp.dot(ct_ref[j], p_refs[slot][...], preferred_element_type=F32)
        return carry

    def ties_before(j, run_eq):
        re_ref[j] = jnp.broadcast_to(run_eq, (8, QB))
        return run_eq + jnp.where(keys_ref[j] == thr, 1.0, 0.0).sum(axis=0, keepdims=True)

    lax.fori_loop(0, nt, ties_before, jnp.zeros((1, QB), F32))
    m_all = lax.fori_loop(0, nt, lambda j, mx: jnp.maximum(mx, mx_ref[j]),
                          jnp.full((8, hq), NEG, F32)).max(axis=0, keepdims=True)

    def fast_weights(j, slot):
        key = keys_ref[j]
        eqf = jnp.where(key == thr, 1.0, 0.0).astype(BF16)
        rank = jnp.dot(tri, eqf, preferred_element_type=F32) + re_ref[j, 0:1, :]
        keep = jnp.where(selected(j, key, rank), 1.0, 0.0).astype(BF16)
        for h in range(A_HEADS):
            cols = slice(h * QB, (h + 1) * QB)
            p = jnp.exp2(s_ref[j, :, cols] - m_all[:, cols]).astype(BF16)
            p_refs[slot][:, cols] = p * keep

    acc_ref[...] = jnp.zeros(acc_ref.shape, F32)
    pipelined(fast_weights, weighted, 0)

    @pl.when(jnp.min(acc_ref[A_LATENT:A_LATENT + 1, :]) < WEIGHT_SUM_FLOOR)
    def _exact_shift():
        def tie_ranks(j, slot):
            eqf = jnp.where(keys_ref[j] == thr, 1.0, 0.0).astype(BF16)
            r_refs[slot][...] = jnp.dot(tri, eqf, preferred_element_type=F32)

        def mask_tile(j, slot, m8):
            rank = r_refs[slot][...] + re_ref[j, 0:1, :]
            negm = jnp.where(selected(j, keys_ref[j], rank), 0.0, NEG)
            m8_new = []
            for h in range(A_HEADS):
                cols = slice(h * QB, (h + 1) * QB)
                blk = s_ref[j, :, cols] + negm
                s_ref[j, :, cols] = blk
                m8_new.append(jnp.maximum(m8[:, cols], blk.reshape(KT // 8, 8, QB).max(axis=0)))
            return jnp.concatenate(m8_new, axis=1)

        m8 = pipelined(tie_ranks, mask_tile, jnp.full((8, hq), NEG, F32))
        m = m8.max(axis=0, keepdims=True)

        def weights(j, slot):
            p_refs[slot][...] = jnp.exp2(s_ref[j] - m).astype(BF16)

        acc_ref[...] = jnp.zeros(acc_ref.shape, F32)
        pipelined(weights, weighted, 0)

    inv_l = 1.0 / acc_ref[A_LATENT:A_LATENT + 1, :]
    ot = (acc_ref[:A_LATENT, :] * inv_l).astype(BF16)
    for h in range(A_HEADS):
        o_ref[0, h * A_HEAD_DIM:(h + 1) * A_HEAD_DIM, :] = jnp.dot(
            wuvt_ref[h], ot[:, h * QB:(h + 1) * QB], preferred_element_type=F32
        ).astype(o_ref.dtype)


def dsa_attention(proj_h, proj_f, lat_g, w_uk, w_uv, bias, *, topk):
    b, s, _ = proj_h.shape
    ntiles = s // KT
    kern = functools.partial(_dsa_kernel, topk=topk, ntiles=ntiles)
    wuk_t = jnp.swapaxes(w_uk, 1, 2)
    zero = jnp.zeros_like(wuk_t)
    odd = (jnp.arange(A_HEADS) % 2 == 1)[:, None, None]
    wuk_pad = jnp.concatenate([jnp.where(odd, zero, wuk_t), jnp.where(odd, wuk_t, zero)], axis=1)
    col = lambda blk: (lambda bi, i: (bi, 0, blk))
    const = lambda shape: pl.BlockSpec(shape, lambda bi, i: (0,) * len(shape))
    return pl.pallas_call(
        kern,
        out_shape=jax.ShapeDtypeStruct((b, A_HEADS * A_HEAD_DIM, s), BF16),
        grid=(b, s // QB),
        in_specs=[
            pl.BlockSpec((1, QB, 512), lambda bi, i: (bi, i, PROJ_H_QA)),
            pl.BlockSpec((1, QB, 512), lambda bi, i: (bi, i, PROJ_H_QI)),
            pl.BlockSpec((1, QB, LANE), lambda bi, i: (bi, i, PROJ_F_WI)),
            pl.BlockSpec((1, s, LANE), col(PROJ_F_C)),
            pl.BlockSpec((1, s, LANE), col(PROJ_H_KI_LO)),
            pl.BlockSpec((1, s, LANE), col(PROJ_H_KI_HI)),
            const((1, A_LATENT)),
            const((A_HEADS, LANE, A_LATENT)),
            const((A_HEADS, A_HEAD_DIM, A_LATENT)),
            pl.BlockSpec((3, KT, A_HEADS * QB), lambda bi, i: (0, 0, 0),
                         pipeline_mode=pl.Buffered(1)),
        ],
        out_specs=pl.BlockSpec((1, A_HEADS * A_HEAD_DIM, QB), lambda bi, i: (bi, 0, i)),
        scratch_shapes=[
            pltpu.VMEM((s, A_LATENT), BF16),
            pltpu.VMEM((ntiles, ACC_ROWS, KT), BF16),
            pltpu.VMEM((ntiles, KT, QB), I32),
            pltpu.VMEM((ntiles, KT, QB), jnp.int16),
            pltpu.VMEM((ntiles, KT, A_HEADS * QB), F32),
            pltpu.VMEM((ACC_ROWS, A_HEADS * QB), F32),
            pltpu.VMEM((2 * KT, (IDX_HEADS // 2) * QB), F32),
            pltpu.VMEM((2 * KT, (IDX_HEADS // 2) * QB), F32),
            pltpu.VMEM((KT, QB), F32),
            pltpu.VMEM((KT, QB), F32),
            pltpu.VMEM((KT, A_HEADS * QB), BF16),
            pltpu.VMEM((KT, A_HEADS * QB), BF16),
            pltpu.VMEM((ntiles, 8, A_HEADS * QB), F32),
            pltpu.VMEM((ntiles, 8, QB), F32),
        ],
        compiler_params=_params(("parallel", "arbitrary")),
        name="dsa_attention",
    )(proj_h, proj_h, proj_f, proj_f, proj_h, proj_h, lat_g.reshape(1, A_LATENT),
      wuk_pad.astype(BF16), jnp.swapaxes(w_uv, 1, 2).astype(BF16), bias)


def _hgrn_kernel(f_ref, v_ref, q_ref, g_ref, lb_ref, og_ref, o_ref, st_ref, *, nchunk):
    @pl.when(pl.program_id(1) == 0)
    def _():
        st_ref[...] = jnp.zeros(st_ref.shape, F32)

    lb = lb_ref[...]
    og = og_ref[...]
    row = lax.broadcasted_iota(I32, (CHUNK, CHUNK), 0)
    col = lax.broadcasted_iota(I32, (CHUNK, CHUNK), 1)
    causal = col <= row
    tri = jnp.where(causal, 1.0, 0.0).astype(BF16)
    hc = CHUNK // 2
    top = lax.broadcasted_iota(I32, (CHUNK, B_HEADS * B_DK), 0) < hc
    zeros_h = jnp.zeros((hc, B_DK), BF16)

    for c in range(nchunk):
        rows = slice(c * CHUNK, (c + 1) * CHUNK)
        f = lb + (1.0 - lb) * jax.nn.sigmoid(f_ref[0, rows, :])
        g = jnp.log(f)
        k = 1.0 - f
        q = q_ref[0, rows, :]
        qf = q * jax.nn.sigmoid(q)
        vb = v_ref[0, rows, :]
        g_hi = g.astype(BF16)
        g_lo = (g - g_hi.astype(F32)).astype(BF16)
        bc = (jnp.dot(tri, g_hi, preferred_element_type=F32)
              + jnp.dot(tri, g_lo, preferred_element_type=F32))
        b_end = bc[CHUNK - 1:CHUNK, :]
        b_half = bc[hc - 1:hc, :]
        e_in = bc - jnp.where(top, bc[hc // 2 - 1:hc // 2, :], bc[hc + hc // 2 - 1:hc + hc // 2, :])
        q_in = (qf * jnp.exp(e_in)).astype(BF16)
        k_in = (k * jnp.exp(-e_in)).astype(BF16)
        q_x = (qf[hc:, :] * jnp.exp(bc[hc:, :] - b_half)).astype(BF16)
        k_x = (k[:hc, :] * jnp.exp(b_half - bc[:hc, :])).astype(BF16)
        qb = (qf * jnp.exp(bc)).astype(BF16)
        kd = (k * jnp.exp(b_end - bc)).astype(BF16)
        dec = jnp.exp(b_end)
        gate = g_ref[0, rows, :]
        gate = gate * jax.nn.sigmoid(gate)
        for h in range(B_HEADS):
            sl = slice(h * B_DK, (h + 1) * B_DK)
            q3 = jnp.concatenate([
                jnp.concatenate([q_in[:hc, sl], zeros_h, zeros_h], axis=1),
                jnp.concatenate([zeros_h, q_x[:, sl], q_in[hc:, sl]], axis=1)], axis=0)
            k3 = jnp.concatenate([
                jnp.concatenate([k_in[:hc, sl], k_x[:, sl], zeros_h], axis=1),
                jnp.concatenate([zeros_h, zeros_h, k_in[hc:, sl]], axis=1)], axis=0)
            sc = lax.dot_general(q3, k3, NT, preferred_element_type=F32)
            sc = jnp.where(causal, sc, 0.0).astype(BF16)
            st = st_ref[h]
            o = (jnp.dot(sc, vb[:, sl], preferred_element_type=F32)
                 + lax.dot_general(qb[:, sl], st.astype(BF16), NT, preferred_element_type=F32))
            st_ref[h] = st * dec[:, sl] + lax.dot_general(vb[:, sl], kd[:, sl], TN,
                                                          preferred_element_type=F32)
            o_ref[0, rows, sl] = (_rms(o, og) * gate[:, sl]).astype(o_ref.dtype)


def hgrn2(proj_h, proj_f, lb, o_g, *, rows):
    b, s, _ = proj_h.shape
    width = B_HEADS * B_DK
    kern = functools.partial(_hgrn_kernel, nchunk=rows // CHUNK)
    spec = lambda blk: pl.BlockSpec((1, rows, width), lambda bi, r: (bi, r, blk))
    return pl.pallas_call(
        kern,
        out_shape=jax.ShapeDtypeStruct((b, s, width), BF16),
        grid=(b, s // rows),
        in_specs=[spec(PROJ_F_F), spec(PROJ_H_V), spec(PROJ_F_Q), spec(PROJ_F_GATE),
                  pl.BlockSpec((1, width), lambda bi, r: (0, 0)),
                  pl.BlockSpec((1, B_DK), lambda bi, r: (0, 0))],
        out_specs=pl.BlockSpec((1, rows, width), lambda bi, r: (bi, r, 0)),
        scratch_shapes=[pltpu.VMEM((B_HEADS, B_DK, B_DK), F32)],
        compiler_params=_params(("parallel", "arbitrary")),
        name="hgrn2",
    )(proj_f, proj_h, proj_f, proj_f, lb.reshape(1, width), o_g.reshape(1, B_DK))


def _gmlp_kernel(h_ref, g_ref, win_ref, lng_ref, lnb_ref, wsp_ref, bsp_ref, wout_ref, o_ref,
                 gated_ref, *, nchunk, nsplit):
    width = h_ref.shape[2]
    gw = width // GM_GROUPS
    row = lax.broadcasted_iota(I32, (GM_CHUNK, GM_CHUNK), 0)
    col = lax.broadcasted_iota(I32, (GM_CHUNK, GM_CHUNK), 1)
    wgs = [jnp.where(col <= row, wsp_ref[gi], 0.0).astype(BF16) for gi in range(GM_GROUPS)]
    nc = nchunk // nsplit
    xs = [h_ref[0, sp * nc * GM_CHUNK:(sp + 1) * nc * GM_CHUNK, :] for sp in range(nsplit)]
    uvs = [jnp.dot(_rms(x, g_ref[...]).astype(BF16), win_ref[...], preferred_element_type=F32)
           for x in xs]
    for sp in range(nsplit):
        r0 = sp * nc * GM_CHUNK
        x, uv = xs[sp], uvs[sp]
        uv = 0.5 * uv * (1.0 + lax.erf(uv * (0.5 ** 0.5)))
        u = uv[:, :width]
        v = uv[:, width:]
        mu = jnp.mean(v, axis=-1, keepdims=True)
        vc = v - mu
        var = jnp.mean(vc * vc, axis=-1, keepdims=True)
        vn = (vc * lax.rsqrt(var + EPS) * lng_ref[...] + lnb_ref[...]).astype(BF16)
        for gi in range(GM_GROUPS):
            cs = slice(gi * gw, (gi + 1) * gw)
            vg = jnp.concatenate([vn[c * GM_CHUNK:(c + 1) * GM_CHUNK, cs] for c in range(nc)], axis=1)
            mixed = jnp.dot(wgs[gi], vg, preferred_element_type=F32)
            for c in range(nc):
                rs = slice(c * GM_CHUNK, (c + 1) * GM_CHUNK)
                mc = mixed[:, c * gw:(c + 1) * gw] + bsp_ref[:, cs]
                gated_ref[r0 + c * GM_CHUNK:r0 + (c + 1) * GM_CHUNK, cs] = (u[rs, cs] * mc).astype(BF16)
        o_ref[0, r0:r0 + nc * GM_CHUNK, :] = x + jnp.dot(
            gated_ref[r0:r0 + nc * GM_CHUNK, :], wout_ref[...], preferred_element_type=F32)


def gmlp(h, g, w_in, ln_g, ln_b, w_sp, b_sp, w_out, *, rows):
    b, s, d = h.shape
    width = w_out.shape[0]
    gw = width // GM_GROUPS
    bsp_full = jnp.repeat(jnp.transpose(b_sp), gw, axis=1)
    kern = functools.partial(_gmlp_kernel, nchunk=rows // GM_CHUNK, nsplit=4)
    const = lambda shape: pl.BlockSpec(shape, lambda bi, r: (0,) * len(shape),
                                       pipeline_mode=pl.Buffered(1))
    return pl.pallas_call(
        kern,
        out_shape=jax.ShapeDtypeStruct((b, s, d), F32),
        grid=(b, s // rows),
        in_specs=[pl.BlockSpec((1, rows, d), lambda bi, r: (bi, r, 0)),
                  const((1, d)), const((d, 2 * width)), const((1, width)), const((1, width)),
                  const((GM_GROUPS, GM_CHUNK, GM_CHUNK)), const((GM_CHUNK, width)),
                  const((width, d))],
        out_specs=pl.BlockSpec((1, rows, d), lambda bi, r: (bi, r, 0)),
        scratch_shapes=[pltpu.VMEM((rows, width), BF16)],
        compiler_params=_params(("parallel", "parallel")),
        name="gmlp",
    )(h, g.reshape(1, d), w_in.astype(BF16), ln_g.reshape(1, width), ln_b.reshape(1, width),
      w_sp, bsp_full, w_out.astype(BF16))


def _xattn_kernel(*refs, mixed):
    if mixed:
        h_ref, at_ref, b_ref, wa_ref, wb_ref, g_ref, wq_ref, k_ref, v_ref, wo_ref, o_ref = refs
        x = (h_ref[0] + lax.dot_general(at_ref[0], wa_ref[...], TN, preferred_element_type=F32)
             + jnp.dot(b_ref[0], wb_ref[...], preferred_element_type=F32))
    else:
        h_ref, g_ref, wq_ref, k_ref, v_ref, wo_ref, o_ref = refs
        x = h_ref[0]
    xn = _rms(x, g_ref[...]).astype(BF16)
    q = jnp.dot(xn, wq_ref[...], preferred_element_type=F32)
    q = (q * (X_HEAD_DIM ** -0.5 * LOG2E)).astype(BF16)
    ones = jnp.ones((k_ref.shape[1], X_HEAD_DIM), BF16)
    outs = []
    for h in range(X_HEADS):
        sl = slice(h * X_HEAD_DIM, (h + 1) * X_HEAD_DIM)
        s = lax.dot_general(q[:, sl], k_ref[0, :, sl], NT, preferred_element_type=F32)
        p = jnp.exp2(s - s.max(axis=-1, keepdims=True)).astype(BF16)
        ol = jnp.dot(p, jnp.concatenate([v_ref[0, :, sl], ones], axis=1),
                     preferred_element_type=F32)
        outs.append((ol[:, :X_HEAD_DIM] * (1.0 / ol[:, X_HEAD_DIM:X_HEAD_DIM + 1])).astype(BF16))
    att = jnp.concatenate(outs, axis=1)
    o_ref[0] = x + jnp.dot(att, wo_ref[...], preferred_element_type=F32)


def xattn(h, g, wq, kv, wo, layer, *, rows, mix=None):
    b, s, d = h.shape
    m = kv.shape[1]
    xw = wq.shape[2]
    const = lambda shape: pl.BlockSpec(shape, lambda bi, r: (0,) * len(shape))
    of_layer = lambda shape: pl.BlockSpec((None,) + shape, lambda bi, r: (layer, 0, 0))
    rows_spec = lambda width: pl.BlockSpec((1, rows, width), lambda bi, r: (bi, r, 0))
    mix_specs, mix_args = [], []
    if mix is not None:
        a_t, b_out, w_mix = mix
        ka, kb = a_t.shape[1], b_out.shape[2]
        assert ka == kb and w_mix.shape[0] == ka + kb
        mix_specs = [pl.BlockSpec((1, ka, rows), lambda bi, r: (bi, 0, r)), rows_spec(kb),
                     pl.BlockSpec((ka, d), lambda bi, r: (0, 0)),
                     pl.BlockSpec((kb, d), lambda bi, r: (1, 0))]
        mix_args = [a_t, b_out, w_mix, w_mix]
    return pl.pallas_call(
        functools.partial(_xattn_kernel, mixed=mix is not None),
        out_shape=jax.ShapeDtypeStruct((b, s, d), F32),
        grid=(b, s // rows),
        in_specs=[rows_spec(d), *mix_specs,
                  const((1, d)), of_layer((d, xw)),
                  pl.BlockSpec((1, m, xw), lambda bi, r: (bi, 0, 0)),
                  pl.BlockSpec((1, m, xw), lambda bi, r: (bi, 0, 1)),
                  of_layer((xw, d))],
        out_specs=rows_spec(d),
        compiler_params=_params(("parallel", "parallel")),
        name="xattn",
    )(h, *mix_args, g.reshape(1, d), wq, kv, kv, wo)


def _swiglu_kernel(x_ref, g_ref, wgu_ref, wd_ref, fg_ref, o_ref, a_ref, *, ff, fchunk, final):
    x = x_ref[...]
    xn = _rms(x, g_ref[...]).astype(BF16)
    for c in range(ff // fchunk):
        gate = jnp.dot(xn, wgu_ref[:, c * fchunk:(c + 1) * fchunk], preferred_element_type=F32)
        up = jnp.dot(xn, wgu_ref[:, ff + c * fchunk:ff + (c + 1) * fchunk],
                     preferred_element_type=F32)
        a_ref[:, c * fchunk:(c + 1) * fchunk] = (gate * jax.nn.sigmoid(gate) * up).astype(BF16)
    y = x + jnp.dot(a_ref[...], wd_ref[...], preferred_element_type=F32)
    if final:
        y = _rms(y, fg_ref[...])
    o_ref[...] = y


def swiglu(x, g, w_gu, w_down, layer, final_g, *, tm, final):
    n, d = x.shape
    ff = w_down.shape[1]
    kern = functools.partial(_swiglu_kernel, ff=ff, fchunk=256, final=final)
    once = dict(pipeline_mode=pl.Buffered(1))
    return pl.pallas_call(
        kern,
        out_shape=jax.ShapeDtypeStruct((n, d), F32),
        grid=(n // tm,),
        in_specs=[pl.BlockSpec((tm, d), lambda i: (i, 0)),
                  pl.BlockSpec((1, d), lambda i: (0, 0)),
                  pl.BlockSpec((None, d, 2 * ff), lambda i: (layer, 0, 0), **once),
                  pl.BlockSpec((None, ff, d), lambda i: (layer, 0, 0), **once),
                  pl.BlockSpec((1, d), lambda i: (0, 0))],
        out_specs=pl.BlockSpec((tm, d), lambda i: (i, 0)),
        scratch_shapes=[pltpu.VMEM((tm, ff), BF16)],
        compiler_params=_params(("parallel",)),
        name="swiglu",
    )(x, g.reshape(1, d), w_gu, w_down, final_g.reshape(1, d))


def _pack_in_proj(w):
    sizes = [A_HEADS * A_HEAD_DIM, A_LATENT, IDX_HEADS * IDX_DIM, IDX_DIM, IDX_HEADS,
             B_HEADS * B_DK, B_HEADS * B_DK, B_HEADS * B_DK, B_HEADS * B_DK]
    offs = [0]
    for sz in sizes:
        offs.append(offs[-1] + sz)
    q_a, c, qi, ki, wi, f_b, i_b, q_b, g_b = [w[:, offs[n]:offs[n + 1]] for n in range(9)]
    z = lambda n: jnp.zeros((w.shape[0], n), w.dtype)
    return jnp.concatenate([q_a, qi, i_b, ki, z(LANE - IDX_DIM), z(LANE - IDX_DIM), ki,
                            f_b, q_b, g_b, c, wi, z(LANE - IDX_HEADS)], axis=1)


def kernel(x, mem, rel_bias, hgrn_lb, mix_norm, e_w_in, e_lat_norm, e_w_uk, e_w_uv, e_o_norm, e_w_out, o_w_in, o_ln_g, o_ln_b, o_w_sp, o_b_sp, o_w_out, x_norm, mem_norm, x_wq, x_wkv, x_wo, f_norm, f_w_gu, f_w_down, final_norm):
    b, s, d = x.shape
    m = mem.shape[1]
    n = b * s
    depth = mix_norm.shape[0]
    topk = min(TOPK_MAX, s // 4)
    lb_all = jnp.cumsum(jax.nn.softmax(hgrn_lb.astype(F32), axis=0), axis=0)
    bias = bias_tiles(rel_bias)
    mem2 = mem.reshape(b * m, d)
    wq_all, wo_all = x_wq.astype(BF16), x_wo.astype(BF16)
    w_gu_all, w_down_all = f_w_gu.astype(BF16), f_w_down.astype(BF16)

    h = x.reshape(n, d)
    for l in range(depth):
        j = l // 2
        mix = None
        if l % 2 == 0:
            proj_h, proj_f = norm_matmul(h, mix_norm[l], _pack_in_proj(e_w_in[j].astype(BF16)),
                                         tm=1024, tn=256,
                                         groups=((PROJ_H_COLS, BF16), (PROJ_F_COLS, F32)))
            proj_h = proj_h.reshape(b, s, -1)
            proj_f = proj_f.reshape(b, s, -1)
            a_out = dsa_attention(proj_h, proj_f, e_lat_norm[j], e_w_uk[j], e_w_uv[j], bias,
                                  topk=topk)
            b_out = hgrn2(proj_h, proj_f, lb_all[l], e_o_norm[j], rows=512)
            mix = (a_out, b_out, e_w_out[j].astype(BF16))
        else:
            h = gmlp(h.reshape(b, s, d), mix_norm[l], o_w_in[j], o_ln_g[j], o_ln_b[j], o_w_sp[j],
                     o_b_sp[j], o_w_out[j], rows=1024).reshape(n, d)
        (kv,) = norm_matmul(mem2, mem_norm[l], x_wkv[l].astype(BF16), tm=512, tn=512,
                            groups=((x_wkv.shape[2], BF16),))
        kv = kv.reshape(b, m, -1)
        h = xattn(h.reshape(b, s, d), x_norm[l], wq_all, kv, wo_all, l, rows=1024,
                  mix=mix).reshape(n, d)
        h = swiglu(h, f_norm[l], w_gu_all, w_down_all, l, final_norm, tm=1024,
                   final=(l == depth - 1))
    return h.reshape(b, s, d)
```

```python
import functools

import jax
import jax.numpy as jnp
from jax import lax
from jax.experimental import pallas as pl
from jax.experimental.pallas import tpu as pltpu

F32 = jnp.float32
BF16 = jnp.bfloat16
I32 = jnp.int32

EPS = 1e-6
CHUNK = 64
LANE = 128
KT = 256
QB = 256
A_HEADS = 8
A_HEAD_DIM = 64
A_LATENT = 128
IDX_HEADS = 8
IDX_DIM = 64
TOPK_MAX = 256
REL_BUCKETS = 32
B_HEADS = 4
B_DK = 128
GM_CHUNK = 128
GM_GROUPS = 8
X_HEADS = 4
X_HEAD_DIM = 128
NEG = -1e30
INT_MIN = -2**31
LOG2E = 1.4426950408889634
ACC_ROWS = A_LATENT + 16
WEIGHT_SUM_FLOOR = 2.0 ** -60
PROJ_H_COLS = PROJ_F_COLS = 1792
PROJ_H_QA, PROJ_H_QI, PROJ_H_V = 0, 1, 2
PROJ_H_KI_LO, PROJ_H_KI_HI = 12, 13
PROJ_F_F, PROJ_F_Q, PROJ_F_GATE = 0, 1, 2
PROJ_F_C, PROJ_F_WI = 12, 13

VMEM_LIMIT = 56 * 1024 * 1024

NT = (((1,), (1,)), ((), ()))
TN = (((0,), (0,)), ((), ()))


def _rms(x, g):
    ms = jnp.mean(x * x, axis=-1, keepdims=True)
    return x * lax.rsqrt(ms + EPS) * g


def _params(sem, vmem=VMEM_LIMIT):
    return pltpu.CompilerParams(dimension_semantics=sem, vmem_limit_bytes=vmem)


def _norm_matmul_kernel(x_ref, g_ref, w_ref, *o_refs, tn):
    xn = _rms(x_ref[...], g_ref[...]).astype(BF16)
    base = 0
    for o_ref in o_refs:
        for c in range(o_ref.shape[1] // tn):
            y = jnp.dot(xn, w_ref[:, base + c * tn:base + (c + 1) * tn], preferred_element_type=F32)
            o_ref[:, c * tn:(c + 1) * tn] = y.astype(o_ref.dtype)
        base += o_ref.shape[1]


def norm_matmul(x, g, w, *, tm, tn, groups):
    n, k = x.shape
    nout = w.shape[1]
    assert sum(nc for nc, _ in groups) == nout and all(nc % tn == 0 for nc, _ in groups)
    return pl.pallas_call(
        functools.partial(_norm_matmul_kernel, tn=tn),
        out_shape=tuple(jax.ShapeDtypeStruct((n, nc), dt) for nc, dt in groups),
        grid=(n // tm,),
        in_specs=[
            pl.BlockSpec((tm, k), lambda i: (i, 0)),
            pl.BlockSpec((1, k), lambda i: (0, 0)),
            pl.BlockSpec((k, nout), lambda i: (0, 0), pipeline_mode=pl.Buffered(1)),
        ],
        out_specs=tuple(pl.BlockSpec((tm, nc), lambda i: (i, 0)) for nc, _ in groups),
        compiler_params=_params(("parallel",)),
        name="norm_matmul",
    )(x, g.reshape(1, k), w)


def _bias_tiles_kernel(rb_ref, o_ref):
    s = lax.broadcasted_iota(I32, (KT, QB), 0)
    t = lax.broadcasted_iota(I32, (KT, QB), 1)
    nb = REL_BUCKETS // 2
    max_exact = nb // 2
    for kind in range(3):
        rel = s - t - KT * kind
        n = jnp.abs(rel)
        n2 = n * n
        large = jnp.full((KT, QB), max_exact, I32)
        for j in range(1, nb - max_exact):
            large = large + jnp.where(n2 >= (max_exact * max_exact) * (2 ** j), 1, 0)
        bucket = jnp.where(rel > 0, nb, 0) + jnp.where(n < max_exact, n, large)
        buckets = (range(REL_BUCKETS), range(nb), (nb - 1,))[kind]
        hits = [(b, bucket == b) for b in buckets]
        for h in range(A_HEADS):
            acc = jnp.zeros((KT, QB), F32)
            for b, hit in hits:
                acc = jnp.where(hit, rb_ref[b, h] * LOG2E, acc)
            o_ref[kind, :, h * QB:(h + 1) * QB] = acc


def bias_tiles(rel_bias):
    return pl.pallas_call(
        _bias_tiles_kernel,
        out_shape=jax.ShapeDtypeStruct((3, KT, A_HEADS * QB), F32),
        in_specs=[pl.BlockSpec(memory_space=pltpu.SMEM)],
        out_specs=pl.BlockSpec(memory_space=pltpu.VMEM),
        name="bias_tiles",
    )(rel_bias)


def _dsa_kernel(qa_ref, qi_ref, wi_ref, c_ref, ka_ref, kb_ref, latg_ref, wuk_ref, wuvt_ref, bias_ref,
                o_ref, cn_ref, ct_ref, keys_ref, hi_ref, s_ref, acc_ref,
                d0_ref, d1_ref, r0_ref, r1_ref, p0_ref, p1_ref, mx_ref, re_ref, *, topk, ntiles):
    i = pl.program_id(1)
    hq = A_HEADS * QB
    nt = i + 1
    last = i

    @pl.when(i == 0)
    def _prep():
        cn = _rms(c_ref[0], latg_ref[...])
        cn_ref[...] = cn.astype(BF16)
        ones = jnp.ones((ACC_ROWS - A_LATENT, KT), BF16)
        for j in range(ntiles):
            ct_ref[j, :A_LATENT, :] = cn[j * KT:(j + 1) * KT, :].T.astype(BF16)
            ct_ref[j, A_LATENT:, :] = ones

    qa = qa_ref[0]
    ql = jnp.concatenate(
        [jnp.dot(qa[:, (h // 2) * LANE:(h // 2 + 1) * LANE], wuk_ref[h],
                 preferred_element_type=F32) for h in range(A_HEADS)], axis=0)
    qlb = (ql * (A_HEAD_DIM ** -0.5 * LOG2E)).astype(BF16)
    qi = qi_ref[0]
    qp = jnp.concatenate([qi[:, p * LANE:(p + 1) * LANE] for p in range(IDX_HEADS // 2)], axis=0)
    w = wi_ref[0] * (IDX_HEADS ** -0.5) * (IDX_DIM ** -0.5)
    wt = w.T

    row = lax.broadcasted_iota(I32, (KT, QB), 0)
    col = lax.broadcasted_iota(I32, (KT, QB), 1)
    chunk_bits = CHUNK.bit_length() - 1
    inadm_last = jnp.where(lax.shift_right_logical(row, chunk_bits)
                           > lax.shift_right_logical(col, chunk_bits), 1, 0)

    def inadmissible(j):
        return (inadm_last * jnp.where(j == last, 1, 0)) != 0

    def rows(j):
        return pl.ds(pl.multiple_of(j * KT, KT), KT)

    def pipelined(prepare, consume, carry):
        prepare(0, 0)

        def pair(t, carry):
            j = 2 * t
            prepare(j + 1, 1)
            carry = consume(j, 0, carry)
            prepare(jnp.minimum(j + 2, last), 0)
            return consume(j + 1, 1, carry)

        carry = lax.fori_loop(0, lax.shift_right_logical(nt, 1), pair, carry)
        return lax.cond((nt & 1) == 1, lambda c: consume(last, 0, c), lambda c: c, carry)

    d_refs = (d0_ref, d1_ref)

    def dots(j, slot):
        ks = rows(j)
        d_ref = d_refs[slot]
        kk = jnp.concatenate([ka_ref[0, ks, :], kb_ref[0, ks, :]], axis=0)
        d_ref[...] = lax.dot_general(kk, qp, NT, preferred_element_type=F32)
        s = lax.dot_general(cn_ref[ks, :], qlb, NT, preferred_element_type=F32)
        s = s + bias_ref[jnp.minimum(i - j, 2)]
        s_ref[j] = s
        mx_ref[j] = s.reshape(KT // 8, 8, hq).max(axis=0)

    def score_keys(j, slot, carry):
        d_ref = d_refs[slot]
        sc = jnp.zeros((KT, QB), F32)
        for h in range(IDX_HEADS):
            d = d_ref[(h % 2) * KT:(h % 2 + 1) * KT, (h // 2) * QB:(h // 2 + 1) * QB]
            sc = sc + jnp.maximum(d, 0.0) * wt[h:h + 1, :]
        sc = jnp.where(inadmissible(j), -jnp.inf, sc)
        bits = pltpu.bitcast(sc, I32)
        key = jnp.where(bits < 0, bits ^ 0x7FFFFFFF, bits)
        key = jnp.where(bits == INT_MIN, 0, key)
        keys_ref[j] = key
        hi_ref[j] = lax.shift_right_arithmetic(key, 16).astype(jnp.int16)
        return carry

    pipelined(dots, score_keys, 0)

    half_min = -2 ** 15

    def select(cap):
        if cap * KT <= topk:
            return jnp.full((1, QB), INT_MIN, I32), jnp.zeros((1, QB), F32)

        def count16(pred):
            acc = jnp.zeros((16, QB), jnp.int16)
            for j in range(cap):
                hit = jnp.where(pred(hi_ref[j]), jnp.int16(1), jnp.int16(0))
                parts = [hit[r * 16:(r + 1) * 16, :] for r in range(KT // 16)]
                while len(parts) > 1:
                    parts = [a + b for a, b in zip(parts[::2], parts[1::2])]
                acc = acc + parts[0]
            return acc.astype(I32).sum(axis=0, keepdims=True)

        def bisect16(target):
            def step_fn(step, lo):
                trial = lo + lax.shift_left(jnp.int32(1), 15 - step)
                t16 = trial.astype(jnp.int16)
                return jnp.where(count16(lambda k: k >= t16) >= target, trial, lo)
            return lax.fori_loop(0, 16, step_fn, jnp.full((1, QB), half_min, I32))

        thr_hi = bisect16(topk)

        def low_halves(j, acc):
            key = keys_ref[j]
            hi = lax.shift_right_arithmetic(key, 16)
            lo = (key & 0xFFFF) + half_min
            hi_ref[j] = jnp.where(hi == thr_hi, lo, half_min).astype(jnp.int16)
            above = jnp.where(hi > thr_hi, 1, 0)
            return acc + above.reshape(KT // 8, 8, QB).sum(axis=0)

        above_hi = lax.fori_loop(0, nt, low_halves, jnp.zeros((8, QB), I32))
        above_hi = above_hi.sum(axis=0, keepdims=True)
        thr_lo = bisect16(topk - above_hi)
        thr_lo16 = thr_lo.astype(jnp.int16)
        above = above_hi + count16(lambda k: k > thr_lo16)
        return lax.shift_left(thr_hi, 16) + (thr_lo - half_min), (topk - above).astype(F32)

    thr, need = lax.switch(nt - 1, [functools.partial(select, c) for c in range(1, ntiles + 1)])

    trow = lax.broadcasted_iota(I32, (KT, KT), 0)
    tcol = lax.broadcasted_iota(I32, (KT, KT), 1)
    tri = jnp.where(tcol < trow, 1.0, 0.0).astype(BF16)

    r_refs = (r0_ref, r1_ref)
    p_refs = (p0_ref, p1_ref)

    def selected(j, key, rank):
        sel = (key > thr) | ((key == thr) & (rank < need))
        return sel & jnp.logical_not(inadmissible(j))

    def weighted(j, slot, carry):
        acc_ref[...] += jnp.dot(ct_ref[j], p_refs[slot][...], preferred_element_type=F32)
        return carry

    def ties_before(j, run_eq):
        re_ref[j] = jnp.broadcast_to(run_eq, (8, QB))
        return run_eq + jnp.where(keys_ref[j] == thr, 1.0, 0.0).sum(axis=0, keepdims=True)

    lax.fori_loop(0, nt, ties_before, jnp.zeros((1, QB), F32))
    m_all = lax.fori_loop(0, nt, lambda j, mx: jnp.maximum(mx, mx_ref[j]),
                          jnp.full((8, hq), NEG, F32)).max(axis=0, keepdims=True)

    def fast_weights(j, slot):
        key = keys_ref[j]
        eqf = jnp.where(key == thr, 1.0, 0.0).astype(BF16)
        rank = jnp.dot(tri, eqf, preferred_element_type=F32) + re_ref[j, 0:1, :]
        keep = jnp.where(selected(j, key, rank), 1.0, 0.0).astype(BF16)
        for h in range(A_HEADS):
            cols = slice(h * QB, (h + 1) * QB)
            p = jnp.exp2(s_ref[j, :, cols] - m_all[:, cols]).astype(BF16)
            p_refs[slot][:, cols] = p * keep

    acc_ref[...] = jnp.zeros(acc_ref.shape, F32)
    pipelined(fast_weights, weighted, 0)

    @pl.when(jnp.min(acc_ref[A_LATENT:A_LATENT + 1, :]) < WEIGHT_SUM_FLOOR)
    def _exact_shift():
        def tie_ranks(j, slot):
            eqf = jnp.where(keys_ref[j] == thr, 1.0, 0.0).astype(BF16)
            r_refs[slot][...] = jnp.dot(tri, eqf, preferred_element_type=F32)

        def mask_tile(j, slot, m8):
            rank = r_refs[slot][...] + re_ref[j, 0:1, :]
            negm = jnp.where(selected(j, keys_ref[j], rank), 0.0, NEG)
            m8_new = []
            for h in range(A_HEADS):
                cols = slice(h * QB, (h + 1) * QB)
                blk = s_ref[j, :, cols] + negm
                s_ref[j, :, cols] = blk
                m8_new.append(jnp.maximum(m8[:, cols], blk.reshape(KT // 8, 8, QB).max(axis=0)))
            return jnp.concatenate(m8_new, axis=1)

        m8 = pipelined(tie_ranks, mask_tile, jnp.full((8, hq), NEG, F32))
        m = m8.max(axis=0, keepdims=True)

        def weights(j, slot):
            p_refs[slot][...] = jnp.exp2(s_ref[j] - m).astype(BF16)

        acc_ref[...] = jnp.zeros(acc_ref.shape, F32)
        pipelined(weights, weighted, 0)

    inv_l = 1.0 / acc_ref[A_LATENT:A_LATENT + 1, :]
    ot = (acc_ref[:A_LATENT, :] * inv_l).astype(BF16)
    for h in range(A_HEADS):
        o_ref[0, h * A_HEAD_DIM:(h + 1) * A_HEAD_DIM, :] = jnp.dot(
            wuvt_ref[h], ot[:, h * QB:(h + 1) * QB], preferred_element_type=F32
        ).astype(o_ref.dtype)


def dsa_attention(proj_h, proj_f, lat_g, w_uk, w_uv, bias, *, topk):
    b, s, _ = proj_h.shape
    ntiles = s // KT
    kern = functools.partial(_dsa_kernel, topk=topk, ntiles=ntiles)
    wuk_t = jnp.swapaxes(w_uk, 1, 2)
    zero = jnp.zeros_like(wuk_t)
    odd = (jnp.arange(A_HEADS) % 2 == 1)[:, None, None]
    wuk_pad = jnp.concatenate([jnp.where(odd, zero, wuk_t), jnp.where(odd, wuk_t, zero)], axis=1)
    col = lambda blk: (lambda bi, i: (bi, 0, blk))
    const = lambda shape: pl.BlockSpec(shape, lambda bi, i: (0,) * len(shape))
    return pl.pallas_call(
        kern,
        out_shape=jax.ShapeDtypeStruct((b, A_HEADS * A_HEAD_DIM, s), BF16),
        grid=(b, s // QB),
        in_specs=[
            pl.BlockSpec((1, QB, 512), lambda bi, i: (bi, i, PROJ_H_QA)),
            pl.BlockSpec((1, QB, 512), lambda bi, i: (bi, i, PROJ_H_QI)),
            pl.BlockSpec((1, QB, LANE), lambda bi, i: (bi, i, PROJ_F_WI)),
            pl.BlockSpec((1, s, LANE), col(PROJ_F_C)),
            pl.BlockSpec((1, s, LANE), col(PROJ_H_KI_LO)),
            pl.BlockSpec((1, s, LANE), col(PROJ_H_KI_HI)),
            const((1, A_LATENT)),
            const((A_HEADS, LANE, A_LATENT)),
            const((A_HEADS, A_HEAD_DIM, A_LATENT)),
            pl.BlockSpec((3, KT, A_HEADS * QB), lambda bi, i: (0, 0, 0),
                         pipeline_mode=pl.Buffered(1)),
        ],
        out_specs=pl.BlockSpec((1, A_HEADS * A_HEAD_DIM, QB), lambda bi, i: (bi, 0, i)),
        scratch_shapes=[
            pltpu.VMEM((s, A_LATENT), BF16),
            pltpu.VMEM((ntiles, ACC_ROWS, KT), BF16),
            pltpu.VMEM((ntiles, KT, QB), I32),
            pltpu.VMEM((ntiles, KT, QB), jnp.int16),
            pltpu.VMEM((ntiles, KT, A_HEADS * QB), F32),
            pltpu.VMEM((ACC_ROWS, A_HEADS * QB), F32),
            pltpu.VMEM((2 * KT, (IDX_HEADS // 2) * QB), F32),
            pltpu.VMEM((2 * KT, (IDX_HEADS // 2) * QB), F32),
            pltpu.VMEM((KT, QB), F32),
            pltpu.VMEM((KT, QB), F32),
            pltpu.VMEM((KT, A_HEADS * QB), BF16),
            pltpu.VMEM((KT, A_HEADS * QB), BF16),
            pltpu.VMEM((ntiles, 8, A_HEADS * QB), F32),
            pltpu.VMEM((ntiles, 8, QB), F32),
        ],
        compiler_params=_params(("parallel", "arbitrary")),
        name="dsa_attention",
    )(proj_h, proj_h, proj_f, proj_f, proj_h, proj_h, lat_g.reshape(1, A_LATENT),
      wuk_pad.astype(BF16), jnp.swapaxes(w_uv, 1, 2).astype(BF16), bias)


def _hgrn_kernel(f_ref, v_ref, q_ref, g_ref, lb_ref, og_ref, o_ref, st_ref, *, nchunk):
    @pl.when(pl.program_id(1) == 0)
    def _():
        st_ref[...] = jnp.zeros(st_ref.shape, F32)

    lb = lb_ref[...]
    og = og_ref[...]
    row = lax.broadcasted_iota(I32, (CHUNK, CHUNK), 0)
    col = lax.broadcasted_iota(I32, (CHUNK, CHUNK), 1)
    causal = col <= row
    tri = jnp.where(causal, 1.0, 0.0).astype(BF16)
    hc = CHUNK // 2
    top = lax.broadcasted_iota(I32, (CHUNK, B_HEADS * B_DK), 0) < hc
    zeros_h = jnp.zeros((hc, B_DK), BF16)

    for c in range(nchunk):
        rows = slice(c * CHUNK, (c + 1) * CHUNK)
        f = lb + (1.0 - lb) * jax.nn.sigmoid(f_ref[0, rows, :])
        g = jnp.log(f)
        k = 1.0 - f
        q = q_ref[0, rows, :]
        qf = q * jax.nn.sigmoid(q)
        vb = v_ref[0, rows, :]
        g_hi = g.astype(BF16)
        g_lo = (g - g_hi.astype(F32)).astype(BF16)
        bc = (jnp.dot(tri, g_hi, preferred_element_type=F32)
              + jnp.dot(tri, g_lo, preferred_element_type=F32))
        b_end = bc[CHUNK - 1:CHUNK, :]
        b_half = bc[hc - 1:hc, :]
        e_in = bc - jnp.where(top, bc[hc // 2 - 1:hc // 2, :], bc[hc + hc // 2 - 1:hc + hc // 2, :])
        q_in = (qf * jnp.exp(e_in)).astype(BF16)
        k_in = (k * jnp.exp(-e_in)).astype(BF16)
        q_x = (qf[hc:, :] * jnp.exp(bc[hc:, :] - b_half)).astype(BF16)
        k_x = (k[:hc, :] * jnp.exp(b_half - bc[:hc, :])).astype(BF16)
        qb = (qf * jnp.exp(bc)).astype(BF16)
        kd = (k * jnp.exp(b_end - bc)).astype(BF16)
        dec = jnp.exp(b_end)
        gate = g_ref[0, rows, :]
        gate = gate * jax.nn.sigmoid(gate)
        for h in range(B_HEADS):
            sl = slice(h * B_DK, (h + 1) * B_DK)
            q3 = jnp.concatenate([
                jnp.concatenate([q_in[:hc, sl], zeros_h, zeros_h], axis=1),
                jnp.concatenate([zeros_h, q_x[:, sl], q_in[hc:, sl]], axis=1)], axis=0)
            k3 = jnp.concatenate([
                jnp.concatenate([k_in[:hc, sl], k_x[:, sl], zeros_h], axis=1),
                jnp.concatenate([zeros_h, zeros_h, k_in[hc:, sl]], axis=1)], axis=0)
            sc = lax.dot_general(q3, k3, NT, preferred_element_type=F32)
            sc = jnp.where(causal, sc, 0.0).astype(BF16)
            st = st_ref[h]
            o = (jnp.dot(sc, vb[:, sl], preferred_element_type=F32)
                 + lax.dot_general(qb[:, sl], st.astype(BF16), NT, preferred_element_type=F32))
            st_ref[h] = st * dec[:, sl] + lax.dot_general(vb[:, sl], kd[:, sl], TN,
                                                          preferred_element_type=F32)
            o_ref[0, rows, sl] = (_rms(o, og) * gate[:, sl]).astype(o_ref.dtype)


def hgrn2(proj_h, proj_f, lb, o_g, *, rows):
    b, s, _ = proj_h.shape
    width = B_HEADS * B_DK
    kern = functools.partial(_hgrn_kernel, nchunk=rows // CHUNK)
    spec = lambda blk: pl.BlockSpec((1, rows, width), lambda bi, r: (bi, r, blk))
    return pl.pallas_call(
        kern,
        out_shape=jax.ShapeDtypeStruct((b, s, width), BF16),
        grid=(b, s // rows),
        in_specs=[spec(PROJ_F_F), spec(PROJ_H_V), spec(PROJ_F_Q), spec(PROJ_F_GATE),
                  pl.BlockSpec((1, width), lambda bi, r: (0, 0)),
                  pl.BlockSpec((1, B_DK), lambda bi, r: (0, 0))],
        out_specs=pl.BlockSpec((1, rows, width), lambda bi, r: (bi, r, 0)),
        scratch_shapes=[pltpu.VMEM((B_HEADS, B_DK, B_DK), F32)],
        compiler_params=_params(("parallel", "arbitrary")),
        name="hgrn2",
    )(proj_f, proj_h, proj_f, proj_f, lb.reshape(1, width), o_g.reshape(1, B_DK))


def _gmlp_kernel(h_ref, g_ref, win_ref, lng_ref, lnb_ref, wsp_ref, bsp_ref, wout_ref, o_ref,
                 gated_ref, *, nchunk, nsplit):
    width = h_ref.shape[2]
    gw = width // GM_GROUPS
    row = lax.broadcasted_iota(I32, (GM_CHUNK, GM_CHUNK), 0)
    col = lax.broadcasted_iota(I32, (GM_CHUNK, GM_CHUNK), 1)
    wgs = [jnp.where(col <= row, wsp_ref[gi], 0.0).astype(BF16) for gi in range(GM_GROUPS)]
    nc = nchunk // nsplit
    xs = [h_ref[0, sp * nc * GM_CHUNK:(sp + 1) * nc * GM_CHUNK, :] for sp in range(nsplit)]
    uvs = [jnp.dot(_rms(x, g_ref[...]).astype(BF16), win_ref[...], preferred_element_type=F32)
           for x in xs]
    for sp in range(nsplit):
        r0 = sp * nc * GM_CHUNK
        x, uv = xs[sp], uvs[sp]
        uv = 0.5 * uv * (1.0 + lax.erf(uv * (0.5 ** 0.5)))
        u = uv[:, :width]
        v = uv[:, width:]
        mu = jnp.mean(v, axis=-1, keepdims=True)
        vc = v - mu
        var = jnp.mean(vc * vc, axis=-1, keepdims=True)
        vn = (vc * lax.rsqrt(var + EPS) * lng_ref[...] + lnb_ref[...]).astype(BF16)
        for gi in range(GM_GROUPS):
            cs = slice(gi * gw, (gi + 1) * gw)
            vg = jnp.concatenate([vn[c * GM_CHUNK:(c + 1) * GM_CHUNK, cs] for c in range(nc)], axis=1)
            mixed = jnp.dot(wgs[gi], vg, preferred_element_type=F32)
            for c in range(nc):
                rs = slice(c * GM_CHUNK, (c + 1) * GM_CHUNK)
                mc = mixed[:, c * gw:(c + 1) * gw] + bsp_ref[:, cs]
                gated_ref[r0 + c * GM_CHUNK:r0 + (c + 1) * GM_CHUNK, cs] = (u[rs, cs] * mc).astype(BF16)
        o_ref[0, r0:r0 + nc * GM_CHUNK, :] = x + jnp.dot(
            gated_ref[r0:r0 + nc * GM_CHUNK, :], wout_ref[...], preferred_element_type=F32)


def gmlp(h, g, w_in, ln_g, ln_b, w_sp, b_sp, w_out, *, rows):
    b, s, d = h.shape
    width = w_out.shape[0]
    gw = width // GM_GROUPS
    bsp_full = jnp.repeat(jnp.transpose(b_sp), gw, axis=1)
    kern = functools.partial(_gmlp_kernel, nchunk=rows // GM_CHUNK, nsplit=4)
    const = lambda shape: pl.BlockSpec(shape, lambda bi, r: (0,) * len(shape),
                                       pipeline_mode=pl.Buffered(1))
    return pl.pallas_call(
        kern,
        out_shape=jax.ShapeDtypeStruct((b, s, d), F32),
        grid=(b, s // rows),
        in_specs=[pl.BlockSpec((1, rows, d), lambda bi, r: (bi, r, 0)),
                  const((1, d)), const((d, 2 * width)), const((1, width)), const((1, width)),
                  const((GM_GROUPS, GM_CHUNK, GM_CHUNK)), const((GM_CHUNK, width)),
                  const((width, d))],
        out_specs=pl.BlockSpec((1, rows, d), lambda bi, r: (bi, r, 0)),
        scratch_shapes=[pltpu.VMEM((rows, width), BF16)],
        compiler_params=_params(("parallel", "parallel")),
        name="gmlp",
    )(h, g.reshape(1, d), w_in.astype(BF16), ln_g.reshape(1, width), ln_b.reshape(1, width),
      w_sp, bsp_full, w_out.astype(BF16))


def _xattn_kernel(*refs, mixed):
    if mixed:
        h_ref, at_ref, b_ref, wa_ref, wb_ref, g_ref, wq_ref, k_ref, v_ref, wo_ref, o_ref = refs
        x = (h_ref[0] + lax.dot_general(at_ref[0], wa_ref[...], TN, preferred_element_type=F32)
             + jnp.dot(b_ref[0], wb_ref[...], preferred_element_type=F32))
    else:
        h_ref, g_ref, wq_ref, k_ref, v_ref, wo_ref, o_ref = refs
        x = h_ref[0]
    xn = _rms(x, g_ref[...]).astype(BF16)
    q = jnp.dot(xn, wq_ref[...], preferred_element_type=F32)
    q = (q * (X_HEAD_DIM ** -0.5 * LOG2E)).astype(BF16)
    ones = jnp.ones((k_ref.shape[1], X_HEAD_DIM), BF16)
    outs = []
    for h in range(X_HEADS):
        sl = slice(h * X_HEAD_DIM, (h + 1) * X_HEAD_DIM)
        s = lax.dot_general(q[:, sl], k_ref[0, :, sl], NT, preferred_element_type=F32)
        p = jnp.exp2(s - s.max(axis=-1, keepdims=True)).astype(BF16)
        ol = jnp.dot(p, jnp.concatenate([v_ref[0, :, sl], ones], axis=1),
                     preferred_element_type=F32)
        outs.append((ol[:, :X_HEAD_DIM] * (1.0 / ol[:, X_HEAD_DIM:X_HEAD_DIM + 1])).astype(BF16))
    att = jnp.concatenate(outs, axis=1)
    o_ref[0] = x + jnp.dot(att, wo_ref[...], preferred_element_type=F32)


def xattn(h, g, wq, kv, wo, layer, *, rows, mix=None):
    b, s, d = h.shape
    m = kv.shape[1]
    xw = wq.shape[2]
    const = lambda shape: pl.BlockSpec(shape, lambda bi, r: (0,) * len(shape))
    of_layer = lambda shape: pl.BlockSpec((None,) + shape, lambda bi, r: (layer, 0, 0))
    rows_spec = lambda width: pl.BlockSpec((1, rows, width), lambda bi, r: (bi, r, 0))
    mix_specs, mix_args = [], []
    if mix is not None:
        a_t, b_out, w_mix = mix
        ka, kb = a_t.shape[1], b_out.shape[2]
        assert ka == kb and w_mix.shape[0] == ka + kb
        mix_specs = [pl.BlockSpec((1, ka, rows), lambda bi, r: (bi, 0, r)), rows_spec(kb),
                     pl.BlockSpec((ka, d), lambda bi, r: (0, 0)),
                     pl.BlockSpec((kb, d), lambda bi, r: (1, 0))]
        mix_args = [a_t, b_out, w_mix, w_mix]
    return pl.pallas_call(
        functools.partial(_xattn_kernel, mixed=mix is not None),
        out_shape=jax.ShapeDtypeStruct((b, s, d), F32),
        grid=(b, s // rows),
        in_specs=[rows_spec(d), *mix_specs,
                  const((1, d)), of_layer((d, xw)),
                  pl.BlockSpec((1, m, xw), lambda bi, r: (bi, 0, 0)),
                  pl.BlockSpec((1, m, xw), lambda bi, r: (bi, 0, 1)),
                  of_layer((xw, d))],
        out_specs=rows_spec(d),
        compiler_params=_params(("parallel", "parallel")),
        name="xattn",
    )(h, *mix_args, g.reshape(1, d), wq, kv, kv, wo)


def _swiglu_kernel(x_ref, g_ref, wgu_ref, wd_ref, fg_ref, o_ref, a_ref, *, ff, fchunk, final):
    x = x_ref[...]
    xn = _rms(x, g_ref[...]).astype(BF16)
    for c in range(ff // fchunk):
        gate = jnp.dot(xn, wgu_ref[:, c * fchunk:(c + 1) * fchunk], preferred_element_type=F32)
        up = jnp.dot(xn, wgu_ref[:, ff + c * fchunk:ff + (c + 1) * fchunk],
                     preferred_element_type=F32)
        a_ref[:, c * fchunk:(c + 1) * fchunk] = (gate * jax.nn.sigmoid(gate) * up).astype(BF16)
    y = x + jnp.dot(a_ref[...], wd_ref[...], preferred_element_type=F32)
    if final:
        y = _rms(y, fg_ref[...])
    o_ref[...] = y


def swiglu(x, g, w_gu, w_down, layer, final_g, *, tm, final):
    n, d = x.shape
    ff = w_down.shape[1]
    kern = functools.partial(_swiglu_kernel, ff=ff, fchunk=256, final=final)
    once = dict(pipeline_mode=pl.Buffered(1))
    return pl.pallas_call(
        kern,
        out_shape=jax.ShapeDtypeStruct((n, d), F32),
        grid=(n // tm,),
        in_specs=[pl.BlockSpec((tm, d), lambda i: (i, 0)),
                  pl.BlockSpec((1, d), lambda i: (0, 0)),
                  pl.BlockSpec((None, d, 2 * ff), lambda i: (layer, 0, 0), **once),
                  pl.BlockSpec((None, ff, d), lambda i: (layer, 0, 0), **once),
                  pl.BlockSpec((1, d), lambda i: (0, 0))],
        out_specs=pl.BlockSpec((tm, d), lambda i: (i, 0)),
        scratch_shapes=[pltpu.VMEM((tm, ff), BF16)],
        compiler_params=_params(("parallel",)),
        name="swiglu",
    )(x, g.reshape(1, d), w_gu, w_down, final_g.reshape(1, d))


def _pack_in_proj(w):
    sizes = [A_HEADS * A_HEAD_DIM, A_LATENT, IDX_HEADS * IDX_DIM, IDX_DIM, IDX_HEADS,
             B_HEADS * B_DK, B_HEADS * B_DK, B_HEADS * B_DK, B_HEADS * B_DK]
    offs = [0]
    for sz in sizes:
        offs.append(offs[-1] + sz)
    q_a, c, qi, ki, wi, f_b, i_b, q_b, g_b = [w[:, offs[n]:offs[n + 1]] for n in range(9)]
    z = lambda n: jnp.zeros((w.shape[0], n), w.dtype)
    return jnp.concatenate([q_a, qi, i_b, ki, z(LANE - IDX_DIM), z(LANE - IDX_DIM), ki,
                            f_b, q_b, g_b, c, wi, z(LANE - IDX_HEADS)], axis=1)


def kernel(x, mem, rel_bias, hgrn_lb, mix_norm, e_w_in, e_lat_norm, e_w_uk, e_w_uv, e_o_norm, e_w_out, o_w_in, o_ln_g, o_ln_b, o_w_sp, o_b_sp, o_w_out, x_norm, mem_norm, x_wq, x_wkv, x_wo, f_norm, f_w_gu, f_w_down, final_norm):
    b, s, d = x.shape
    m = mem.shape[1]
    n = b * s
    depth = mix_norm.shape[0]
    topk = min(TOPK_MAX, s // 4)
    lb_all = jnp.cumsum(jax.nn.softmax(hgrn_lb.astype(F32), axis=0), axis=0)
    bias = bias_tiles(rel_bias)
    mem2 = mem.reshape(b * m, d)
    wq_all, wo_all = x_wq.astype(BF16), x_wo.astype(BF16)
    w_gu_all, w_down_all = f_w_gu.astype(BF16), f_w_down.astype(BF16)

    h = x.reshape(n, d)
    for l in range(depth):
        j = l // 2
        mix = None
        if l % 2 == 0:
            proj_h, proj_f = norm_matmul(h, mix_norm[l], _pack_in_proj(e_w_in[j].astype(BF16)),
                                         tm=1024, tn=256,
                                         groups=((PROJ_H_COLS, BF16), (PROJ_F_COLS, F32)))
            proj_h = proj_h.reshape(b, s, -1)
            proj_f = proj_f.reshape(b, s, -1)
            a_out = dsa_attention(proj_h, proj_f, e_lat_norm[j], e_w_uk[j], e_w_uv[j], bias,
                                  topk=topk)
            b_out = hgrn2(proj_h, proj_f, lb_all[l], e_o_norm[j], rows=512)
            mix = (a_out, b_out, e_w_out[j].astype(BF16))
        else:
            h = gmlp(h.reshape(b, s, d), mix_norm[l], o_w_in[j], o_ln_g[j], o_ln_b[j], o_w_sp[j],
                     o_b_sp[j], o_w_out[j], rows=1024).reshape(n, d)
        (kv,) = norm_matmul(mem2, mem_norm[l], x_wkv[l].astype(BF16), tm=512, tn=512,
                            groups=((x_wkv.shape[2], BF16),))
        kv = kv.reshape(b, m, -1)
        h = xattn(h.reshape(b, s, d), x_norm[l], wq_all, kv, wo_all, l, rows=1024,
                  mix=mix).reshape(n, d)
        h = swiglu(h, f_norm[l], w_gu_all, w_down_all, l, final_norm, tm=1024,
                   final=(l == depth - 1))
    return h.reshape(b, s, d)
```

```python
import functools

import jax
import jax.numpy as jnp
from jax import lax
from jax.experimental import pallas as pl
from jax.experimental.pallas import tpu as pltpu

F32 = jnp.float32
BF16 = jnp.bfloat16
I32 = jnp.int32

EPS = 1e-6
CHUNK = 64
LANE = 128
KT = 256
QB = 256
A_HEADS = 8
A_HEAD_DIM = 64
A_LATENT = 128
IDX_HEADS = 8
IDX_DIM = 64
TOPK_MAX = 256
REL_BUCKETS = 32
B_HEADS = 4
B_DK = 128
GM_CHUNK = 128
GM_GROUPS = 8
X_HEADS = 4
X_HEAD_DIM = 128
NEG = -1e30
INT_MIN = -2**31
LOG2E = 1.4426950408889634
ACC_ROWS = A_LATENT + 16
WEIGHT_SUM_FLOOR = 2.0 ** -60
PROJ_H_COLS = PROJ_F_COLS = 1792
PROJ_H_QA, PROJ_H_QI, PROJ_H_V = 0, 1, 2
PROJ_H_KI_LO, PROJ_H_KI_HI = 12, 13
PROJ_F_F, PROJ_F_Q, PROJ_F_GATE = 0, 1, 2
PROJ_F_C, PROJ_F_WI = 12, 13

VMEM_LIMIT = 56 * 1024 * 1024

NT = (((1,), (1,)), ((), ()))
TN = (((0,), (0,)), ((), ()))


def _rms(x, g):
    ms = jnp.mean(x * x, axis=-1, keepdims=True)
    return x * lax.rsqrt(ms + EPS) * g


def _params(sem, vmem=VMEM_LIMIT):
    return pltpu.CompilerParams(dimension_semantics=sem, vmem_limit_bytes=vmem)


def _norm_matmul_kernel(x_ref, g_ref, w_ref, *o_refs, tn):
    xn = _rms(x_ref[...], g_ref[...]).astype(BF16)
    base = 0
    for o_ref in o_refs:
        for c in range(o_ref.shape[1] // tn):
            y = jnp.dot(xn, w_ref[:, base + c * tn:base + (c + 1) * tn], preferred_element_type=F32)
            o_ref[:, c * tn:(c + 1) * tn] = y.astype(o_ref.dtype)
        base += o_ref.shape[1]


def norm_matmul(x, g, w, *, tm, tn, groups):
    n, k = x.shape
    nout = w.shape[1]
    assert sum(nc for nc, _ in groups) == nout and all(nc % tn == 0 for nc, _ in groups)
    return pl.pallas_call(
        functools.partial(_norm_matmul_kernel, tn=tn),
        out_shape=tuple(jax.ShapeDtypeStruct((n, nc), dt) for nc, dt in groups),
        grid=(n // tm,),
        in_specs=[
            pl.BlockSpec((tm, k), lambda i: (i, 0)),
            pl.BlockSpec((1, k), lambda i: (0, 0)),
            pl.BlockSpec((k, nout), lambda i: (0, 0), pipeline_mode=pl.Buffered(1)),
        ],
        out_specs=tuple(pl.BlockSpec((tm, nc), lambda i: (i, 0)) for nc, _ in groups),
        compiler_params=_params(("parallel",)),
        name="norm_matmul",
    )(x, g.reshape(1, k), w)


def _bias_tiles_kernel(rb_ref, o_ref):
    s = lax.broadcasted_iota(I32, (KT, QB), 0)
    t = lax.broadcasted_iota(I32, (KT, QB), 1)
    nb = REL_BUCKETS // 2
    max_exact = nb // 2
    for kind in range(3):
        rel = s - t - KT * kind
        n = jnp.abs(rel)
        n2 = n * n
        large = jnp.full((KT, QB), max_exact, I32)
        for j in range(1, nb - max_exact):
            large = large + jnp.where(n2 >= (max_exact * max_exact) * (2 ** j), 1, 0)
        bucket = jnp.where(rel > 0, nb, 0) + jnp.where(n < max_exact, n, large)
        buckets = (range(REL_BUCKETS), range(nb), (nb - 1,))[kind]
        hits = [(b, bucket == b) for b in buckets]
        for h in range(A_HEADS):
            acc = jnp.zeros((KT, QB), F32)
            for b, hit in hits:
                acc = jnp.where(hit, rb_ref[b, h] * LOG2E, acc)
            o_ref[kind, :, h * QB:(h + 1) * QB] = acc


def bias_tiles(rel_bias):
    return pl.pallas_call(
        _bias_tiles_kernel,
        out_shape=jax.ShapeDtypeStruct((3, KT, A_HEADS * QB), F32),
        in_specs=[pl.BlockSpec(memory_space=pltpu.SMEM)],
        out_specs=pl.BlockSpec(memory_space=pltpu.VMEM),
        name="bias_tiles",
    )(rel_bias)


def _dsa_kernel(qa_ref, qi_ref, wi_ref, c_ref, ka_ref, kb_ref, latg_ref, wuk_ref, wuvt_ref, bias_ref,
                o_ref, cn_ref, ct_ref, keys_ref, hi_ref, s_ref, acc_ref,
                d0_ref, d1_ref, r0_ref, r1_ref, p0_ref, p1_ref, mx_ref, re_ref, *, topk, ntiles):
    i = pl.program_id(1)
    hq = A_HEADS * QB
    nt = i + 1
    last = i

    @pl.when(i == 0)
    def _prep():
        cn = _rms(c_ref[0], latg_ref[...])
        cn_ref[...] = cn.astype(BF16)
        ones = jnp.ones((ACC_ROWS - A_LATENT, KT), BF16)
        for j in range(ntiles):
            ct_ref[j, :A_LATENT, :] = cn[j * KT:(j + 1) * KT, :].T.astype(BF16)
            ct_ref[j, A_LATENT:, :] = ones

    qa = qa_ref[0]
    ql = jnp.concatenate(
        [jnp.dot(qa[:, (h // 2) * LANE:(h // 2 + 1) * LANE], wuk_ref[h],
                 preferred_element_type=F32) for h in range(A_HEADS)], axis=0)
    qlb = (ql * (A_HEAD_DIM ** -0.5 * LOG2E)).astype(BF16)
    qi = qi_ref[0]
    qp = jnp.concatenate([qi[:, p * LANE:(p + 1) * LANE] for p in range(IDX_HEADS // 2)], axis=0)
    w = wi_ref[0] * (IDX_HEADS ** -0.5) * (IDX_DIM ** -0.5)
    wt = w.T

    row = lax.broadcasted_iota(I32, (KT, QB), 0)
    col = lax.broadcasted_iota(I32, (KT, QB), 1)
    chunk_bits = CHUNK.bit_length() - 1
    inadm_last = jnp.where(lax.shift_right_logical(row, chunk_bits)
                           > lax.shift_right_logical(col, chunk_bits), 1, 0)

    def inadmissible(j):
        return (inadm_last * jnp.where(j == last, 1, 0)) != 0

    def rows(j):
        return pl.ds(pl.multiple_of(j * KT, KT), KT)

    def pipelined(prepare, consume, carry):
        prepare(0, 0)

        def pair(t, carry):
            j = 2 * t
            prepare(j + 1, 1)
            carry = consume(j, 0, carry)
            prepare(jnp.minimum(j + 2, last), 0)
            return consume(j + 1, 1, carry)

        carry = lax.fori_loop(0, lax.shift_right_logical(nt, 1), pair, carry)
        return lax.cond((nt & 1) == 1, lambda c: consume(last, 0, c), lambda c: c, carry)

    d_refs = (d0_ref, d1_ref)

    def dots(j, slot):
        ks = rows(j)
        d_ref = d_refs[slot]
        kk = jnp.concatenate([ka_ref[0, ks, :], kb_ref[0, ks, :]], axis=0)
        d_ref[...] = lax.dot_general(kk, qp, NT, preferred_element_type=F32)
        s = lax.dot_general(cn_ref[ks, :], qlb, NT, preferred_element_type=F32)
        s = s + bias_ref[jnp.minimum(i - j, 2)]
        s_ref[j] = s
        mx_ref[j] = s.reshape(KT // 8, 8, hq).max(axis=0)

    def score_keys(j, slot, carry):
        d_ref = d_refs[slot]
        sc = jnp.zeros((KT, QB), F32)
        for h in range(IDX_HEADS):
            d = d_ref[(h % 2) * KT:(h % 2 + 1) * KT, (h // 2) * QB:(h // 2 + 1) * QB]
            sc = sc + jnp.maximum(d, 0.0) * wt[h:h + 1, :]
        sc = jnp.where(inadmissible(j), -jnp.inf, sc)
        bits = pltpu.bitcast(sc, I32)
        key = jnp.where(bits < 0, bits ^ 0x7FFFFFFF, bits)
        key = jnp.where(bits == INT_MIN, 0, key)
        keys_ref[j] = key
        hi_ref[j] = lax.shift_right_arithmetic(key, 16).astype(jnp.int16)
        return carry

    pipelined(dots, score_keys, 0)

    half_min = -2 ** 15

    def select(cap):
        if cap * KT <= topk:
            return jnp.full((1, QB), INT_MIN, I32), jnp.zeros((1, QB), F32)

        def count16(pred):
            acc = jnp.zeros((16, QB), jnp.int16)
            for j in range(cap):
                hit = jnp.where(pred(hi_ref[j]), jnp.int16(1), jnp.int16(0))
                parts = [hit[r * 16:(r + 1) * 16, :] for r in range(KT // 16)]
                while len(parts) > 1:
                    parts = [a + b for a, b in zip(parts[::2], parts[1::2])]
                acc = acc + parts[0]
            return acc.astype(I32).sum(axis=0, keepdims=True)

        def bisect16(target):
            def step_fn(step, lo):
                trial = lo + lax.shift_left(jnp.int32(1), 15 - step)
                t16 = trial.astype(jnp.int16)
                return jnp.where(count16(lambda k: k >= t16) >= target, trial, lo)
            return lax.fori_loop(0, 16, step_fn, jnp.full((1, QB), half_min, I32))

        thr_hi = bisect16(topk)

        def low_halves(j, acc):
            key = keys_ref[j]
            hi = lax.shift_right_arithmetic(key, 16)
            lo = (key & 0xFFFF) + half_min
            hi_ref[j] = jnp.where(hi == thr_hi, lo, half_min).astype(jnp.int16)
            above = jnp.where(hi > thr_hi, 1, 0)
            return acc + above.reshape(KT // 8, 8, QB).sum(axis=0)

        above_hi = lax.fori_loop(0, nt, low_halves, jnp.zeros((8, QB), I32))
        above_hi = above_hi.sum(axis=0, keepdims=True)
        thr_lo = bisect16(topk - above_hi)
        thr_lo16 = thr_lo.astype(jnp.int16)
        above = above_hi + count16(lambda k: k > thr_lo16)
        return lax.shift_left(thr_hi, 16) + (thr_lo - half_min), (topk - above).astype(F32)

    thr, need = lax.switch(nt - 1, [functools.partial(select, c) for c in range(1, ntiles + 1)])

    trow = lax.broadcasted_iota(I32, (KT, KT), 0)
    tcol = lax.broadcasted_iota(I32, (KT, KT), 1)
    tri = jnp.where(tcol < trow, 1.0, 0.0).astype(BF16)

    r_refs = (r0_ref, r1_ref)
    p_refs = (p0_ref, p1_ref)

    def selected(j, key, rank):
        sel = (key > thr) | ((key == thr) & (rank < need))
        return sel & jnp.logical_not(inadmissible(j))

    def weighted(j, slot, carry):
        acc_ref[...] += jnp.dot(ct_ref[j], p_refs[slot][...], preferred_element_type=F32)
        return carry

    def ties_before(j, run_eq):
        re_ref[j] = jnp.broadcast_to(run_eq, (8, QB))
        return run_eq + jnp.where(keys_ref[j] == thr, 1.0, 0.0).sum(axis=0, keepdims=True)

    lax.fori_loop(0, nt, ties_before, jnp.zeros((1, QB), F32))
    m_all = lax.fori_loop(0, nt, lambda j, mx: jnp.maximum(mx, mx_ref[j]),
                          jnp.full((8, hq), NEG, F32)).max(axis=0, keepdims=True)

    def fast_weights(j, slot):
        key = keys_ref[j]
        eqf = jnp.where(key == thr, 1.0, 0.0).astype(BF16)
        rank = jnp.dot(tri, eqf, preferred_element_type=F32) + re_ref[j, 0:1, :]
        keep = jnp.where(selected(j, key, rank), 1.0, 0.0).astype(BF16)
        for h in range(A_HEADS):
            cols = slice(h * QB, (h + 1) * QB)
            p = jnp.exp2(s_ref[j, :, cols] - m_all[:, cols]).astype(BF16)
            p_refs[slot][:, cols] = p * keep

    acc_ref[...] = jnp.zeros(acc_ref.shape, F32)
    pipelined(fast_weights, weighted, 0)

    @pl.when(jnp.min(acc_ref[A_LATENT:A_LATENT + 1, :]) < WEIGHT_SUM_FLOOR)
    def _exact_shift():
        def tie_ranks(j, slot):
            eqf = jnp.where(keys_ref[j] == thr, 1.0, 0.0).astype(BF16)
            r_refs[slot][...] = jnp.dot(tri, eqf, preferred_element_type=F32)

        def mask_tile(j, slot, m8):
            rank = r_refs[slot][...] + re_ref[j, 0:1, :]
            negm = jnp.where(selected(j, keys_ref[j], rank), 0.0, NEG)
            m8_new = []
            for h in range(A_HEADS):
                cols = slice(h * QB, (h + 1) * QB)
                blk = s_ref[j, :, cols] + negm
                s_ref[j, :, cols] = blk
                m8_new.append(jnp.maximum(m8[:, cols], blk.reshape(KT // 8, 8, QB).max(axis=0)))
            return jnp.concatenate(m8_new, axis=1)

        m8 = pipelined(tie_ranks, mask_tile, jnp.full((8, hq), NEG, F32))
        m = m8.max(axis=0, keepdims=True)

        def weights(j, slot):
            p_refs[slot][...] = jnp.exp2(s_ref[j] - m).astype(BF16)

        acc_ref[...] = jnp.zeros(acc_ref.shape, F32)
        pipelined(weights, weighted, 0)

    inv_l = 1.0 / acc_ref[A_LATENT:A_LATENT + 1, :]
    ot = (acc_ref[:A_LATENT, :] * inv_l).astype(BF16)
    for h in range(A_HEADS):
        o_ref[0, h * A_HEAD_DIM:(h + 1) * A_HEAD_DIM, :] = jnp.dot(
            wuvt_ref[h], ot[:, h * QB:(h + 1) * QB], preferred_element_type=F32
        ).astype(o_ref.dtype)


def dsa_attention(proj_h, proj_f, lat_g, w_uk, w_uv, bias, *, topk):
    b, s, _ = proj_h.shape
    ntiles = s // KT
    kern = functools.partial(_dsa_kernel, topk=topk, ntiles=ntiles)
    wuk_t = jnp.swapaxes(w_uk, 1, 2)
    zero = jnp.zeros_like(wuk_t)
    odd = (jnp.arange(A_HEADS) % 2 == 1)[:, None, None]
    wuk_pad = jnp.concatenate([jnp.where(odd, zero, wuk_t), jnp.where(odd, wuk_t, zero)], axis=1)
    col = lambda blk: (lambda bi, i: (bi, 0, blk))
    const = lambda shape: pl.BlockSpec(shape, lambda bi, i: (0,) * len(shape))
    return pl.pallas_call(
        kern,
        out_shape=jax.ShapeDtypeStruct((b, A_HEADS * A_HEAD_DIM, s), BF16),
        grid=(b, s // QB),
        in_specs=[
            pl.BlockSpec((1, QB, 512), lambda bi, i: (bi, i, PROJ_H_QA)),
            pl.BlockSpec((1, QB, 512), lambda bi, i: (bi, i, PROJ_H_QI)),
            pl.BlockSpec((1, QB, LANE), lambda bi, i: (bi, i, PROJ_F_WI)),
            pl.BlockSpec((1, s, LANE), col(PROJ_F_C)),
            pl.BlockSpec((1, s, LANE), col(PROJ_H_KI_LO)),
            pl.BlockSpec((1, s, LANE), col(PROJ_H_KI_HI)),
            const((1, A_LATENT)),
            const((A_HEADS, LANE, A_LATENT)),
            const((A_HEADS, A_HEAD_DIM, A_LATENT)),
            pl.BlockSpec((3, KT, A_HEADS * QB), lambda bi, i: (0, 0, 0),
                         pipeline_mode=pl.Buffered(1)),
        ],
        out_specs=pl.BlockSpec((1, A_HEADS * A_HEAD_DIM, QB), lambda bi, i: (bi, 0, i)),
        scratch_shapes=[
            pltpu.VMEM((s, A_LATENT), BF16),
            pltpu.VMEM((ntiles, ACC_ROWS, KT), BF16),
            pltpu.VMEM((ntiles, KT, QB), I32),
            pltpu.VMEM((ntiles, KT, QB), jnp.int16),
            pltpu.VMEM((ntiles, KT, A_HEADS * QB), F32),
            pltpu.VMEM((ACC_ROWS, A_HEADS * QB), F32),
            pltpu.VMEM((2 * KT, (IDX_HEADS // 2) * QB), F32),
            pltpu.VMEM((2 * KT, (IDX_HEADS // 2) * QB), F32),
            pltpu.VMEM((KT, QB), F32),
            pltpu.VMEM((KT, QB), F32),
            pltpu.VMEM((KT, A_HEADS * QB), BF16),
            pltpu.VMEM((KT, A_HEADS * QB), BF16),
            pltpu.VMEM((ntiles, 8, A_HEADS * QB), F32),
            pltpu.VMEM((ntiles, 8, QB), F32),
        ],
        compiler_params=_params(("parallel", "arbitrary")),
        name="dsa_attention",
    )(proj_h, proj_h, proj_f, proj_f, proj_h, proj_h, lat_g.reshape(1, A_LATENT),
      wuk_pad.astype(BF16), jnp.swapaxes(w_uv, 1, 2).astype(BF16), bias)


def _hgrn_kernel(f_ref, v_ref, q_ref, g_ref, lb_ref, og_ref, o_ref, st_ref, *, nchunk):
    @pl.when(pl.program_id(1) == 0)
    def _():
        st_ref[...] = jnp.zeros(st_ref.shape, F32)

    lb = lb_ref[...]
    og = og_ref[...]
    row = lax.broadcasted_iota(I32, (CHUNK, CHUNK), 0)
    col = lax.broadcasted_iota(I32, (CHUNK, CHUNK), 1)
    causal = col <= row
    tri = jnp.where(causal, 1.0, 0.0).astype(BF16)
    hc = CHUNK // 2
    top = lax.broadcasted_iota(I32, (CHUNK, B_HEADS * B_DK), 0) < hc
    zeros_h = jnp.zeros((hc, B_DK), BF16)

    for c in range(nchunk):
        rows = slice(c * CHUNK, (c + 1) * CHUNK)
        f = lb + (1.0 - lb) * jax.nn.sigmoid(f_ref[0, rows, :])
        g = jnp.log(f)
        k = 1.0 - f
        q = q_ref[0, rows, :]
        qf = q * jax.nn.sigmoid(q)
        vb = v_ref[0, rows, :]
        g_hi = g.astype(BF16)
        g_lo = (g - g_hi.astype(F32)).astype(BF16)
        bc = (jnp.dot(tri, g_hi, preferred_element_type=F32)
              + jnp.dot(tri, g_lo, preferred_element_type=F32))
        b_end = bc[CHUNK - 1:CHUNK, :]
        b_half = bc[hc - 1:hc, :]
        e_in = bc - jnp.where(top, bc[hc // 2 - 1:hc // 2, :], bc[hc + hc // 2 - 1:hc + hc // 2, :])
        q_in = (qf * jnp.exp(e_in)).astype(BF16)
        k_in = (k * jnp.exp(-e_in)).astype(BF16)
        q_x = (qf[hc:, :] * jnp.exp(bc[hc:, :] - b_half)).astype(BF16)
        k_x = (k[:hc, :] * jnp.exp(b_half - bc[:hc, :])).astype(BF16)
        qb = (qf * jnp.exp(bc)).astype(BF16)
        kd = (k * jnp.exp(b_end - bc)).astype(BF16)
        dec = jnp.exp(b_end)
        gate = g_ref[0, rows, :]
        gate = gate * jax.nn.sigmoid(gate)
        for h in range(B_HEADS):
            sl = slice(h * B_DK, (h + 1) * B_DK)
            q3 = jnp.concatenate([
                jnp.concatenate([q_in[:hc, sl], zeros_h, zeros_h], axis=1),
                jnp.concatenate([zeros_h, q_x[:, sl], q_in[hc:, sl]], axis=1)], axis=0)
            k3 = jnp.concatenate([
                jnp.concatenate([k_in[:hc, sl], k_x[:, sl], zeros_h], axis=1),
                jnp.concatenate([zeros_h, zeros_h, k_in[hc:, sl]], axis=1)], axis=0)
            sc = lax.dot_general(q3, k3, NT, preferred_element_type=F32)
            sc = jnp.where(causal, sc, 0.0).astype(BF16)
            st = st_ref[h]
            o = (jnp.dot(sc, vb[:, sl], preferred_element_type=F32)
                 + lax.dot_general(qb[:, sl], st.astype(BF16), NT, preferred_element_type=F32))
            st_ref[h] = st * dec[:, sl] + lax.dot_general(vb[:, sl], kd[:, sl], TN,
                                                          preferred_element_type=F32)
            o_ref[0, rows, sl] = (_rms(o, og) * gate[:, sl]).astype(o_ref.dtype)


def hgrn2(proj_h, proj_f, lb, o_g, *, rows):
    b, s, _ = proj_h.shape
    width = B_HEADS * B_DK
    kern = functools.partial(_hgrn_kernel, nchunk=rows // CHUNK)
    spec = lambda blk: pl.BlockSpec((1, rows, width), lambda bi, r: (bi, r, blk))
    return pl.pallas_call(
        kern,
        out_shape=jax.ShapeDtypeStruct((b, s, width), BF16),
        grid=(b, s // rows),
        in_specs=[spec(PROJ_F_F), spec(PROJ_H_V), spec(PROJ_F_Q), spec(PROJ_F_GATE),
                  pl.BlockSpec((1, width), lambda bi, r: (0, 0)),
                  pl.BlockSpec((1, B_DK), lambda bi, r: (0, 0))],
        out_specs=pl.BlockSpec((1, rows, width), lambda bi, r: (bi, r, 0)),
        scratch_shapes=[pltpu.VMEM((B_HEADS, B_DK, B_DK), F32)],
        compiler_params=_params(("parallel", "arbitrary")),
        name="hgrn2",
    )(proj_f, proj_h, proj_f, proj_f, lb.reshape(1, width), o_g.reshape(1, B_DK))


def _gmlp_kernel(h_ref, g_ref, win_ref, lng_ref, lnb_ref, wsp_ref, bsp_ref, wout_ref, o_ref,
                 gated_ref, *, nchunk, nsplit):
    width = h_ref.shape[2]
    gw = width // GM_GROUPS
    row = lax.broadcasted_iota(I32, (GM_CHUNK, GM_CHUNK), 0)
    col = lax.broadcasted_iota(I32, (GM_CHUNK, GM_CHUNK), 1)
    wgs = [jnp.where(col <= row, wsp_ref[gi], 0.0).astype(BF16) for gi in range(GM_GROUPS)]
    nc = nchunk // nsplit
    xs = [h_ref[0, sp * nc * GM_CHUNK:(sp + 1) * nc * GM_CHUNK, :] for sp in range(nsplit)]
    uvs = [jnp.dot(_rms(x, g_ref[...]).astype(BF16), win_ref[...], preferred_element_type=F32)
           for x in xs]
    for sp in range(nsplit):
        r0 = sp * nc * GM_CHUNK
        x, uv = xs[sp], uvs[sp]
        uv = 0.5 * uv * (1.0 + lax.erf(uv * (0.5 ** 0.5)))
        u = uv[:, :width]
        v = uv[:, width:]
        mu = jnp.mean(v, axis=-1, keepdims=True)
        vc = v - mu
        var = jnp.mean(vc * vc, axis=-1, keepdims=True)
        vn = (vc * lax.rsqrt(var + EPS) * lng_ref[...] + lnb_ref[...]).astype(BF16)
        for gi in range(GM_GROUPS):
            cs = slice(gi * gw, (gi + 1) * gw)
            vg = jnp.concatenate([vn[c * GM_CHUNK:(c + 1) * GM_CHUNK, cs] for c in range(nc)], axis=1)
            mixed = jnp.dot(wgs[gi], vg, preferred_element_type=F32)
            for c in range(nc):
                rs = slice(c * GM_CHUNK, (c + 1) * GM_CHUNK)
                mc = mixed[:, c * gw:(c + 1) * gw] + bsp_ref[:, cs]
                gated_ref[r0 + c * GM_CHUNK:r0 + (c + 1) * GM_CHUNK, cs] = (u[rs, cs] * mc).astype(BF16)
        o_ref[0, r0:r0 + nc * GM_CHUNK, :] = x + jnp.dot(
            gated_ref[r0:r0 + nc * GM_CHUNK, :], wout_ref[...], preferred_element_type=F32)


def gmlp(h, g, w_in, ln_g, ln_b, w_sp, b_sp, w_out, *, rows):
    b, s, d = h.shape
    width = w_out.shape[0]
    gw = width // GM_GROUPS
    bsp_full = jnp.repeat(jnp.transpose(b_sp), gw, axis=1)
    kern = functools.partial(_gmlp_kernel, nchunk=rows // GM_CHUNK, nsplit=4)
    const = lambda shape: pl.BlockSpec(shape, lambda bi, r: (0,) * len(shape),
                                       pipeline_mode=pl.Buffered(1))
    return pl.pallas_call(
        kern,
        out_shape=jax.ShapeDtypeStruct((b, s, d), F32),
        grid=(b, s // rows),
        in_specs=[pl.BlockSpec((1, rows, d), lambda bi, r: (bi, r, 0)),
                  const((1, d)), const((d, 2 * width)), const((1, width)), const((1, width)),
                  const((GM_GROUPS, GM_CHUNK, GM_CHUNK)), const((GM_CHUNK, width)),
                  const((width, d))],
        out_specs=pl.BlockSpec((1, rows, d), lambda bi, r: (bi, r, 0)),
        scratch_shapes=[pltpu.VMEM((rows, width), BF16)],
        compiler_params=_params(("parallel", "parallel")),
        name="gmlp",
    )(h, g.reshape(1, d), w_in.astype(BF16), ln_g.reshape(1, width), ln_b.reshape(1, width),
      w_sp, bsp_full, w_out.astype(BF16))


def _xattn_kernel(*refs, mixed):
    if mixed:
        h_ref, at_ref, b_ref, wa_ref, wb_ref, g_ref, wq_ref, k_ref, v_ref, wo_ref, o_ref = refs
        x = (h_ref[0] + lax.dot_general(at_ref[0], wa_ref[...], TN, preferred_element_type=F32)
             + jnp.dot(b_ref[0], wb_ref[...], preferred_element_type=F32))
    else:
        h_ref, g_ref, wq_ref, k_ref, v_ref, wo_ref, o_ref = refs
        x = h_ref[0]
    xn = _rms(x, g_ref[...]).astype(BF16)
    q = jnp.dot(xn, wq_ref[...], preferred_element_type=F32)
    q = (q * (X_HEAD_DIM ** -0.5 * LOG2E)).astype(BF16)
    ones = jnp.ones((k_ref.shape[1], X_HEAD_DIM), BF16)
    outs = []
    for h in range(X_HEADS):
        sl = slice(h * X_HEAD_DIM, (h + 1) * X_HEAD_DIM)
        s = lax.dot_general(q[:, sl], k_ref[0, :, sl], NT, preferred_element_type=F32)
        p = jnp.exp2(s - s.max(axis=-1, keepdims=True)).astype(BF16)
        ol = jnp.dot(p, jnp.concatenate([v_ref[0, :, sl], ones], axis=1),
                     preferred_element_type=F32)
        outs.append((ol[:, :X_HEAD_DIM] * (1.0 / ol[:, X_HEAD_DIM:X_HEAD_DIM + 1])).astype(BF16))
    att = jnp.concatenate(outs, axis=1)
    o_ref[0] = x + jnp.dot(att, wo_ref[...], preferred_element_type=F32)


def xattn(h, g, wq, kv, wo, layer, *, rows, mix=None):
    b, s, d = h.shape
    m = kv.shape[1]
    xw = wq.shape[2]
    const = lambda shape: pl.BlockSpec(shape, lambda bi, r: (0,) * len(shape))
    of_layer = lambda shape: pl.BlockSpec((None,) + shape, lambda bi, r: (layer, 0, 0))
    rows_spec = lambda width: pl.BlockSpec((1, rows, width), lambda bi, r: (bi, r, 0))
    mix_specs, mix_args = [], []
    if mix is not None:
        a_t, b_out, w_mix = mix
        ka, kb = a_t.shape[1], b_out.shape[2]
        assert ka == kb and w_mix.shape[0] == ka + kb
        mix_specs = [pl.BlockSpec((1, ka, rows), lambda bi, r: (bi, 0, r)), rows_spec(kb),
                     pl.BlockSpec((ka, d), lambda bi, r: (0, 0)),
                     pl.BlockSpec((kb, d), lambda bi, r: (1, 0))]
        mix_args = [a_t, b_out, w_mix, w_mix]
    return pl.pallas_call(
        functools.partial(_xattn_kernel, mixed=mix is not None),
        out_shape=jax.ShapeDtypeStruct((b, s, d), F32),
        grid=(b, s // rows),
        in_specs=[rows_spec(d), *mix_specs,
                  const((1, d)), of_layer((d, xw)),
                  pl.BlockSpec((1, m, xw), lambda bi, r: (bi, 0, 0)),
                  pl.BlockSpec((1, m, xw), lambda bi, r: (bi, 0, 1)),
                  of_layer((xw, d))],
        out_specs=rows_spec(d),
        compiler_params=_params(("parallel", "parallel")),
        name="xattn",
    )(h, *mix_args, g.reshape(1, d), wq, kv, kv, wo)


def _swiglu_kernel(x_ref, g_ref, wgu_ref, wd_ref, fg_ref, o_ref, a_ref, *, ff, fchunk, final):
    x = x_ref[...]
    xn = _rms(x, g_ref[...]).astype(BF16)
    for c in range(ff // fchunk):
        gate = jnp.dot(xn, wgu_ref[:, c * fchunk:(c + 1) * fchunk], preferred_element_type=F32)
        up = jnp.dot(xn, wgu_ref[:, ff + c * fchunk:ff + (c + 1) * fchunk],
                     preferred_element_type=F32)
        a_ref[:, c * fchunk:(c + 1) * fchunk] = (gate * jax.nn.sigmoid(gate) * up).astype(BF16)
    y = x + jnp.dot(a_ref[...], wd_ref[...], preferred_element_type=F32)
    if final:
        y = _rms(y, fg_ref[...])
    o_ref[...] = y


def swiglu(x, g, w_gu, w_down, layer, final_g, *, tm, final):
    n, d = x.shape
    ff = w_down.shape[1]
    kern = functools.partial(_swiglu_kernel, ff=ff, fchunk=256, final=final)
    once = dict(pipeline_mode=pl.Buffered(1))
    return pl.pallas_call(
        kern,
        out_shape=jax.ShapeDtypeStruct((n, d), F32),
        grid=(n // tm,),
        in_specs=[pl.BlockSpec((tm, d), lambda i: (i, 0)),
                  pl.BlockSpec((1, d), lambda i: (0, 0)),
                  pl.BlockSpec((None, d, 2 * ff), lambda i: (layer, 0, 0), **once),
                  pl.BlockSpec((None, ff, d), lambda i: (layer, 0, 0), **once),
                  pl.BlockSpec((1, d), lambda i: (0, 0))],
        out_specs=pl.BlockSpec((tm, d), lambda i: (i, 0)),
        scratch_shapes=[pltpu.VMEM((tm, ff), BF16)],
        compiler_params=_params(("parallel",)),
        name="swiglu",
    )(x, g.reshape(1, d), w_gu, w_down, final_g.reshape(1, d))


def _pack_in_proj(w):
    sizes = [A_HEADS * A_HEAD_DIM, A_LATENT, IDX_HEADS * IDX_DIM, IDX_DIM, IDX_HEADS,
             B_HEADS * B_DK, B_HEADS * B_DK, B_HEADS * B_DK, B_HEADS * B_DK]
    offs = [0]
    for sz in sizes:
        offs.append(offs[-1] + sz)
    q_a, c, qi, ki, wi, f_b, i_b, q_b, g_b = [w[:, offs[n]:offs[n + 1]] for n in range(9)]
    z = lambda n: jnp.zeros((w.shape[0], n), w.dtype)
    return jnp.concatenate([q_a, qi, i_b, ki, z(LANE - IDX_DIM), z(LANE - IDX_DIM), ki,
                            f_b, q_b, g_b, c, wi, z(LANE - IDX_HEADS)], axis=1)


def kernel(x, mem, rel_bias, hgrn_lb, mix_norm, e_w_in, e_lat_norm, e_w_uk, e_w_uv, e_o_norm, e_w_out, o_w_in, o_ln_g, o_ln_b, o_w_sp, o_b_sp, o_w_out, x_norm, mem_norm, x_wq, x_wkv, x_wo, f_norm, f_w_gu, f_w_down, final_norm):
    b, s, d = x.shape
    m = mem.shape[1]
    n = b * s
    depth = mix_norm.shape[0]
    topk = min(TOPK_MAX, s // 4)
    lb_all = jnp.cumsum(jax.nn.softmax(hgrn_lb.astype(F32), axis=0), axis=0)
    bias = bias_tiles(rel_bias)
    mem2 = mem.reshape(b * m, d)
    wq_all, wo_all = x_wq.astype(BF16), x_wo.astype(BF16)
    w_gu_all, w_down_all = f_w_gu.astype(BF16), f_w_down.astype(BF16)

    h = x.reshape(n, d)
    for l in range(depth):
        j = l // 2
        mix = None
        if l % 2 == 0:
            proj_h, proj_f = norm_matmul(h, mix_norm[l], _pack_in_proj(e_w_in[j].astype(BF16)),
                                         tm=1024, tn=256,
                                         groups=((PROJ_H_COLS, BF16), (PROJ_F_COLS, F32)))
            proj_h = proj_h.reshape(b, s, -1)
            proj_f = proj_f.reshape(b, s, -1)
            a_out = dsa_attention(proj_h, proj_f, e_lat_norm[j], e_w_uk[j], e_w_uv[j], bias,
                                  topk=topk)
            b_out = hgrn2(proj_h, proj_f, lb_all[l], e_o_norm[j], rows=1024)
            mix = (a_out, b_out, e_w_out[j].astype(BF16))
        else:
            h = gmlp(h.reshape(b, s, d), mix_norm[l], o_w_in[j], o_ln_g[j], o_ln_b[j], o_w_sp[j],
                     o_b_sp[j], o_w_out[j], rows=1024).reshape(n, d)
        (kv,) = norm_matmul(mem2, mem_norm[l], x_wkv[l].astype(BF16), tm=512, tn=512,
                            groups=((x_wkv.shape[2], BF16),))
        kv = kv.reshape(b, m, -1)
        h = xattn(h.reshape(b, s, d), x_norm[l], wq_all, kv, wo_all, l, rows=1024,
                  mix=mix).reshape(n, d)
        h = swiglu(h, f_norm[l], w_gu_all, w_down_all, l, final_norm, tm=1024,
                   final=(l == depth - 1))
    return h.reshape(b, s, d)
```
